```python
import jax, jax.numpy as jnp
from jax import lax
import numpy as np

D_MODEL = 2048
BATCH = 16
SEQ = 256
DEPTH = 1
DEC_BATCH = 4
DEC_SEQ = 2048
PAST_LEN = 256

GRID_W = 64
D_FF = 5504
N_MOD = 9
GLA_HEADS = 4
GLA_DK = 256
GLA_DV = 512
GLA_QK_DIM = 1024
GLA_V_DIM = 2048
GLA_GATE_RANK = 16
GLA_TAU = 16.0
GLA_CHUNK = 64
MLA_HEADS = 16
MLA_NOPE = 128
MLA_ROPE = 64
MLA_V = 128
Q_LORA = 512
KV_LORA = 512
ROPE_THETA = 10000.0
Q_BLOCK = 128
NORM_EPS = 1e-6
PROJ_SPLITS = (1024, 1024, 2048, 2048, 16, 16, 512, 512, 64, 2048, 2048)
D_IN_PROJ = 11360

kernel_name = "hybrid_gla_mla_diffusion_step"


def rms_norm(x, g):
    xf = x.astype(jnp.float32)
    y = xf * lax.rsqrt(jnp.mean(xf * xf, axis=-1, keepdims=True) + NORM_EPS)
    return (y * g.astype(jnp.float32)).astype(x.dtype)


def swiglu(h, w_in, w_out):
    g, u = jnp.split(h @ w_in, 2, axis=-1)
    return (jax.nn.silu(g) * u) @ w_out


def axial_rope_tables(n_tokens):
    rows = n_tokens // GRID_W
    row = jnp.repeat(jnp.arange(rows), GRID_W).astype(jnp.float32)
    col = jnp.tile(jnp.arange(GRID_W), rows).astype(jnp.float32)
    half = MLA_ROPE // 2
    inv_freq = ROPE_THETA ** (-jnp.arange(0, half, 2, dtype=jnp.float32) / half)
    ang = jnp.stack([row[:, None] * inv_freq, col[:, None] * inv_freq], axis=1)
    ang = ang[:, :, None, :]
    return jnp.cos(ang), jnp.sin(ang)


def apply_axial_rope(x, cos, sin):
    xp = x.astype(jnp.float32).reshape(x.shape[:-1] + (2, 2, MLA_ROPE // 4))
    x1, x2 = xp[..., 0:1, :], xp[..., 1:2, :]
    out = jnp.concatenate([x1 * cos - x2 * sin, x2 * cos + x1 * sin], axis=-2)
    return out.reshape(x.shape).astype(x.dtype)


def gla_chunked(q, k, v, log_a, s0):
    B, T, H, K = q.shape
    V = v.shape[-1]
    n = T // GLA_CHUNK
    f32 = jnp.float32

    def chunks(t):
        return t.astype(f32).reshape(B, n, GLA_CHUNK, H, t.shape[-1])

    qc, kc, vc = chunks(q), chunks(k), chunks(v)
    b = jnp.cumsum(chunks(log_a), axis=2)
    b_last = b[:, :, -1:]
    q_dec = qc * jnp.exp(b)
    k_inv = kc * jnp.exp(-b)
    k_end = kc * jnp.exp(b_last - b)
    lower = jnp.tril(jnp.ones((GLA_CHUNK, GLA_CHUNK), dtype=bool))
    a = jnp.einsum("bnihk,bnjhk->bnhij", q_dec, k_inv)
    a = jnp.where(lower, a, 0.0)
    o_intra = jnp.einsum("bnhij,bnjhv->bnihv", a, vc)

    def step(S, xs):
        q_n, k_n, v_n, dec_n = xs
        o_n = jnp.einsum("bihk,bhkv->bihv", q_n, S)
        S = jnp.exp(dec_n)[..., None] * S + jnp.einsum("bjhk,bjhv->bhkv", k_n, v_n)
        return S, o_n

    xs = (q_dec.swapaxes(0, 1), k_end.swapaxes(0, 1), vc.swapaxes(0, 1), b_last[:, :, 0].swapaxes(0, 1))
    s_fin, o_inter = lax.scan(step, s0.astype(f32), xs)
    o = o_intra + o_inter.swapaxes(0, 1)
    return o.reshape(B, T, H, V).astype(v.dtype), s_fin.astype(s0.dtype)


def mla_attention(q_nope, q_rope, k_nope, k_rope, v):
    B, T, H, _ = q_nope.shape
    nb = T // Q_BLOCK
    scale = (MLA_NOPE + MLA_ROPE) ** -0.5
    qn = q_nope.reshape(B, nb, Q_BLOCK, H, MLA_NOPE).swapaxes(0, 1)
    qr = q_rope.reshape(B, nb, Q_BLOCK, H, MLA_ROPE).swapaxes(0, 1)

    def block(args):
        qn_b, qr_b = args
        s = jnp.einsum("bqhd,bkhd->bhqk", qn_b, k_nope) + jnp.einsum("bqhr,bkr->bhqk", qr_b, k_rope)
        p = jax.nn.softmax(s.astype(jnp.float32) * scale, axis=-1)
        return jnp.einsum("bhqk,bkhd->bqhd", p.astype(v.dtype), v)

    o = lax.map(block, (qn, qr))
    return o.swapaxes(0, 1).reshape(B, T, H * MLA_V)


def token_mixer(h, ctx, w_in, w_gla_alpha, b_gla_alpha, gla_norm, w_gla_out,
                q_norm, kv_norm, w_uq, w_ukv, w_mla_out, w_out):
    B, T, _ = h.shape
    proj = h @ w_in
    (q_g, k_g, v_g, r_g, a_f, a_b, cq, ckv, k_rope, g_a, g_b) = jnp.split(
        proj, np.cumsum(PROJ_SPLITS)[:-1].tolist(), axis=-1)

    q_g = q_g.reshape(B, T, GLA_HEADS, GLA_DK) * (GLA_DK ** -0.5)
    k_g = k_g.reshape(B, T, GLA_HEADS, GLA_DK)
    v_g = v_g.reshape(B, T, GLA_HEADS, GLA_DV)
    la_f = (jax.nn.log_sigmoid((a_f @ w_gla_alpha[0] + b_gla_alpha[0]).astype(jnp.float32)) / GLA_TAU
            ).reshape(B, T, GLA_HEADS, GLA_DK)
    la_b = (jax.nn.log_sigmoid((a_b @ w_gla_alpha[1] + b_gla_alpha[1]).astype(jnp.float32)) / GLA_TAU
            ).reshape(B, T, GLA_HEADS, GLA_DK)

    cq = rms_norm(cq, q_norm)
    q = (cq @ w_uq).reshape(B, T, MLA_HEADS, MLA_NOPE + MLA_ROPE)
    q_nope, q_rope = q[..., :MLA_NOPE], q[..., MLA_NOPE:]
    ckv = rms_norm(ckv, kv_norm)

    if ctx is None:
        s0_f = jnp.zeros((B, GLA_HEADS, GLA_DK, GLA_DV), h.dtype)
        s0_b = s0_f
        ckv_all, krope_all = ckv, k_rope
    else:
        ckv_ctx, krope_ctx, s0_f, s0_b = ctx
        cos, sin = axial_rope_tables(T)
        q_rope = apply_axial_rope(q_rope, cos[:, None], sin[:, None])
        k_rope = apply_axial_rope(k_rope, cos, sin)
        ckv_all = jnp.concatenate([ckv_ctx.astype(h.dtype), ckv], axis=1)
        krope_all = jnp.concatenate([krope_ctx.astype(h.dtype), k_rope], axis=1)

    kv = (ckv_all @ w_ukv).reshape(B, ckv_all.shape[1], MLA_HEADS, MLA_NOPE + MLA_V)
    k_nope, v_m = kv[..., :MLA_NOPE], kv[..., MLA_NOPE:]
    y_mla = mla_attention(q_nope, q_rope, k_nope, krope_all, v_m) @ w_mla_out

    o_f, s_f = gla_chunked(q_g, k_g, v_g, la_f, s0_f)
    o_b, s_b = gla_chunked(jnp.flip(q_g, 1), jnp.flip(k_g, 1), jnp.flip(v_g, 1), jnp.flip(la_b, 1), s0_b)
    o = rms_norm(o_f + jnp.flip(o_b, 1), gla_norm).reshape(B, T, GLA_V_DIM) * jax.nn.silu(r_g)
    y_gla = o @ w_gla_out

    y = (jax.nn.sigmoid(g_a) * y_gla + jax.nn.sigmoid(g_b) * y_mla) @ w_out
    return y, (ckv, k_rope, s_f, s_b)


def trunk_layer(x, mod, ctx, norm_gains, w_ffn1_in, w_ffn1_out, w_ffn2_in, w_ffn2_out, mix_w):
    sh1, sc1, gt1, sh2, sc2, gt2, sh3, sc3, gt3 = jnp.split(mod, N_MOD, axis=-1)
    h = rms_norm(x, norm_gains[0]) * (1 + sc1) + sh1
    x = x + 0.5 * gt1 * rms_norm(swiglu(h, w_ffn1_in, w_ffn1_out), norm_gains[1])
    h = rms_norm(x, norm_gains[2]) * (1 + sc2) + sh2
    y, ctx_out = token_mixer(h, ctx, *mix_w)
    x = x + gt2 * rms_norm(y, norm_gains[3])
    h = rms_norm(x, norm_gains[4]) * (1 + sc3) + sh3
    x = x + 0.5 * gt3 * rms_norm(swiglu(h, w_ffn2_in, w_ffn2_out), norm_gains[5])
    return x, ctx_out


def setup_inputs(seed: int = 0) -> dict:
    key = jax.random.key(seed)
    ks = jax.random.split(key, 32)
    f32 = jnp.float32

    def nrm(k, shape, scale):
        return jax.random.normal(k, shape, f32) * scale

    H_QK = MLA_NOPE + MLA_ROPE
    return {
        "x_prompt": nrm(ks[0], (BATCH, SEQ, D_MODEL), 1.0),
        "x_sample": nrm(ks[1], (DEC_BATCH, DEC_SEQ, D_MODEL), 1.0),
        "cache_ckv": nrm(ks[2], (DEC_BATCH, DEPTH, PAST_LEN, KV_LORA), 1.0),
        "cache_krope": nrm(ks[3], (DEC_BATCH, DEPTH, PAST_LEN, MLA_ROPE), 1.0),
        "state_gla_fwd": nrm(ks[4], (DEC_BATCH, DEPTH, GLA_HEADS, GLA_DK, GLA_DV), 0.5),
        "state_gla_bwd": nrm(ks[5], (DEC_BATCH, DEPTH, GLA_HEADS, GLA_DK, GLA_DV), 0.5),
        "c": nrm(ks[6], (DEC_BATCH, D_MODEL), 1.0),
        "c_ctx": nrm(ks[7], (D_MODEL,), 1.0),
        "w_ada": nrm(ks[8], (DEPTH, D_MODEL, N_MOD * D_MODEL), 0.5 * D_MODEL ** -0.5),
        "b_ada": nrm(ks[9], (DEPTH, N_MOD * D_MODEL), 0.01),
        "norm_gains": 1.0 + nrm(ks[10], (DEPTH, 6, D_MODEL), 0.05),
        "w_ffn1_in": nrm(ks[11], (DEPTH, D_MODEL, 2 * D_FF), D_MODEL ** -0.5),
        "w_ffn1_out": nrm(ks[12], (DEPTH, D_FF, D_MODEL), D_FF ** -0.5),
        "w_ffn2_in": nrm(ks[13], (DEPTH, D_MODEL, 2 * D_FF), D_MODEL ** -0.5),
        "w_ffn2_out": nrm(ks[14], (DEPTH, D_FF, D_MODEL), D_FF ** -0.5),
        "w_in": nrm(ks[15], (DEPTH, D_MODEL, D_IN_PROJ), D_MODEL ** -0.5),
        "w_gla_alpha": nrm(ks[16], (DEPTH, 2, GLA_GATE_RANK, GLA_QK_DIM), GLA_GATE_RANK ** -0.5),
        "b_gla_alpha": nrm(ks[17], (DEPTH, 2, GLA_QK_DIM), 0.1),
        "gla_norm": 1.0 + nrm(ks[18], (DEPTH, GLA_DV), 0.05),
        "w_gla_out": nrm(ks[19], (DEPTH, GLA_V_DIM, D_MODEL), GLA_V_DIM ** -0.5),
        "q_norm": 1.0 + nrm(ks[20], (DEPTH, Q_LORA), 0.05),
        "kv_norm": 1.0 + nrm(ks[21], (DEPTH, KV_LORA), 0.05),
        "w_uq": nrm(ks[22], (DEPTH, Q_LORA, MLA_HEADS * H_QK), Q_LORA ** -0.5),
        "w_ukv": nrm(ks[23], (DEPTH, KV_LORA, MLA_HEADS * (MLA_NOPE + MLA_V)), KV_LORA ** -0.5),
        "w_mla_out": nrm(ks[24], (DEPTH, MLA_HEADS * MLA_V, D_MODEL), (MLA_HEADS * MLA_V) ** -0.5),
        "w_out": nrm(ks[25], (DEPTH, D_MODEL, D_MODEL), D_MODEL ** -0.5),
    }


def reference(x_prompt, x_sample, cache_ckv, cache_krope, state_gla_fwd, state_gla_bwd, c, c_ctx,
              w_ada, b_ada, norm_gains, w_ffn1_in, w_ffn1_out, w_ffn2_in, w_ffn2_out,
              w_in, w_gla_alpha, b_gla_alpha, gla_norm, w_gla_out,
              q_norm, kv_norm, w_uq, w_ukv, w_mla_out, w_out):
    xp, xs = x_prompt, x_sample
    new_ckv, new_krope, new_sf, new_sb = [], [], [], []
    for l in range(DEPTH):
        mix_w = (w_in[l], w_gla_alpha[l], b_gla_alpha[l], gla_norm[l], w_gla_out[l],
                 q_norm[l], kv_norm[l], w_uq[l], w_ukv[l], w_mla_out[l], w_out[l])
        ffn_w = (norm_gains[l], w_ffn1_in[l], w_ffn1_out[l], w_ffn2_in[l], w_ffn2_out[l])
        mod_ctx = (jax.nn.silu(c_ctx) @ w_ada[l] + b_ada[l])[None, None, :]
        mod_lat = (jax.nn.silu(c) @ w_ada[l] + b_ada[l])[:, None, :]
        xp, (ckv, krope, s_f, s_b) = trunk_layer(xp, mod_ctx, None, *ffn_w, mix_w)
        new_ckv.append(ckv)
        new_krope.append(krope)
        new_sf.append(s_f)
        new_sb.append(s_b)
        ctx = (cache_ckv[:, l], cache_krope[:, l], state_gla_fwd[:, l], state_gla_bwd[:, l])
        xs, _ = trunk_layer(xs, mod_lat, ctx, *ffn_w, mix_w)
    return (xp, xs, jnp.stack(new_ckv, axis=1), jnp.stack(new_krope, axis=1),
            jnp.stack(new_sf, axis=1), jnp.stack(new_sb, axis=1))
```

```python
import functools

import numpy as np
import jax
import jax.numpy as jnp
from jax import lax
from jax.experimental import pallas as pl
from jax.experimental.pallas import tpu as pltpu

F32 = jnp.float32
BF16 = jnp.bfloat16

D_MODEL = 2048
BATCH, SEQ = 16, 256
DEC_BATCH, DEC_SEQ = 4, 2048
PAST_LEN = 256
GRID_W = 64
D_FF = 5504
N_MOD = 9
GLA_HEADS, GLA_DK, GLA_DV = 4, 256, 512
GLA_RANK = 16
GLA_TAU = 16.0
GLA_CHUNK = 64
MLA_HEADS, MLA_NOPE, MLA_ROPE, MLA_V = 16, 128, 64, 128
Q_LORA = KV_LORA = 512
ROPE_THETA = 10000.0
NORM_EPS = 1e-6

N_CTX = BATCH * SEQ
N_LAT = DEC_BATCH * DEC_SEQ
N_TOK = N_CTX + N_LAT

LANES = 128
V7X_VMEM_LIMIT_BYTES = 60000 * 1024

FF_TILE = 512
FF_PAD = -(-D_FF // FF_TILE) * FF_TILE
N_FF = FF_PAD // FF_TILE
FFN_TM = 512
PROJ_TM, PROJ_TN = 1024, 1024
MLA_TM = 512
MERGE_TM = 256
ATT_TQ = 512
HEAD_W = 2 * LANES

COL_Q, COL_K, COL_V, COL_R = 0, 1024, 2048, 4096
COL_GA, COL_GB, COL_CQ, COL_CKV = 6144, 8192, 10240, 10752
PROJ_MAIN = 11264
PROJ_SMALL = 256


def _mod_row(i, tm):
    n_ctx_tiles = N_CTX // tm
    return jnp.where(i < n_ctx_tiles, 0, 1 + (i - n_ctx_tiles) // (DEC_SEQ // tm))


def _rms(x, gain):
    ms = jnp.mean(x * x, axis=-1, keepdims=True)
    return x * lax.rsqrt(ms + NORM_EPS) * gain


def _silu(x):
    return x * jax.nn.sigmoid(x)


def _cparams(n_axes):
    return pltpu.CompilerParams(dimension_semantics=("arbitrary",) * n_axes,
                                vmem_limit_bytes=V7X_VMEM_LIMIT_BYTES)


def _resident(shape):
    nd = len(shape)
    return pl.BlockSpec(shape, lambda *_: (0,) * nd, pipeline_mode=pl.Buffered(1))


def _mod_kernel(c_ref, w_ref, b_ref, o_ref):
    s = _silu(c_ref[...]).astype(BF16)
    o_ref[...] = jnp.dot(s, w_ref[...].astype(BF16), preferred_element_type=F32) + b_ref[...]


def _modulation(c_all, w_ada, b_ada):
    n = w_ada.shape[1]
    tn = 1024
    return pl.pallas_call(
        _mod_kernel,
        grid=(n // tn,),
        in_specs=[pl.BlockSpec((8, D_MODEL), lambda j: (0, 0)),
                  pl.BlockSpec((D_MODEL, tn), lambda j: (0, j)),
                  pl.BlockSpec((1, tn), lambda j: (0, j))],
        out_specs=pl.BlockSpec((8, tn), lambda j: (0, j)),
        out_shape=jax.ShapeDtypeStruct((8, n), F32),
        compiler_params=_cparams(1),
        name="adaln_mod",
    )(c_all, w_ada, b_ada.reshape(1, n))


def _ffn_kernel(x_ref, mod_ref, g_ref, wgu_ref, wo_ref, o_ref, h_ref, *, sub):
    j = pl.program_id(1)

    @pl.when(j == 0)
    def _():
        h = _rms(x_ref[...], g_ref[2 * sub:2 * sub + 1, :])
        h = h * (1.0 + mod_ref[3 * sub + 1:3 * sub + 2, :]) + mod_ref[3 * sub:3 * sub + 1, :]
        h_ref[...] = h.astype(BF16)
        o_ref[...] = jnp.zeros_like(o_ref)

    gu = jnp.dot(h_ref[...], wgu_ref[...], preferred_element_type=F32)
    a = (_silu(gu[:, :FF_TILE]) * gu[:, FF_TILE:]).astype(BF16)
    o_ref[...] += jnp.dot(a, wo_ref[...], preferred_element_type=F32)

    @pl.when(j == pl.num_programs(1) - 1)
    def _():
        y = _rms(o_ref[...], g_ref[2 * sub + 1:2 * sub + 2, :])
        o_ref[...] = x_ref[...] + (0.5 * mod_ref[3 * sub + 2:3 * sub + 3, :]) * y


def _ffn(x, mod, gains, w_gu, w_o, sub):
    tm = FFN_TM
    return pl.pallas_call(
        functools.partial(_ffn_kernel, sub=sub),
        grid=(N_TOK // tm, N_FF),
        in_specs=[pl.BlockSpec((tm, D_MODEL), lambda i, j: (i, 0)),
                  pl.BlockSpec((None, N_MOD, D_MODEL), lambda i, j: (_mod_row(i, tm), 0, 0)),
                  pl.BlockSpec((6, D_MODEL), lambda i, j: (0, 0)),
                  pl.BlockSpec((D_MODEL, 2 * FF_TILE), lambda i, j: (0, j)),
                  pl.BlockSpec((FF_TILE, D_MODEL), lambda i, j: (j, 0))],
        out_specs=pl.BlockSpec((tm, D_MODEL), lambda i, j: (i, 0)),
        out_shape=jax.ShapeDtypeStruct((N_TOK, D_MODEL), F32),
        scratch_shapes=[pltpu.VMEM((tm, D_MODEL), BF16)],
        compiler_params=_cparams(2),
        name=f"ffn{sub}",
    )(x, mod, gains, w_gu, w_o)


def _proj_kernel(x_ref, mod_ref, g_ref, w_ref, ws_ref, o_ref, os_ref, h_ref):
    j = pl.program_id(1)

    @pl.when(j == 0)
    def _():
        h = _rms(x_ref[...], g_ref[2:3, :])
        h = h * (1.0 + mod_ref[4:5, :]) + mod_ref[3:4, :]
        h_ref[...] = h.astype(BF16)
        os_ref[...] = jnp.dot(h_ref[...], ws_ref[...], preferred_element_type=F32)

    o_ref[...] = jnp.dot(h_ref[...], w_ref[...], preferred_element_type=F32)


def _mixer_proj(x, mod, gains, w_main, w_small):
    tm, tn = PROJ_TM, PROJ_TN
    return pl.pallas_call(
        _proj_kernel,
        grid=(N_TOK // tm, PROJ_MAIN // tn),
        in_specs=[pl.BlockSpec((tm, D_MODEL), lambda i, j: (i, 0)),
                  pl.BlockSpec((None, N_MOD, D_MODEL), lambda i, j: (_mod_row(i, tm), 0, 0)),
                  pl.BlockSpec((6, D_MODEL), lambda i, j: (0, 0)),
                  pl.BlockSpec((D_MODEL, tn), lambda i, j: (0, j)),
                  pl.BlockSpec((D_MODEL, PROJ_SMALL), lambda i, j: (0, 0))],
        out_specs=[pl.BlockSpec((tm, tn), lambda i, j: (i, j)),
                   pl.BlockSpec((tm, PROJ_SMALL), lambda i, j: (i, 0))],
        out_shape=[jax.ShapeDtypeStruct((N_TOK, PROJ_MAIN), F32),
                   jax.ShapeDtypeStruct((N_TOK, PROJ_SMALL), F32)],
        scratch_shapes=[pltpu.VMEM((tm, D_MODEL), BF16)],
        compiler_params=_cparams(2),
        name="mixer_proj",
    )(x, mod, gains, w_main, w_small)


_NT = (((1,), (1,)), ((), ()))
_TN = (((0,), (0,)), ((), ()))


def _log_sigmoid(x):
    return jnp.minimum(x, 0.0) - jnp.log1p(jnp.exp(-jnp.abs(x)))


def _gla_kernel(*refs, seq, has_s0, emit_state):
    q_ref, k_ref, v_ref, r_ref, a_ref, wa_ref, ba_ref, gn_ref = refs[:8]
    pos = 8
    if has_s0:
        s0f_ref, s0b_ref = refs[pos:pos + 2]
        pos += 2
    o_ref = refs[pos]
    pos += 1
    if emit_state:
        sf_ref, sb_ref = refs[pos:pos + 2]
        pos += 2
    of_scr, ob_scr, stf, stb = refs[pos:pos + 4]

    c = GLA_CHUNK
    nc = seq // c
    if has_s0:
        stf[...] = s0f_ref[...].T
        stb[...] = s0b_ref[...].T
    else:
        stf[...] = jnp.zeros_like(stf)
        stb[...] = jnp.zeros_like(stb)

    row = lax.broadcasted_iota(jnp.int32, (c, c), 0)
    col = lax.broadcasted_iota(jnp.int32, (c, c), 1)
    masks = (col <= row, col >= row)

    def chunk(n, d, st, o_scr):
        rows = pl.ds(pl.multiple_of(n * c, c), c)
        keep = masks[d]
        tri = jnp.where(keep, 1.0, 0.0).astype(BF16)
        x = jnp.dot(a_ref[rows, :].astype(BF16), wa_ref[d], preferred_element_type=F32) + ba_ref[d:d + 1, :]
        la = _log_sigmoid(x) * (1.0 / GLA_TAU)
        la1 = la.astype(BF16)
        rem = la - la1.astype(F32)
        la2 = rem.astype(BF16)
        la3 = (rem - la2.astype(F32)).astype(BF16)
        b = (jnp.dot(tri, la1, preferred_element_type=F32) + jnp.dot(tri, la2, preferred_element_type=F32)
             + jnp.dot(tri, la3, preferred_element_type=F32))
        b_last = b[c - 1:c, :] if d == 0 else b[0:1, :]
        q = q_ref[rows, :] * (GLA_DK ** -0.5)
        k = k_ref[rows, :]
        q_dec = (q * jnp.exp(b)).astype(BF16)
        k_inv = (k * jnp.exp(-b)).astype(BF16)
        k_end = (k * jnp.exp(b_last - b)).astype(BF16)
        vb = v_ref[rows, :].astype(BF16)
        a = lax.dot_general(q_dec, k_inv, _NT, preferred_element_type=F32)
        a = jnp.where(keep, a, 0.0).astype(BF16)
        s_t = st[...]
        o = jnp.dot(a, vb, preferred_element_type=F32)
        o += lax.dot_general(q_dec, s_t.astype(BF16), _NT, preferred_element_type=F32)
        o_scr[rows, :] = o
        st[...] = s_t * jnp.exp(b_last) + lax.dot_general(vb, k_end, _TN, preferred_element_type=F32)

    def body(i, carry):
        chunk(i, 0, stf, of_scr)
        chunk(nc - 1 - i, 1, stb, ob_scr)
        return carry

    lax.fori_loop(0, nc, body, 0)

    def finish(i, carry):
        rows = pl.ds(pl.multiple_of(i * 256, 256), 256)
        o = _rms(of_scr[rows, :] + ob_scr[rows, :], gn_ref[...])
        o_ref[rows, :] = (o * _silu(r_ref[rows, :])).astype(BF16)
        return carry

    lax.fori_loop(0, seq // 256, finish, 0)
    if emit_state:
        sf_ref[...] = stf[...].T
        sb_ref[...] = stb[...].T


def _gla(proj, small, wa, ba, gn, *, seq, n_batch, row0, s0=None, emit_state=False):
    rb = row0 // seq
    dk, dv = GLA_DK, GLA_DV
    in_specs = [pl.BlockSpec((seq, dk), lambda b, h: (rb + b, COL_Q // dk + h)),
                pl.BlockSpec((seq, dk), lambda b, h: (rb + b, COL_K // dk + h)),
                pl.BlockSpec((seq, dv), lambda b, h: (rb + b, COL_V // dv + h)),
                pl.BlockSpec((seq, dv), lambda b, h: (rb + b, COL_R // dv + h)),
                pl.BlockSpec((seq, LANES), lambda b, h: (rb + b, 1)),
                pl.BlockSpec((2, LANES, dk), lambda b, h: (0, 0, h)),
                pl.BlockSpec((2, dk), lambda b, h: (0, h)),
                pl.BlockSpec((1, dv), lambda b, h: (0, 0))]
    args = [proj, proj, proj, proj, small, wa, ba, gn]
    state_spec = pl.BlockSpec((None, None, dk, dv), lambda b, h: (b, h, 0, 0))
    if s0 is not None:
        in_specs += [state_spec, state_spec]
        args += list(s0)
    out_specs = [pl.BlockSpec((seq, dv), lambda b, h: (b, h))]
    out_shape = [jax.ShapeDtypeStruct((n_batch * seq, GLA_HEADS * dv), BF16)]
    if emit_state:
        out_specs += [state_spec, state_spec]
        out_shape += [jax.ShapeDtypeStruct((n_batch, GLA_HEADS, dk, dv), F32)] * 2
    return pl.pallas_call(
        functools.partial(_gla_kernel, seq=seq, has_s0=s0 is not None, emit_state=emit_state),
        grid=(n_batch, GLA_HEADS),
        in_specs=in_specs, out_specs=out_specs, out_shape=out_shape,
        scratch_shapes=[pltpu.VMEM((seq, dv), F32), pltpu.VMEM((seq, dv), F32),
                        pltpu.VMEM((dv, dk), F32), pltpu.VMEM((dv, dk), F32)],
        compiler_params=_cparams(2),
        name=f"gla_{seq}",
    )(*args)


def _rope_block(x, cs_ref):
    return x * cs_ref[:, :LANES] + pltpu.roll(x, MLA_ROPE, axis=1) * cs_ref[:, LANES:]


def _q_kernel(cq_ref, qn_ref, w_ref, cs_ref, o_ref):
    n = _rms(cq_ref[...], qn_ref[...]).astype(BF16)
    for h in range(MLA_HEADS):
        q = jnp.dot(n, w_ref[:, h * HEAD_W:(h + 1) * HEAD_W], preferred_element_type=F32)
        o_ref[:, h * HEAD_W:h * HEAD_W + LANES] = q[:, :LANES].astype(BF16)
        o_ref[:, h * HEAD_W + LANES:(h + 1) * HEAD_W] = _rope_block(q[:, LANES:], cs_ref).astype(BF16)


def _rope_row(i, tm):
    n_ctx_tiles = N_CTX // tm
    return jnp.where(i < n_ctx_tiles, 0, 1 + (i - n_ctx_tiles) % (DEC_SEQ // tm))


def _q_proj(proj, q_norm, w_q, rope_tab):
    tm = MLA_TM
    return pl.pallas_call(
        _q_kernel,
        grid=(N_TOK // tm,),
        in_specs=[pl.BlockSpec((tm, Q_LORA), lambda i: (i, COL_CQ // Q_LORA)),
                  _resident((1, Q_LORA)),
                  _resident((Q_LORA, MLA_HEADS * HEAD_W)),
                  pl.BlockSpec((tm, 2 * LANES), lambda i: (_rope_row(i, tm), 0))],
        out_specs=pl.BlockSpec((tm, MLA_HEADS * HEAD_W), lambda i: (i, 0)),
        out_shape=jax.ShapeDtypeStruct((N_TOK, MLA_HEADS * HEAD_W), BF16),
        compiler_params=_cparams(1),
        name="mla_q",
    )(proj, q_norm, w_q, rope_tab)


def _kv_kernel(*refs, norm, emit_ckv):
    ckv_ref, kn_ref, kr_ref, w_ref, cs_ref, k_ref, v_ref = refs[:7]
    c = ckv_ref[...]
    if norm:
        c = _rms(c, kn_ref[...])
    if emit_ckv:
        refs[7][...] = c
    cb = c.astype(BF16)
    nk = MLA_HEADS * MLA_NOPE
    v_ref[...] = jnp.dot(cb, w_ref[:, nk:], preferred_element_type=F32).astype(BF16)
    kr = _rope_block(kr_ref[...], cs_ref).astype(BF16)
    for h in range(MLA_HEADS):
        kn = jnp.dot(cb, w_ref[:, h * MLA_NOPE:(h + 1) * MLA_NOPE], preferred_element_type=F32)
        k_ref[:, h * HEAD_W:h * HEAD_W + LANES] = kn.astype(BF16)
        k_ref[:, h * HEAD_W + LANES:(h + 1) * HEAD_W] = kr


def _kv_proj(ckv_src, ckv_col, kr_src, kv_norm, w_kv, rope_tab, *, n_rows, row0, rope_row, norm, emit_ckv):
    tm = MLA_TM
    r0 = row0 // tm
    out_specs = [pl.BlockSpec((tm, MLA_HEADS * HEAD_W), lambda i: (i, 0)),
                 pl.BlockSpec((tm, MLA_HEADS * MLA_V), lambda i: (i, 0))]
    out_shape = [jax.ShapeDtypeStruct((n_rows, MLA_HEADS * HEAD_W), BF16),
                 jax.ShapeDtypeStruct((n_rows, MLA_HEADS * MLA_V), BF16)]
    if emit_ckv:
        out_specs.append(pl.BlockSpec((tm, KV_LORA), lambda i: (i, 0)))
        out_shape.append(jax.ShapeDtypeStruct((n_rows, KV_LORA), F32))
    return pl.pallas_call(
        functools.partial(_kv_kernel, norm=norm, emit_ckv=emit_ckv),
        grid=(n_rows // tm,),
        in_specs=[pl.BlockSpec((tm, KV_LORA), lambda i: (r0 + i, ckv_col)),
                  _resident((1, KV_LORA)),
                  pl.BlockSpec((tm, LANES), lambda i: (r0 + i, 0)),
                  _resident((KV_LORA, MLA_HEADS * (MLA_NOPE + MLA_V))),
                  pl.BlockSpec((tm, 2 * LANES), lambda i: (rope_row(i), 0))],
        out_specs=out_specs, out_shape=out_shape,
        compiler_params=_cparams(1),
        name=f"mla_kv_{row0}_{n_rows}",
    )(ckv_src, kv_norm, kr_src, w_kv, rope_tab)


def _attn_kernel(*refs, n_seg, heads):
    q_ref = refs[0]
    kv = refs[1:1 + 2 * n_seg]
    o_ref = refs[1 + 2 * n_seg]
    scale = (MLA_NOPE + MLA_ROPE) ** -0.5
    for h in range(heads):
        q = q_ref[:, h * HEAD_W:(h + 1) * HEAD_W]
        s = [lax.dot_general(q, kv[2 * g][:, h * HEAD_W:(h + 1) * HEAD_W], _NT,
                             preferred_element_type=F32) * scale for g in range(n_seg)]
        m = s[0].max(axis=-1, keepdims=True)
        for g in range(1, n_seg):
            m = jnp.maximum(m, s[g].max(axis=-1, keepdims=True))
        den = 0.0
        acc = 0.0
        for g in range(n_seg):
            p = jnp.exp(s[g] - m)
            den = den + p.sum(axis=-1, keepdims=True)
            acc = acc + jnp.dot(p.astype(BF16), kv[2 * g + 1][:, h * MLA_V:(h + 1) * MLA_V],
                                preferred_element_type=F32)
        o_ref[:, h * MLA_V:(h + 1) * MLA_V] = (acc / den).astype(BF16)


def _attention(q, q_row0, segs, *, n_batch, seq, tq, heads):
    nt = seq // tq
    qb0 = q_row0 // tq
    n_hblk = MLA_HEADS // heads
    in_specs = [pl.BlockSpec((tq, heads * HEAD_W), lambda b, h, t: (qb0 + b * nt + t, h))]
    args = [q]
    for keys, vals, n_keys in segs:
        in_specs += [pl.BlockSpec((n_keys, heads * HEAD_W), lambda b, h, t: (b, h)),
                     pl.BlockSpec((n_keys, heads * MLA_V), lambda b, h, t: (b, h))]
        args += [keys, vals]
    return pl.pallas_call(
        functools.partial(_attn_kernel, n_seg=len(segs), heads=heads),
        grid=(n_batch, n_hblk, nt),
        in_specs=in_specs,
        out_specs=pl.BlockSpec((tq, heads * MLA_V), lambda b, h, t: (b * nt + t, h)),
        out_shape=jax.ShapeDtypeStruct((n_batch * seq, MLA_HEADS * MLA_V), BF16),
        compiler_params=_cparams(3),
        name=f"mla_attn_{seq}",
    )(*args)


def _merge_kernel(og_ref, om_ref, ga_ref, gb_ref, x_ref, mod_ref, g_ref, wg_ref, wm_ref, wo_ref, o_ref):
    yg = jnp.dot(og_ref[...], wg_ref[...], preferred_element_type=F32)
    ym = jnp.dot(om_ref[...], wm_ref[...], preferred_element_type=F32)
    m = (jax.nn.sigmoid(ga_ref[...]) * yg + jax.nn.sigmoid(gb_ref[...]) * ym).astype(BF16)
    y = _rms(jnp.dot(m, wo_ref[...], preferred_element_type=F32), g_ref[3:4, :])
    o_ref[...] = x_ref[...] + mod_ref[5:6, :] * y


def _merge(o_gla, o_mla, proj, x, mod, gains, w_gla_out, w_mla_out, w_out):
    tm = MERGE_TM
    tile = lambda c0: pl.BlockSpec((tm, D_MODEL), lambda i: (i, c0 // D_MODEL))
    return pl.pallas_call(
        _merge_kernel,
        grid=(N_TOK // tm,),
        in_specs=[tile(0), tile(0), tile(COL_GA), tile(COL_GB), tile(0),
                  pl.BlockSpec((None, N_MOD, D_MODEL), lambda i: (_mod_row(i, tm), 0, 0)),
                  _resident((6, D_MODEL)),
                  _resident((D_MODEL, D_MODEL)), _resident((D_MODEL, D_MODEL)), _resident((D_MODEL, D_MODEL))],
        out_specs=tile(0),
        out_shape=jax.ShapeDtypeStruct((N_TOK, D_MODEL), F32),
        compiler_params=_cparams(1),
        name="mixer_merge",
    )(o_gla, o_mla, proj, proj, x, mod, gains, w_gla_out, w_mla_out, w_out)


_SWAP = np.arange(MLA_ROPE)
_SWAP = np.where(_SWAP % 32 < 16, _SWAP + 16, _SWAP - 16)


def _prep_ffn_in(w):
    w = w.astype(BF16)
    pad = ((0, 0), (0, FF_PAD - D_FF))
    g = jnp.pad(w[:, :D_FF], pad).reshape(D_MODEL, N_FF, FF_TILE)
    u = jnp.pad(w[:, D_FF:], pad).reshape(D_MODEL, N_FF, FF_TILE)
    return jnp.concatenate([g, u], axis=-1).reshape(D_MODEL, N_FF * 2 * FF_TILE)


def _prep_ffn_out(w):
    return jnp.pad(w.astype(BF16), ((0, FF_PAD - D_FF), (0, 0)))


def _prep_w_in(w):
    w = w.astype(BF16)
    sl = lambda a, n: w[:, a:a + n]
    q, k, v, r = sl(0, 1024), sl(1024, 1024), sl(2048, 2048), sl(4096, 2048)
    af, ab, cq, ckv, kr = sl(6144, 16), sl(6160, 16), sl(6176, 512), sl(6688, 512), sl(7200, 64)
    ga, gb = sl(7264, 2048), sl(9312, 2048)
    main = jnp.concatenate([q, k, v, r, ga, gb, cq, ckv], axis=1)
    small = jnp.concatenate([kr, kr[:, _SWAP], af, ab,
                             jnp.zeros((D_MODEL, PROJ_SMALL - 2 * MLA_ROPE - 2 * GLA_RANK), BF16)], axis=1)
    return main, small


def _prep_w_uq(w):
    w = w.astype(BF16).reshape(Q_LORA, MLA_HEADS, MLA_NOPE + MLA_ROPE)
    rope = w[..., MLA_NOPE:]
    return jnp.concatenate([w[..., :MLA_NOPE], rope, rope[..., _SWAP]], axis=-1).reshape(Q_LORA, MLA_HEADS * HEAD_W)


def _prep_w_ukv(w):
    w = w.astype(BF16).reshape(KV_LORA, MLA_HEADS, MLA_NOPE + MLA_V)
    return jnp.concatenate([w[..., :MLA_NOPE].reshape(KV_LORA, -1), w[..., MLA_NOPE:].reshape(KV_LORA, -1)], axis=1)


def _prep_w_alpha(w):
    out = jnp.zeros((2, LANES, GLA_HEADS * GLA_DK), BF16)
    out = out.at[0, 0:GLA_RANK].set(w[0].astype(BF16))
    return out.at[1, GLA_RANK:2 * GLA_RANK].set(w[1].astype(BF16))


def _rope_table(tm):
    rows = DEC_SEQ // GRID_W
    row = jnp.repeat(jnp.arange(rows), GRID_W).astype(F32)
    col = jnp.tile(jnp.arange(GRID_W), rows).astype(F32)
    half = MLA_ROPE // 2
    inv_freq = ROPE_THETA ** (-jnp.arange(0, half, 2, dtype=F32) / half)
    ang_r, ang_c = row[:, None] * inv_freq, col[:, None] * inv_freq
    cos = jnp.concatenate([jnp.cos(ang_r), jnp.cos(ang_r), jnp.cos(ang_c), jnp.cos(ang_c)], axis=1)
    sin = jnp.concatenate([-jnp.sin(ang_r), jnp.sin(ang_r), -jnp.sin(ang_c), jnp.sin(ang_c)], axis=1)
    z = jnp.zeros((DEC_SEQ, LANES - MLA_ROPE), F32)
    lat = jnp.concatenate([cos, z, sin, z], axis=1)
    ident = jnp.concatenate([jnp.ones((tm, MLA_ROPE), F32), jnp.zeros((tm, 2 * LANES - MLA_ROPE), F32)], axis=1)
    return jnp.concatenate([ident, lat], axis=0)


def kernel(x_prompt, x_sample, cache_ckv, cache_krope, state_gla_fwd, state_gla_bwd, c, c_ctx, w_ada, b_ada, norm_gains, w_ffn1_in, w_ffn1_out, w_ffn2_in, w_ffn2_out, w_in, w_gla_alpha, b_gla_alpha, gla_norm, w_gla_out, q_norm, kv_norm, w_uq, w_ukv, w_mla_out, w_out):
    x = jnp.concatenate([x_prompt.reshape(N_CTX, D_MODEL), x_sample.reshape(N_LAT, D_MODEL)], axis=0)
    c_all = jnp.concatenate([c_ctx[None, :], c, jnp.zeros((8 - 1 - DEC_BATCH, D_MODEL), F32)], axis=0)
    mod = _modulation(c_all, w_ada[0], b_ada[0]).reshape(8, N_MOD, D_MODEL)
    gains = norm_gains[0]

    x = _ffn(x, mod, gains, _prep_ffn_in(w_ffn1_in[0]), _prep_ffn_out(w_ffn1_out[0]), sub=0)

    w_main, w_small = _prep_w_in(w_in[0])
    proj, small = _mixer_proj(x, mod, gains, w_main, w_small)

    wa = _prep_w_alpha(w_gla_alpha[0])
    ba = b_gla_alpha[0]
    gn = gla_norm[0].reshape(1, GLA_DV)
    og_ctx, s_f, s_b = _gla(proj, small, wa, ba, gn, seq=SEQ, n_batch=BATCH, row0=0, emit_state=True)
    (og_lat,) = _gla(proj, small, wa, ba, gn, seq=DEC_SEQ, n_batch=DEC_BATCH, row0=N_CTX,
                     s0=(state_gla_fwd[:, 0], state_gla_bwd[:, 0]))
    o_gla = jnp.concatenate([og_ctx, og_lat], axis=0)

    tm = MLA_TM
    rope_tab = _rope_table(tm)
    q = _q_proj(proj, q_norm[0].reshape(1, Q_LORA), _prep_w_uq(w_uq[0]), rope_tab)
    w_kv = _prep_w_ukv(w_ukv[0])
    kvn = kv_norm[0].reshape(1, KV_LORA)
    ckv_col = COL_CKV // KV_LORA
    k_ctx, v_ctx, ckv_new = _kv_proj(proj, ckv_col, small, kvn, w_kv, rope_tab, n_rows=N_CTX, row0=0,
                                     rope_row=lambda i: 0, norm=True, emit_ckv=True)
    k_lat, v_lat = _kv_proj(proj, ckv_col, small, kvn, w_kv, rope_tab, n_rows=N_LAT, row0=N_CTX,
                            rope_row=lambda i: 1 + i % (DEC_SEQ // tm), norm=True, emit_ckv=False)
    n_past = DEC_BATCH * PAST_LEN
    kr_past = jnp.pad(cache_krope[:, 0].reshape(n_past, MLA_ROPE), ((0, 0), (0, LANES - MLA_ROPE)))
    k_past, v_past = _kv_proj(cache_ckv[:, 0].reshape(n_past, KV_LORA), 0, kr_past, kvn, w_kv, rope_tab,
                              n_rows=n_past, row0=0, rope_row=lambda i: 0, norm=False, emit_ckv=False)
    om_ctx = _attention(q, 0, [(k_ctx, v_ctx, SEQ)], n_batch=BATCH, seq=SEQ, tq=SEQ, heads=MLA_HEADS)
    om_lat = _attention(q, N_CTX, [(k_past, v_past, PAST_LEN), (k_lat, v_lat, DEC_SEQ)],
                        n_batch=DEC_BATCH, seq=DEC_SEQ, tq=ATT_TQ, heads=1)
    o_mla = jnp.concatenate([om_ctx, om_lat], axis=0)

    x = _merge(o_gla, o_mla, proj, x, mod, gains, w_gla_out[0].astype(BF16), w_mla_out[0].astype(BF16),
               w_out[0].astype(BF16))
    x = _ffn(x, mod, gains, _prep_ffn_in(w_ffn2_in[0]), _prep_ffn_out(w_ffn2_out[0]), sub=2)

    y_prompt = x[:N_CTX].reshape(BATCH, SEQ, D_MODEL)
    y_sample = x[N_CTX:].reshape(DEC_BATCH, DEC_SEQ, D_MODEL)
    new_ckv = ckv_new.reshape(BATCH, 1, SEQ, KV_LORA)
    new_krope = small[:N_CTX, :MLA_ROPE].reshape(BATCH, 1, SEQ, MLA_ROPE)
    return (y_prompt, y_sample, new_ckv, new_krope,
            s_f.reshape(BATCH, 1, GLA_HEADS, GLA_DK, GLA_DV), s_b.reshape(BATCH, 1, GLA_HEADS, GLA_DK, GLA_DV))
```

```python
import functools

import numpy as np
import jax
import jax.numpy as jnp
from jax import lax
from jax.experimental import pallas as pl
from jax.experimental.pallas import tpu as pltpu

F32 = jnp.float32
BF16 = jnp.bfloat16

D_MODEL = 2048
BATCH, SEQ = 16, 256
DEC_BATCH, DEC_SEQ = 4, 2048
PAST_LEN = 256
GRID_W = 64
D_FF = 5504
N_MOD = 9
GLA_HEADS, GLA_DK, GLA_DV = 4, 256, 512
GLA_RANK = 16
GLA_TAU = 16.0
GLA_CHUNK = 64
MLA_HEADS, MLA_NOPE, MLA_ROPE, MLA_V = 16, 128, 64, 128
Q_LORA = KV_LORA = 512
ROPE_THETA = 10000.0
NORM_EPS = 1e-6

N_CTX = BATCH * SEQ
N_LAT = DEC_BATCH * DEC_SEQ
N_TOK = N_CTX + N_LAT

LANES = 128
V7X_VMEM_LIMIT_BYTES = 60000 * 1024

FF_TILE = 512
FF_PAD = -(-D_FF // FF_TILE) * FF_TILE
N_FF = FF_PAD // FF_TILE
FFN_TM = 512
PROJ_TM, PROJ_TN = 1024, 1024
MLA_TM = 512
MERGE_TM = 256
ATT_TQ = 512
HEAD_W = 2 * LANES

COL_Q, COL_K, COL_V, COL_R = 0, 1024, 2048, 4096
COL_GA, COL_GB, COL_CQ, COL_CKV = 6144, 8192, 10240, 10752
PROJ_MAIN = 11264
PROJ_SMALL = 256


def _mod_row(i, tm):
    n_ctx_tiles = N_CTX // tm
    return jnp.where(i < n_ctx_tiles, 0, 1 + (i - n_ctx_tiles) // (DEC_SEQ // tm))


def _rms(x, gain):
    ms = jnp.mean(x * x, axis=-1, keepdims=True)
    return x * lax.rsqrt(ms + NORM_EPS) * gain


def _silu(x):
    return x * jax.nn.sigmoid(x)


def _cparams(n_axes):
    return pltpu.CompilerParams(dimension_semantics=("arbitrary",) * n_axes,
                                vmem_limit_bytes=V7X_VMEM_LIMIT_BYTES)


def _resident(shape):
    nd = len(shape)
    return pl.BlockSpec(shape, lambda *_: (0,) * nd, pipeline_mode=pl.Buffered(1))


def _mod_kernel(c_ref, w_ref, b_ref, o_ref):
    s = _silu(c_ref[...]).astype(BF16)
    o_ref[...] = jnp.dot(s, w_ref[...].astype(BF16), preferred_element_type=F32) + b_ref[...]


def _modulation(c_all, w_ada, b_ada):
    n = w_ada.shape[1]
    tn = 1024
    return pl.pallas_call(
        _mod_kernel,
        grid=(n // tn,),
        in_specs=[pl.BlockSpec((8, D_MODEL), lambda j: (0, 0)),
                  pl.BlockSpec((D_MODEL, tn), lambda j: (0, j)),
                  pl.BlockSpec((1, tn), lambda j: (0, j))],
        out_specs=pl.BlockSpec((8, tn), lambda j: (0, j)),
        out_shape=jax.ShapeDtypeStruct((8, n), F32),
        compiler_params=_cparams(1),
        name="adaln_mod",
    )(c_all, w_ada, b_ada.reshape(1, n))


def _ctx_spec(tm, width):
    last = N_CTX // tm - 1
    return pl.BlockSpec((tm, width), lambda i, *_: (jnp.minimum(i, last), 0))


def _lat_spec(tm, width):
    n_ctx_tiles = N_CTX // tm
    return pl.BlockSpec((tm, width), lambda i, *_: (jnp.maximum(i - n_ctx_tiles, 0), 0))


def _ffn_kernel(*refs, sub, split_in, split_out, tm):
    n_x = 2 if split_in else 1
    n_o = 2 if split_out else 1
    x_refs = refs[:n_x]
    mod_ref, g_ref, wgu_ref, wo_ref = refs[n_x:n_x + 4]
    o_refs = refs[n_x + 4:n_x + 4 + n_o]
    h_ref, acc_ref = refs[n_x + 4 + n_o:]
    i, j = pl.program_id(0), pl.program_id(1)
    is_ctx = i < N_CTX // tm

    def load_x():
        if split_in:
            return jnp.where(is_ctx, x_refs[0][...], x_refs[1][...])
        return x_refs[0][...]

    @pl.when(j == 0)
    def _():
        h = _rms(load_x(), g_ref[2 * sub:2 * sub + 1, :])
        h = h * (1.0 + mod_ref[3 * sub + 1:3 * sub + 2, :]) + mod_ref[3 * sub:3 * sub + 1, :]
        h_ref[...] = h.astype(BF16)
        acc_ref[...] = jnp.zeros_like(acc_ref)

    gu = jnp.dot(h_ref[...], wgu_ref[...], preferred_element_type=F32)
    a = (_silu(gu[:, :FF_TILE]) * gu[:, FF_TILE:]).astype(BF16)
    acc_ref[...] += jnp.dot(a, wo_ref[...], preferred_element_type=F32)

    @pl.when(j == pl.num_programs(1) - 1)
    def _():
        y = _rms(acc_ref[...], g_ref[2 * sub + 1:2 * sub + 2, :])
        out = load_x() + (0.5 * mod_ref[3 * sub + 2:3 * sub + 3, :]) * y
        if split_out:
            @pl.when(is_ctx)
            def _():
                o_refs[0][...] = out

            @pl.when(jnp.logical_not(is_ctx))
            def _():
                o_refs[1][...] = out
        else:
            o_refs[0][...] = out


def _ffn(xs, mod, gains, w_gu, w_o, sub, *, split_out):
    tm = FFN_TM
    split_in = len(xs) == 2
    tok_spec = lambda: pl.BlockSpec((tm, D_MODEL), lambda i, j: (i, 0))
    pair_specs = lambda: [_ctx_spec(tm, D_MODEL), _lat_spec(tm, D_MODEL)]
    pair_shapes = [jax.ShapeDtypeStruct((N_CTX, D_MODEL), F32), jax.ShapeDtypeStruct((N_LAT, D_MODEL), F32)]
    return pl.pallas_call(
        functools.partial(_ffn_kernel, sub=sub, split_in=split_in, split_out=split_out, tm=tm),
        grid=(N_TOK // tm, N_FF),
        in_specs=(pair_specs() if split_in else [tok_spec()]) + [
            pl.BlockSpec((None, N_MOD, D_MODEL), lambda i, j: (_mod_row(i, tm), 0, 0)),
            pl.BlockSpec((6, D_MODEL), lambda i, j: (0, 0)),
            pl.BlockSpec((D_MODEL, 2 * FF_TILE), lambda i, j: (0, j)),
            pl.BlockSpec((FF_TILE, D_MODEL), lambda i, j: (j, 0))],
        out_specs=pair_specs() if split_out else tok_spec(),
        out_shape=pair_shapes if split_out else jax.ShapeDtypeStruct((N_TOK, D_MODEL), F32),
        scratch_shapes=[pltpu.VMEM((tm, D_MODEL), BF16), pltpu.VMEM((tm, D_MODEL), F32)],
        compiler_params=_cparams(2),
        name=f"ffn{sub}",
    )(*xs, mod, gains, w_gu, w_o)


def _proj_kernel(x_ref, mod_ref, g_ref, w_ref, ws_ref, o_ref, os_ref, h_ref):
    j = pl.program_id(1)

    @pl.when(j == 0)
    def _():
        h = _rms(x_ref[...], g_ref[2:3, :])
        h = h * (1.0 + mod_ref[4:5, :]) + mod_ref[3:4, :]
        h_ref[...] = h.astype(BF16)
        os_ref[...] = jnp.dot(h_ref[...], ws_ref[...], preferred_element_type=F32)

    o_ref[...] = jnp.dot(h_ref[...], w_ref[...], preferred_element_type=F32)


def _mixer_proj(x, mod, gains, w_main, w_small):
    tm, tn = PROJ_TM, PROJ_TN
    return pl.pallas_call(
        _proj_kernel,
        grid=(N_TOK // tm, PROJ_MAIN // tn),
        in_specs=[pl.BlockSpec((tm, D_MODEL), lambda i, j: (i, 0)),
                  pl.BlockSpec((None, N_MOD, D_MODEL), lambda i, j: (_mod_row(i, tm), 0, 0)),
                  pl.BlockSpec((6, D_MODEL), lambda i, j: (0, 0)),
                  pl.BlockSpec((D_MODEL, tn), lambda i, j: (0, j)),
                  pl.BlockSpec((D_MODEL, PROJ_SMALL), lambda i, j: (0, 0))],
        out_specs=[pl.BlockSpec((tm, tn), lambda i, j: (i, j)),
                   pl.BlockSpec((tm, PROJ_SMALL), lambda i, j: (i, 0))],
        out_shape=[jax.ShapeDtypeStruct((N_TOK, PROJ_MAIN), F32),
                   jax.ShapeDtypeStruct((N_TOK, PROJ_SMALL), F32)],
        scratch_shapes=[pltpu.VMEM((tm, D_MODEL), BF16)],
        compiler_params=_cparams(2),
        name="mixer_proj",
    )(x, mod, gains, w_main, w_small)


_NT = (((1,), (1,)), ((), ()))
_TN = (((0,), (0,)), ((), ()))


def _log_sigmoid(x):
    return jnp.minimum(x, 0.0) - jnp.log1p(jnp.exp(-jnp.abs(x)))


def _gla_kernel(*refs, seq, has_s0, emit_state):
    q_ref, k_ref, v_ref, r_ref, a_ref, wa_ref, ba_ref, gn_ref = refs[:8]
    pos = 8
    if has_s0:
        s0f_ref, s0b_ref = refs[pos:pos + 2]
        pos += 2
    o_ref = refs[pos]
    pos += 1
    if emit_state:
        sf_ref, sb_ref = refs[pos:pos + 2]
        pos += 2
    of_scr, ob_scr, stf, stb = refs[pos:pos + 4]

    c = GLA_CHUNK
    nc = seq // c
    if has_s0:
        stf[...] = s0f_ref[...].T
        stb[...] = s0b_ref[...].T
    else:
        stf[...] = jnp.zeros_like(stf)
        stb[...] = jnp.zeros_like(stb)

    row = lax.broadcasted_iota(jnp.int32, (c, c), 0)
    col = lax.broadcasted_iota(jnp.int32, (c, c), 1)
    masks = (col <= row, col >= row)

    def chunk(n, d, st, o_scr):
        rows = pl.ds(pl.multiple_of(n * c, c), c)
        keep = masks[d]
        tri = jnp.where(keep, 1.0, 0.0).astype(BF16)
        x = jnp.dot(a_ref[rows, :].astype(BF16), wa_ref[d], preferred_element_type=F32) + ba_ref[d:d + 1, :]
        la = _log_sigmoid(x) * (1.0 / GLA_TAU)
        la1 = la.astype(BF16)
        rem = la - la1.astype(F32)
        la2 = rem.astype(BF16)
        la3 = (rem - la2.astype(F32)).astype(BF16)
        b = (jnp.dot(tri, la1, preferred_element_type=F32) + jnp.dot(tri, la2, preferred_element_type=F32)
             + jnp.dot(tri, la3, preferred_element_type=F32))
        b_last = b[c - 1:c, :] if d == 0 else b[0:1, :]
        q = q_ref[rows, :] * (GLA_DK ** -0.5)
        k = k_ref[rows, :]
        q_dec = (q * jnp.exp(b)).astype(BF16)
        k_inv = (k * jnp.exp(-b)).astype(BF16)
        k_end = (k * jnp.exp(b_last - b)).astype(BF16)
        vb = v_ref[rows, :].astype(BF16)
        a = lax.dot_general(q_dec, k_inv, _NT, preferred_element_type=F32)
        a = jnp.where(keep, a, 0.0).astype(BF16)
        s_t = st[...]
        o = jnp.dot(a, vb, preferred_element_type=F32)
        o += lax.dot_general(q_dec, s_t.astype(BF16), _NT, preferred_element_type=F32)
        o_scr[rows, :] = o
        st[...] = s_t * jnp.exp(b_last) + lax.dot_general(vb, k_end, _TN, preferred_element_type=F32)

    def body(i, carry):
        chunk(i, 0, stf, of_scr)
        chunk(nc - 1 - i, 1, stb, ob_scr)
        return carry

    lax.fori_loop(0, nc, body, 0)

    def finish(i, carry):
        rows = pl.ds(pl.multiple_of(i * 256, 256), 256)
        o = _rms(of_scr[rows, :] + ob_scr[rows, :], gn_ref[...])
        o_ref[rows, :] = (o * _silu(r_ref[rows, :])).astype(BF16)
        return carry

    lax.fori_loop(0, seq // 256, finish, 0)
    if emit_state:
        sf_ref[...] = stf[...].T
        sb_ref[...] = stb[...].T


def _gla(proj, small, wa, ba, gn, *, seq, n_batch, row0, s0=None, emit_state=False):
    rb = row0 // seq
    dk, dv = GLA_DK, GLA_DV
    in_specs = [pl.BlockSpec((seq, dk), lambda b, h: (rb + b, COL_Q // dk + h)),
                pl.BlockSpec((seq, dk), lambda b, h: (rb + b, COL_K // dk + h)),
                pl.BlockSpec((seq, dv), lambda b, h: (rb + b, COL_V // dv + h)),
                pl.BlockSpec((seq, dv), lambda b, h: (rb + b, COL_R // dv + h)),
                pl.BlockSpec((seq, LANES), lambda b, h: (rb + b, 1)),
                pl.BlockSpec((2, LANES, dk), lambda b, h: (0, 0, h)),
                pl.BlockSpec((2, dk), lambda b, h: (0, h)),
                pl.BlockSpec((1, dv), lambda b, h: (0, 0))]
    args = [proj, proj, proj, proj, small, wa, ba, gn]
    state_spec = pl.BlockSpec((None, None, dk, dv), lambda b, h: (b, h, 0, 0))
    if s0 is not None:
        in_specs += [state_spec, state_spec]
        args += list(s0)
    out_specs = [pl.BlockSpec((seq, dv), lambda b, h: (b, h))]
    out_shape = [jax.ShapeDtypeStruct((n_batch * seq, GLA_HEADS * dv), BF16)]
    if emit_state:
        out_specs += [state_spec, state_spec]
        out_shape += [jax.ShapeDtypeStruct((n_batch, GLA_HEADS, dk, dv), F32)] * 2
    return pl.pallas_call(
        functools.partial(_gla_kernel, seq=seq, has_s0=s0 is not None, emit_state=emit_state),
        grid=(n_batch, GLA_HEADS),
        in_specs=in_specs, out_specs=out_specs, out_shape=out_shape,
        scratch_shapes=[pltpu.VMEM((seq, dv), F32), pltpu.VMEM((seq, dv), F32),
                        pltpu.VMEM((dv, dk), F32), pltpu.VMEM((dv, dk), F32)],
        compiler_params=_cparams(2),
        name=f"gla_{seq}",
    )(*args)


def _rope_block(x, cs_ref):
    return x * cs_ref[:, :LANES] + pltpu.roll(x, MLA_ROPE, axis=1) * cs_ref[:, LANES:]


def _q_kernel(cq_ref, qn_ref, w_ref, cs_ref, o_ref):
    n = _rms(cq_ref[...], qn_ref[...]).astype(BF16)
    for h in range(MLA_HEADS):
        q = jnp.dot(n, w_ref[:, h * HEAD_W:(h + 1) * HEAD_W], preferred_element_type=F32)
        o_ref[:, h * HEAD_W:h * HEAD_W + LANES] = q[:, :LANES].astype(BF16)
        o_ref[:, h * HEAD_W + LANES:(h + 1) * HEAD_W] = _rope_block(q[:, LANES:], cs_ref).astype(BF16)


def _rope_row(i, tm):
    n_ctx_tiles = N_CTX // tm
    return jnp.where(i < n_ctx_tiles, 0, 1 + (i - n_ctx_tiles) % (DEC_SEQ // tm))


def _q_proj(proj, q_norm, w_q, rope_tab):
    tm = MLA_TM
    return pl.pallas_call(
        _q_kernel,
        grid=(N_TOK // tm,),
        in_specs=[pl.BlockSpec((tm, Q_LORA), lambda i: (i, COL_CQ // Q_LORA)),
                  _resident((1, Q_LORA)),
                  _resident((Q_LORA, MLA_HEADS * HEAD_W)),
                  pl.BlockSpec((tm, 2 * LANES), lambda i: (_rope_row(i, tm), 0))],
        out_specs=pl.BlockSpec((tm, MLA_HEADS * HEAD_W), lambda i: (i, 0)),
        out_shape=jax.ShapeDtypeStruct((N_TOK, MLA_HEADS * HEAD_W), BF16),
        compiler_params=_cparams(1),
        name="mla_q",
    )(proj, q_norm, w_q, rope_tab)


def _kv_kernel(*refs, norm, emit_ckv):
    ckv_ref, kn_ref, kr_ref, w_ref, cs_ref, k_ref, v_ref = refs[:7]
    c = ckv_ref[...]
    if norm:
        c = _rms(c, kn_ref[...])
    if emit_ckv:
        refs[7][...] = c
    cb = c.astype(BF16)
    nk = MLA_HEADS * MLA_NOPE
    v_ref[...] = jnp.dot(cb, w_ref[:, nk:], preferred_element_type=F32).astype(BF16)
    kr = _rope_block(kr_ref[...], cs_ref).astype(BF16)
    for h in range(MLA_HEADS):
        kn = jnp.dot(cb, w_ref[:, h * MLA_NOPE:(h + 1) * MLA_NOPE], preferred_element_type=F32)
        k_ref[:, h * HEAD_W:h * HEAD_W + LANES] = kn.astype(BF16)
        k_ref[:, h * HEAD_W + LANES:(h + 1) * HEAD_W] = kr


def _kv_proj(ckv_src, ckv_col, kr_src, kv_norm, w_kv, rope_tab, *, n_rows, row0, rope_row, norm, emit_ckv):
    tm = MLA_TM
    r0 = row0 // tm
    out_specs = [pl.BlockSpec((tm, MLA_HEADS * HEAD_W), lambda i: (i, 0)),
                 pl.BlockSpec((tm, MLA_HEADS * MLA_V), lambda i: (i, 0))]
    out_shape = [jax.ShapeDtypeStruct((n_rows, MLA_HEADS * HEAD_W), BF16),
                 jax.ShapeDtypeStruct((n_rows, MLA_HEADS * MLA_V), BF16)]
    if emit_ckv:
        out_specs.append(pl.BlockSpec((tm, KV_LORA), lambda i: (i, 0)))
        out_shape.append(jax.ShapeDtypeStruct((n_rows, KV_LORA), F32))
    return pl.pallas_call(
        functools.partial(_kv_kernel, norm=norm, emit_ckv=emit_ckv),
        grid=(n_rows // tm,),
        in_specs=[pl.BlockSpec((tm, KV_LORA), lambda i: (r0 + i, ckv_col)),
                  _resident((1, KV_LORA)),
                  pl.BlockSpec((tm, LANES), lambda i: (r0 + i, 0)),
                  _resident((KV_LORA, MLA_HEADS * (MLA_NOPE + MLA_V))),
                  pl.BlockSpec((tm, 2 * LANES), lambda i: (rope_row(i), 0))],
        out_specs=out_specs, out_shape=out_shape,
        compiler_params=_cparams(1),
        name=f"mla_kv_{row0}_{n_rows}",
    )(ckv_src, kv_norm, kr_src, w_kv, rope_tab)


def _attn_kernel(*refs, n_seg, heads):
    q_ref = refs[0]
    kv = refs[1:1 + 2 * n_seg]
    o_ref = refs[1 + 2 * n_seg]
    scale = (MLA_NOPE + MLA_ROPE) ** -0.5
    for h in range(heads):
        q = q_ref[:, h * HEAD_W:(h + 1) * HEAD_W]
        s = [lax.dot_general(q, kv[2 * g][:, h * HEAD_W:(h + 1) * HEAD_W], _NT,
                             preferred_element_type=F32) * scale for g in range(n_seg)]
        m = s[0].max(axis=-1, keepdims=True)
        for g in range(1, n_seg):
            m = jnp.maximum(m, s[g].max(axis=-1, keepdims=True))
        den = 0.0
        acc = 0.0
        for g in range(n_seg):
            p = jnp.exp(s[g] - m)
            den = den + p.sum(axis=-1, keepdims=True)
            acc = acc + jnp.dot(p.astype(BF16), kv[2 * g + 1][:, h * MLA_V:(h + 1) * MLA_V],
                                preferred_element_type=F32)
        o_ref[:, h * MLA_V:(h + 1) * MLA_V] = (acc / den).astype(BF16)


def _attention(q, q_row0, segs, *, n_batch, seq, tq, heads):
    nt = seq // tq
    qb0 = q_row0 // tq
    n_hblk = MLA_HEADS // heads
    in_specs = [pl.BlockSpec((tq, heads * HEAD_W), lambda b, h, t: (qb0 + b * nt + t, h))]
    args = [q]
    for keys, vals, n_keys in segs:
        in_specs += [pl.BlockSpec((n_keys, heads * HEAD_W), lambda b, h, t: (b, h)),
                     pl.BlockSpec((n_keys, heads * MLA_V), lambda b, h, t: (b, h))]
        args += [keys, vals]
    return pl.pallas_call(
        functools.partial(_attn_kernel, n_seg=len(segs), heads=heads),
        grid=(n_batch, n_hblk, nt),
        in_specs=in_specs,
        out_specs=pl.BlockSpec((tq, heads * MLA_V), lambda b, h, t: (b * nt + t, h)),
        out_shape=jax.ShapeDtypeStruct((n_batch * seq, MLA_HEADS * MLA_V), BF16),
        compiler_params=_cparams(3),
        name=f"mla_attn_{seq}",
    )(*args)


def _merge_kernel(ogc_ref, ogl_ref, omc_ref, oml_ref, ga_ref, gb_ref, x_ref, mod_ref, g_ref,
                  wg_ref, wm_ref, wo_ref, o_ref, *, tm):
    is_ctx = pl.program_id(0) < N_CTX // tm
    og = jnp.where(is_ctx, ogc_ref[...], ogl_ref[...])
    om = jnp.where(is_ctx, omc_ref[...], oml_ref[...])
    yg = jnp.dot(og, wg_ref[...], preferred_element_type=F32)
    ym = jnp.dot(om, wm_ref[...], preferred_element_type=F32)
    m = (jax.nn.sigmoid(ga_ref[...]) * yg + jax.nn.sigmoid(gb_ref[...]) * ym).astype(BF16)
    y = _rms(jnp.dot(m, wo_ref[...], preferred_element_type=F32), g_ref[3:4, :])
    o_ref[...] = x_ref[...] + mod_ref[5:6, :] * y


def _merge(o_gla, o_mla, proj, x, mod, gains, w_gla_out, w_mla_out, w_out):
    tm = MERGE_TM
    tile = lambda c0: pl.BlockSpec((tm, D_MODEL), lambda i: (i, c0 // D_MODEL))
    return pl.pallas_call(
        functools.partial(_merge_kernel, tm=tm),
        grid=(N_TOK // tm,),
        in_specs=[_ctx_spec(tm, D_MODEL), _lat_spec(tm, D_MODEL), _ctx_spec(tm, D_MODEL), _lat_spec(tm, D_MODEL),
                  tile(COL_GA), tile(COL_GB), tile(0),
                  pl.BlockSpec((None, N_MOD, D_MODEL), lambda i: (_mod_row(i, tm), 0, 0)),
                  _resident((6, D_MODEL)),
                  _resident((D_MODEL, D_MODEL)), _resident((D_MODEL, D_MODEL)), _resident((D_MODEL, D_MODEL))],
        out_specs=tile(0),
        out_shape=jax.ShapeDtypeStruct((N_TOK, D_MODEL), F32),
        compiler_params=_cparams(1),
        name="mixer_merge",
    )(*o_gla, *o_mla, proj, proj, x, mod, gains, w_gla_out, w_mla_out, w_out)


_SWAP = np.arange(MLA_ROPE)
_SWAP = np.where(_SWAP % 32 < 16, _SWAP + 16, _SWAP - 16)


_FF_BLOCKS = D_FF // LANES
_U_PER_STEP = FF_TILE // LANES


def _prep_ffn_in_kernel(g_ref, *refs):
    u_refs, o_ref = refs[:-1], refs[-1]
    j = pl.program_id(0)
    col = j * FF_TILE + lax.broadcasted_iota(jnp.int32, (1, FF_TILE), 1)
    o_ref[:, :FF_TILE] = jnp.where(col < D_FF, g_ref[...], 0.0).astype(BF16)
    lane = lax.broadcasted_iota(jnp.int32, (1, LANES), 1)
    for q, u_ref in enumerate(u_refs):
        ucol = j * FF_TILE + q * LANES + lane
        o_ref[:, FF_TILE + q * LANES:FF_TILE + (q + 1) * LANES] = jnp.where(ucol < D_FF, u_ref[...], 0.0).astype(BF16)


def _prep_ffn_in(w, name):
    last = 2 * _FF_BLOCKS - 1
    u_spec = lambda q: pl.BlockSpec((D_MODEL, LANES),
                                    lambda j: (0, jnp.minimum(_FF_BLOCKS + _U_PER_STEP * j + q, last)))
    return pl.pallas_call(
        _prep_ffn_in_kernel,
        grid=(N_FF,),
        in_specs=[pl.BlockSpec((D_MODEL, FF_TILE), lambda j: (0, j))] + [u_spec(q) for q in range(_U_PER_STEP)],
        out_specs=pl.BlockSpec((D_MODEL, 2 * FF_TILE), lambda j: (0, j)),
        out_shape=jax.ShapeDtypeStruct((D_MODEL, N_FF * 2 * FF_TILE), BF16),
        compiler_params=_cparams(1),
        name=name,
    )(w, *([w] * _U_PER_STEP))


def _prep_ffn_out_kernel(w_ref, o_ref):
    o_ref[...] = jnp.where(pl.program_id(0) < _FF_BLOCKS, w_ref[...], 0.0).astype(BF16)


def _prep_ffn_out(w, name):
    return pl.pallas_call(
        _prep_ffn_out_kernel,
        grid=(FF_PAD // LANES,),
        in_specs=[pl.BlockSpec((LANES, D_MODEL), lambda j: (jnp.minimum(j, _FF_BLOCKS - 1), 0))],
        out_specs=pl.BlockSpec((LANES, D_MODEL), lambda j: (j, 0)),
        out_shape=jax.ShapeDtypeStruct((FF_PAD, D_MODEL), BF16),
        compiler_params=_cparams(1),
        name=name,
    )(w)


def _prep_w_in(w):
    sl = lambda a, n: w[:, a:a + n]
    q, k, v, r = sl(0, 1024), sl(1024, 1024), sl(2048, 2048), sl(4096, 2048)
    af, ab, cq, ckv, kr = sl(6144, 16), sl(6160, 16), sl(6176, 512), sl(6688, 512), sl(7200, 64)
    ga, gb = sl(7264, 2048), sl(9312, 2048)
    main = jnp.concatenate([q, k, v, r, ga, gb, cq, ckv], axis=1).astype(BF16)
    small = jnp.concatenate([kr, kr[:, _SWAP], af, ab,
                             jnp.zeros((D_MODEL, PROJ_SMALL - 2 * MLA_ROPE - 2 * GLA_RANK), F32)], axis=1).astype(BF16)
    return main, small


def _prep_w_uq(w):
    w = w.astype(BF16).reshape(Q_LORA, MLA_HEADS, MLA_NOPE + MLA_ROPE)
    rope = w[..., MLA_NOPE:]
    return jnp.concatenate([w[..., :MLA_NOPE], rope, rope[..., _SWAP]], axis=-1).reshape(Q_LORA, MLA_HEADS * HEAD_W)


def _prep_w_ukv(w):
    w = w.astype(BF16).reshape(KV_LORA, MLA_HEADS, MLA_NOPE + MLA_V)
    return jnp.concatenate([w[..., :MLA_NOPE].reshape(KV_LORA, -1), w[..., MLA_NOPE:].reshape(KV_LORA, -1)], axis=1)


def _prep_w_alpha(w):
    out = jnp.zeros((2, LANES, GLA_HEADS * GLA_DK), BF16)
    out = out.at[0, 0:GLA_RANK].set(w[0].astype(BF16))
    return out.at[1, GLA_RANK:2 * GLA_RANK].set(w[1].astype(BF16))


def _rope_table(tm):
    rows = DEC_SEQ // GRID_W
    row = jnp.repeat(jnp.arange(rows), GRID_W).astype(F32)
    col = jnp.tile(jnp.arange(GRID_W), rows).astype(F32)
    half = MLA_ROPE // 2
    inv_freq = ROPE_THETA ** (-jnp.arange(0, half, 2, dtype=F32) / half)
    ang_r, ang_c = row[:, None] * inv_freq, col[:, None] * inv_freq
    cos = jnp.concatenate([jnp.cos(ang_r), jnp.cos(ang_r), jnp.cos(ang_c), jnp.cos(ang_c)], axis=1)
    sin = jnp.concatenate([-jnp.sin(ang_r), jnp.sin(ang_r), -jnp.sin(ang_c), jnp.sin(ang_c)], axis=1)
    z = jnp.zeros((DEC_SEQ, LANES - MLA_ROPE), F32)
    lat = jnp.concatenate([cos, z, sin, z], axis=1)
    ident = jnp.concatenate([jnp.ones((tm, MLA_ROPE), F32), jnp.zeros((tm, 2 * LANES - MLA_ROPE), F32)], axis=1)
    return jnp.concatenate([ident, lat], axis=0)


def kernel(x_prompt, x_sample, cache_ckv, cache_krope, state_gla_fwd, state_gla_bwd, c, c_ctx, w_ada, b_ada, norm_gains, w_ffn1_in, w_ffn1_out, w_ffn2_in, w_ffn2_out, w_in, w_gla_alpha, b_gla_alpha, gla_norm, w_gla_out, q_norm, kv_norm, w_uq, w_ukv, w_mla_out, w_out):
    c_all = jnp.concatenate([c_ctx[None, :], c, jnp.zeros((8 - 1 - DEC_BATCH, D_MODEL), F32)], axis=0)
    mod = _modulation(c_all, w_ada[0], b_ada[0]).reshape(8, N_MOD, D_MODEL)
    gains = norm_gains[0]

    x = _ffn((x_prompt.reshape(N_CTX, D_MODEL), x_sample.reshape(N_LAT, D_MODEL)), mod, gains,
             _prep_ffn_in(w_ffn1_in[0], "prep_ffn0_in"), _prep_ffn_out(w_ffn1_out[0], "prep_ffn0_out"),
             sub=0, split_out=False)

    w_main, w_small = _prep_w_in(w_in[0])
    proj, small = _mixer_proj(x, mod, gains, w_main, w_small)

    wa = _prep_w_alpha(w_gla_alpha[0])
    ba = b_gla_alpha[0]
    gn = gla_norm[0].reshape(1, GLA_DV)
    og_ctx, s_f, s_b = _gla(proj, small, wa, ba, gn, seq=SEQ, n_batch=BATCH, row0=0, emit_state=True)
    (og_lat,) = _gla(proj, small, wa, ba, gn, seq=DEC_SEQ, n_batch=DEC_BATCH, row0=N_CTX,
                     s0=(state_gla_fwd[:, 0], state_gla_bwd[:, 0]))

    tm = MLA_TM
    rope_tab = _rope_table(tm)
    q = _q_proj(proj, q_norm[0].reshape(1, Q_LORA), _prep_w_uq(w_uq[0]), rope_tab)
    w_kv = _prep_w_ukv(w_ukv[0])
    kvn = kv_norm[0].reshape(1, KV_LORA)
    ckv_col = COL_CKV // KV_LORA
    k_ctx, v_ctx, ckv_new = _kv_proj(proj, ckv_col, small, kvn, w_kv, rope_tab, n_rows=N_CTX, row0=0,
                                     rope_row=lambda i: 0, norm=True, emit_ckv=True)
    k_lat, v_lat = _kv_proj(proj, ckv_col, small, kvn, w_kv, rope_tab, n_rows=N_LAT, row0=N_CTX,
                            rope_row=lambda i: 1 + i % (DEC_SEQ // tm), norm=True, emit_ckv=False)
    n_past = DEC_BATCH * PAST_LEN
    kr_past = jnp.pad(cache_krope[:, 0].reshape(n_past, MLA_ROPE), ((0, 0), (0, LANES - MLA_ROPE)))
    k_past, v_past = _kv_proj(cache_ckv[:, 0].reshape(n_past, KV_LORA), 0, kr_past, kvn, w_kv, rope_tab,
                              n_rows=n_past, row0=0, rope_row=lambda i: 0, norm=False, emit_ckv=False)
    om_ctx = _attention(q, 0, [(k_ctx, v_ctx, SEQ)], n_batch=BATCH, seq=SEQ, tq=SEQ, heads=MLA_HEADS)
    om_lat = _attention(q, N_CTX, [(k_past, v_past, PAST_LEN), (k_lat, v_lat, DEC_SEQ)],
                        n_batch=DEC_BATCH, seq=DEC_SEQ, tq=ATT_TQ, heads=1)

    x = _merge((og_ctx, og_lat), (om_ctx, om_lat), proj, x, mod, gains, w_gla_out[0].astype(BF16),
               w_mla_out[0].astype(BF16), w_out[0].astype(BF16))
    y_ctx, y_lat = _ffn((x,), mod, gains, _prep_ffn_in(w_ffn2_in[0], "prep_ffn2_in"),
                        _prep_ffn_out(w_ffn2_out[0], "prep_ffn2_out"), sub=2, split_out=True)

    y_prompt = y_ctx.reshape(BATCH, SEQ, D_MODEL)
    y_sample = y_lat.reshape(DEC_BATCH, DEC_SEQ, D_MODEL)
    new_ckv = ckv_new.reshape(BATCH, 1, SEQ, KV_LORA)
    new_krope = small[:N_CTX, :MLA_ROPE].reshape(BATCH, 1, SEQ, MLA_ROPE)
    return (y_prompt, y_sample, new_ckv, new_krope,
            s_f.reshape(BATCH, 1, GLA_HEADS, GLA_DK, GLA_DV), s_b.reshape(BATCH, 1, GLA_HEADS, GLA_DK, GLA_DV))
```

```python
import functools

import numpy as np
import jax
import jax.numpy as jnp
from jax import lax
from jax.experimental import pallas as pl
from jax.experimental.pallas import tpu as pltpu

F32 = jnp.float32
BF16 = jnp.bfloat16

D_MODEL = 2048
BATCH, SEQ = 16, 256
DEC_BATCH, DEC_SEQ = 4, 2048
PAST_LEN = 256
GRID_W = 64
D_FF = 5504
N_MOD = 9
GLA_HEADS, GLA_DK, GLA_DV = 4, 256, 512
GLA_RANK = 16
GLA_TAU = 16.0
GLA_CHUNK = 64
GLA_BLOCK = 256
MLA_HEADS, MLA_NOPE, MLA_ROPE, MLA_V = 16, 128, 64, 128
Q_LORA = KV_LORA = 512
ROPE_THETA = 10000.0
NORM_EPS = 1e-6

N_CTX = BATCH * SEQ
N_LAT = DEC_BATCH * DEC_SEQ
N_TOK = N_CTX + N_LAT

LANES = 128
V7X_VMEM_LIMIT_BYTES = 60000 * 1024

FF_TILE = 512
FF_PAD = -(-D_FF // FF_TILE) * FF_TILE
N_FF = FF_PAD // FF_TILE
FFN_TM = 512
PROJ_TM, PROJ_TN = 1024, 1024
MLA_TM = 512
MERGE_TM = 256
ATT_TQ = 512
HEAD_W = 2 * LANES

COL_Q, COL_K, COL_V, COL_R = 0, 1024, 2048, 4096
COL_GA, COL_GB, COL_CQ, COL_CKV = 6144, 8192, 10240, 10752
PROJ_MAIN = 11264
PROJ_SMALL = 256


def _mod_row(i, tm):
    n_ctx_tiles = N_CTX // tm
    return jnp.where(i < n_ctx_tiles, 0, 1 + (i - n_ctx_tiles) // (DEC_SEQ // tm))


def _rms(x, gain):
    ms = jnp.mean(x * x, axis=-1, keepdims=True)
    return x * lax.rsqrt(ms + NORM_EPS) * gain


def _silu(x):
    return x * jax.nn.sigmoid(x)


def _cparams(n_axes):
    return pltpu.CompilerParams(dimension_semantics=("arbitrary",) * n_axes,
                                vmem_limit_bytes=V7X_VMEM_LIMIT_BYTES)


def _resident(shape):
    nd = len(shape)
    return pl.BlockSpec(shape, lambda *_: (0,) * nd, pipeline_mode=pl.Buffered(1))


def _mod_kernel(c_ref, w_ref, b_ref, o_ref):
    s = _silu(c_ref[...]).astype(BF16)
    o_ref[...] = jnp.dot(s, w_ref[...].astype(BF16), preferred_element_type=F32) + b_ref[...]


def _modulation(c_all, w_ada, b_ada):
    n = w_ada.shape[1]
    tn = 1024
    return pl.pallas_call(
        _mod_kernel,
        grid=(n // tn,),
        in_specs=[pl.BlockSpec((8, D_MODEL), lambda j: (0, 0)),
                  pl.BlockSpec((D_MODEL, tn), lambda j: (0, j)),
                  pl.BlockSpec((1, tn), lambda j: (0, j))],
        out_specs=pl.BlockSpec((8, tn), lambda j: (0, j)),
        out_shape=jax.ShapeDtypeStruct((8, n), F32),
        compiler_params=_cparams(1),
        name="adaln_mod",
    )(c_all, w_ada, b_ada.reshape(1, n))


def _ctx_spec(tm, width):
    last = N_CTX // tm - 1
    return pl.BlockSpec((tm, width), lambda i, *_: (jnp.minimum(i, last), 0))


def _lat_spec(tm, width):
    n_ctx_tiles = N_CTX // tm
    return pl.BlockSpec((tm, width), lambda i, *_: (jnp.maximum(i - n_ctx_tiles, 0), 0))


def _ffn_kernel(*refs, sub, split_in, split_out, tm):
    n_x = 2 if split_in else 1
    n_o = 2 if split_out else 1
    x_refs = refs[:n_x]
    mod_ref, g_ref, wgu_ref, wo_ref = refs[n_x:n_x + 4]
    o_refs = refs[n_x + 4:n_x + 4 + n_o]
    h_ref, acc_ref = refs[n_x + 4 + n_o:]
    i, j = pl.program_id(0), pl.program_id(1)
    is_ctx = i < N_CTX // tm

    def load_x():
        if split_in:
            return jnp.where(is_ctx, x_refs[0][...], x_refs[1][...])
        return x_refs[0][...]

    @pl.when(j == 0)
    def _():
        h = _rms(load_x(), g_ref[2 * sub:2 * sub + 1, :])
        h = h * (1.0 + mod_ref[3 * sub + 1:3 * sub + 2, :]) + mod_ref[3 * sub:3 * sub + 1, :]
        h_ref[...] = h.astype(BF16)
        acc_ref[...] = jnp.zeros_like(acc_ref)

    gu = jnp.dot(h_ref[...], wgu_ref[...], preferred_element_type=F32)
    a = (_silu(gu[:, :FF_TILE]) * gu[:, FF_TILE:]).astype(BF16)
    acc_ref[...] += jnp.dot(a, wo_ref[...], preferred_element_type=F32)

    @pl.when(j == pl.num_programs(1) - 1)
    def _():
        y = _rms(acc_ref[...], g_ref[2 * sub + 1:2 * sub + 2, :])
        out = load_x() + (0.5 * mod_ref[3 * sub + 2:3 * sub + 3, :]) * y
        if split_out:
            @pl.when(is_ctx)
            def _():
                o_refs[0][...] = out

            @pl.when(jnp.logical_not(is_ctx))
            def _():
                o_refs[1][...] = out
        else:
            o_refs[0][...] = out


def _ffn(xs, mod, gains, w_gu, w_o, sub, *, split_out):
    tm = FFN_TM
    split_in = len(xs) == 2
    tok_spec = lambda: pl.BlockSpec((tm, D_MODEL), lambda i, j: (i, 0))
    pair_specs = lambda: [_ctx_spec(tm, D_MODEL), _lat_spec(tm, D_MODEL)]
    pair_shapes = [jax.ShapeDtypeStruct((N_CTX, D_MODEL), F32), jax.ShapeDtypeStruct((N_LAT, D_MODEL), F32)]
    return pl.pallas_call(
        functools.partial(_ffn_kernel, sub=sub, split_in=split_in, split_out=split_out, tm=tm),
        grid=(N_TOK // tm, N_FF),
        in_specs=(pair_specs() if split_in else [tok_spec()]) + [
            pl.BlockSpec((None, N_MOD, D_MODEL), lambda i, j: (_mod_row(i, tm), 0, 0)),
            pl.BlockSpec((6, D_MODEL), lambda i, j: (0, 0)),
            pl.BlockSpec((D_MODEL, 2 * FF_TILE), lambda i, j: (0, j)),
            pl.BlockSpec((FF_TILE, D_MODEL), lambda i, j: (j, 0))],
        out_specs=pair_specs() if split_out else tok_spec(),
        out_shape=pair_shapes if split_out else jax.ShapeDtypeStruct((N_TOK, D_MODEL), F32),
        scratch_shapes=[pltpu.VMEM((tm, D_MODEL), BF16), pltpu.VMEM((tm, D_MODEL), F32)],
        compiler_params=_cparams(2),
        name=f"ffn{sub}",
    )(*xs, mod, gains, w_gu, w_o)


def _proj_kernel(x_ref, mod_ref, g_ref, w_ref, ws_ref, o_ref, os_ref, h_ref):
    j = pl.program_id(1)

    @pl.when(j == 0)
    def _():
        h = _rms(x_ref[...], g_ref[2:3, :])
        h = h * (1.0 + mod_ref[4:5, :]) + mod_ref[3:4, :]
        h_ref[...] = h.astype(BF16)
        os_ref[...] = jnp.dot(h_ref[...], ws_ref[...], preferred_element_type=F32)

    o_ref[...] = jnp.dot(h_ref[...], w_ref[...], preferred_element_type=F32)


def _mixer_proj(x, mod, gains, w_main, w_small):
    tm, tn = PROJ_TM, PROJ_TN
    return pl.pallas_call(
        _proj_kernel,
        grid=(N_TOK // tm, PROJ_MAIN // tn),
        in_specs=[pl.BlockSpec((tm, D_MODEL), lambda i, j: (i, 0)),
                  pl.BlockSpec((None, N_MOD, D_MODEL), lambda i, j: (_mod_row(i, tm), 0, 0)),
                  pl.BlockSpec((6, D_MODEL), lambda i, j: (0, 0)),
                  pl.BlockSpec((D_MODEL, tn), lambda i, j: (0, j)),
                  pl.BlockSpec((D_MODEL, PROJ_SMALL), lambda i, j: (0, 0))],
        out_specs=[pl.BlockSpec((tm, tn), lambda i, j: (i, j)),
                   pl.BlockSpec((tm, PROJ_SMALL), lambda i, j: (i, 0))],
        out_shape=[jax.ShapeDtypeStruct((N_TOK, PROJ_MAIN), F32),
                   jax.ShapeDtypeStruct((N_TOK, PROJ_SMALL), F32)],
        scratch_shapes=[pltpu.VMEM((tm, D_MODEL), BF16)],
        compiler_params=_cparams(2),
        name="mixer_proj",
    )(x, mod, gains, w_main, w_small)


_NT = (((1,), (1,)), ((), ()))
_TN = (((0,), (0,)), ((), ()))


def _log_sigmoid(x):
    return jnp.minimum(x, 0.0) - jnp.log1p(jnp.exp(-jnp.abs(x)))


def _gla_kernel(*refs, seq, has_s0, emit_state):
    q_ref, k_ref, v_ref, r_ref, a_ref, wa_ref, ba_ref, gn_ref = refs[:8]
    pos = 8
    if has_s0:
        s0_refs = refs[pos:pos + 2]
        pos += 2
    o_ref = refs[pos]
    pos += 1
    if emit_state:
        s_out_refs = refs[pos:pos + 2]
        pos += 2
    qd_scr, ke_scr, vb_scr, dec_scr, o_scr, st_scr = refs[pos:pos + 6]

    c, blk_rows = GLA_CHUNK, GLA_BLOCK
    cpb = blk_rows // c
    nc = seq // c
    for d in range(2):
        st_scr[d] = s0_refs[d][...].T if has_s0 else jnp.zeros((GLA_DV, GLA_DK), F32)

    row = lax.broadcasted_iota(jnp.int32, (blk_rows, blk_rows), 0)
    col = lax.broadcasted_iota(jnp.int32, (blk_rows, blk_rows), 1)
    same_chunk = (row // c) == (col // c)
    masks = (same_chunk & (col <= row), same_chunk & (col >= row))

    def bulk(blk, carry):
        rows = pl.ds(pl.multiple_of(blk * blk_rows, blk_rows), blk_rows)
        a_in = a_ref[rows, :].astype(BF16)
        q = q_ref[rows, :] * (GLA_DK ** -0.5)
        k = k_ref[rows, :]
        vb = v_ref[rows, :].astype(BF16)
        vb_scr[rows, :] = vb
        for d in range(2):
            keep = masks[d]
            tri = jnp.where(keep, 1.0, 0.0).astype(BF16)
            x = jnp.dot(a_in, wa_ref[d], preferred_element_type=F32) + ba_ref[d:d + 1, :]
            la = _log_sigmoid(x) * (1.0 / GLA_TAU)
            la1 = la.astype(BF16)
            rem = la - la1.astype(F32)
            la2 = rem.astype(BF16)
            la3 = (rem - la2.astype(F32)).astype(BF16)
            b = (jnp.dot(tri, la1, preferred_element_type=F32) + jnp.dot(tri, la2, preferred_element_type=F32)
                 + jnp.dot(tri, la3, preferred_element_type=F32))
            b3 = b.reshape(cpb, c, GLA_DK)
            b_last = b3[:, c - 1:c, :] if d == 0 else b3[:, 0:1, :]
            b_last_rows = jnp.broadcast_to(b_last, (cpb, c, GLA_DK)).reshape(blk_rows, GLA_DK)
            q_dec = (q * jnp.exp(b)).astype(BF16)
            k_inv = (k * jnp.exp(-b)).astype(BF16)
            k_end = (k * jnp.exp(b_last_rows - b)).astype(BF16)
            a = lax.dot_general(q_dec, k_inv, _NT, preferred_element_type=F32)
            a = jnp.where(keep, a, 0.0).astype(BF16)
            o_scr[d, rows, :] = jnp.dot(a, vb, preferred_element_type=F32)
            qd_scr[d, rows, :] = q_dec
            ke_scr[d, rows, :] = k_end
            dec_rows = pl.ds(pl.multiple_of(blk * (cpb * 8), cpb * 8), cpb * 8)
            dec_scr[d, dec_rows, :] = jnp.broadcast_to(jnp.exp(b_last), (cpb, 8, GLA_DK)).reshape(cpb * 8, GLA_DK)
        return carry

    lax.fori_loop(0, seq // blk_rows, bulk, 0)

    def recur(i, carry):
        for u in range(2):
            for d in range(2):
                n = 2 * i + u if d == 0 else nc - 1 - (2 * i + u)
                rows = pl.ds(pl.multiple_of(n * c, c), c)
                q_dec, k_end, vb = qd_scr[d, rows, :], ke_scr[d, rows, :], vb_scr[rows, :]
                dec = dec_scr[d, pl.ds(pl.multiple_of(n * 8, 8), 8), :][0:1, :]
                s_t = st_scr[d]
                o_scr[d, rows, :] += lax.dot_general(q_dec, s_t.astype(BF16), _NT, preferred_element_type=F32)
                st_scr[d] = s_t * dec + lax.dot_general(vb, k_end, _TN, preferred_element_type=F32)
        return carry

    lax.fori_loop(0, nc // 2, recur, 0)

    def finish(i, carry):
        rows = pl.ds(pl.multiple_of(i * blk_rows, blk_rows), blk_rows)
        o = _rms(o_scr[0, rows, :] + o_scr[1, rows, :], gn_ref[...])
        o_ref[rows, :] = (o * _silu(r_ref[rows, :])).astype(BF16)
        return carry

    lax.fori_loop(0, seq // blk_rows, finish, 0)
    if emit_state:
        for d in range(2):
            s_out_refs[d][...] = st_scr[d].T


def _gla(proj, small, wa, ba, gn, *, seq, n_batch, row0, s0=None, emit_state=False):
    rb = row0 // seq
    dk, dv = GLA_DK, GLA_DV
    in_specs = [pl.BlockSpec((seq, dk), lambda b, h: (rb + b, COL_Q // dk + h)),
                pl.BlockSpec((seq, dk), lambda b, h: (rb + b, COL_K // dk + h)),
                pl.BlockSpec((seq, dv), lambda b, h: (rb + b, COL_V // dv + h)),
                pl.BlockSpec((seq, dv), lambda b, h: (rb + b, COL_R // dv + h)),
                pl.BlockSpec((seq, LANES), lambda b, h: (rb + b, 1)),
                pl.BlockSpec((2, LANES, dk), lambda b, h: (0, 0, h)),
                pl.BlockSpec((2, dk), lambda b, h: (0, h)),
                pl.BlockSpec((1, dv), lambda b, h: (0, 0))]
    args = [proj, proj, proj, proj, small, wa, ba, gn]
    state_spec = pl.BlockSpec((None, None, dk, dv), lambda b, h: (b, h, 0, 0))
    if s0 is not None:
        in_specs += [state_spec, state_spec]
        args += list(s0)
    out_specs = [pl.BlockSpec((seq, dv), lambda b, h: (b, h))]
    out_shape = [jax.ShapeDtypeStruct((n_batch * seq, GLA_HEADS * dv), BF16)]
    if emit_state:
        out_specs += [state_spec, state_spec]
        out_shape += [jax.ShapeDtypeStruct((n_batch, GLA_HEADS, dk, dv), F32)] * 2
    return pl.pallas_call(
        functools.partial(_gla_kernel, seq=seq, has_s0=s0 is not None, emit_state=emit_state),
        grid=(n_batch, GLA_HEADS),
        in_specs=in_specs, out_specs=out_specs, out_shape=out_shape,
        scratch_shapes=[pltpu.VMEM((2, seq, dk), BF16),
                        pltpu.VMEM((2, seq, dk), BF16),
                        pltpu.VMEM((seq, dv), BF16),
                        pltpu.VMEM((2, seq // GLA_CHUNK * 8, dk), F32),
                        pltpu.VMEM((2, seq, dv), F32),
                        pltpu.VMEM((2, dv, dk), F32)],
        compiler_params=_cparams(2),
        name=f"gla_{seq}",
    )(*args)


def _rope_block(x, cs_ref):
    return x * cs_ref[:, :LANES] + pltpu.roll(x, MLA_ROPE, axis=1) * cs_ref[:, LANES:]


def _q_kernel(cq_ref, qn_ref, w_ref, cs_ref, o_ref):
    n = _rms(cq_ref[...], qn_ref[...]).astype(BF16)
    for h in range(MLA_HEADS):
        q = jnp.dot(n, w_ref[:, h * HEAD_W:(h + 1) * HEAD_W], preferred_element_type=F32)
        o_ref[:, h * HEAD_W:h * HEAD_W + LANES] = q[:, :LANES].astype(BF16)
        o_ref[:, h * HEAD_W + LANES:(h + 1) * HEAD_W] = _rope_block(q[:, LANES:], cs_ref).astype(BF16)


def _rope_row(i, tm):
    n_ctx_tiles = N_CTX // tm
    return jnp.where(i < n_ctx_tiles, 0, 1 + (i - n_ctx_tiles) % (DEC_SEQ // tm))


def _q_proj(proj, q_norm, w_q, rope_tab):
    tm = MLA_TM
    return pl.pallas_call(
        _q_kernel,
        grid=(N_TOK // tm,),
        in_specs=[pl.BlockSpec((tm, Q_LORA), lambda i: (i, COL_CQ // Q_LORA)),
                  _resident((1, Q_LORA)),
                  _resident((Q_LORA, MLA_HEADS * HEAD_W)),
                  pl.BlockSpec((tm, 2 * LANES), lambda i: (_rope_row(i, tm), 0))],
        out_specs=pl.BlockSpec((tm, MLA_HEADS * HEAD_W), lambda i: (i, 0)),
        out_shape=jax.ShapeDtypeStruct((N_TOK, MLA_HEADS * HEAD_W), BF16),
        compiler_params=_cparams(1),
        name="mla_q",
    )(proj, q_norm, w_q, rope_tab)


def _kv_kernel(*refs, norm, emit_ckv):
    ckv_ref, kn_ref, kr_ref, w_ref, cs_ref, k_ref, v_ref = refs[:7]
    c = ckv_ref[...]
    if norm:
        c = _rms(c, kn_ref[...])
    if emit_ckv:
        refs[7][...] = c
    cb = c.astype(BF16)
    nk = MLA_HEADS * MLA_NOPE
    kr = _rope_block(kr_ref[...], cs_ref).astype(BF16)
    one_col = jnp.where(lax.broadcasted_iota(jnp.int32, kr.shape, 1) == 0, 1.0, 0.0).astype(BF16)
    for h in range(MLA_HEADS):
        kn = jnp.dot(cb, w_ref[:, h * MLA_NOPE:(h + 1) * MLA_NOPE], preferred_element_type=F32)
        k_ref[:, h * HEAD_W:h * HEAD_W + LANES] = kn.astype(BF16)
        k_ref[:, h * HEAD_W + LANES:(h + 1) * HEAD_W] = kr
        vh = jnp.dot(cb, w_ref[:, nk + h * MLA_V:nk + (h + 1) * MLA_V], preferred_element_type=F32)
        v_ref[:, h * HEAD_W:h * HEAD_W + MLA_V] = vh.astype(BF16)
        v_ref[:, h * HEAD_W + MLA_V:(h + 1) * HEAD_W] = one_col


def _kv_proj(ckv_src, ckv_col, kr_src, kv_norm, w_kv, rope_tab, *, n_rows, row0, rope_row, norm, emit_ckv):
    tm = MLA_TM
    r0 = row0 // tm
    out_specs = [pl.BlockSpec((tm, MLA_HEADS * HEAD_W), lambda i: (i, 0)),
                 pl.BlockSpec((tm, MLA_HEADS * HEAD_W), lambda i: (i, 0))]
    out_shape = [jax.ShapeDtypeStruct((n_rows, MLA_HEADS * HEAD_W), BF16),
                 jax.ShapeDtypeStruct((n_rows, MLA_HEADS * HEAD_W), BF16)]
    if emit_ckv:
        out_specs.append(pl.BlockSpec((tm, KV_LORA), lambda i: (i, 0)))
        out_shape.append(jax.ShapeDtypeStruct((n_rows, KV_LORA), F32))
    return pl.pallas_call(
        functools.partial(_kv_kernel, norm=norm, emit_ckv=emit_ckv),
        grid=(n_rows // tm,),
        in_specs=[pl.BlockSpec((tm, KV_LORA), lambda i: (r0 + i, ckv_col)),
                  _resident((1, KV_LORA)),
                  pl.BlockSpec((tm, LANES), lambda i: (r0 + i, 0)),
                  _resident((KV_LORA, MLA_HEADS * (MLA_NOPE + MLA_V))),
                  pl.BlockSpec((tm, 2 * LANES), lambda i: (rope_row(i), 0))],
        out_specs=out_specs, out_shape=out_shape,
        compiler_params=_cparams(1),
        name=f"mla_kv_{row0}_{n_rows}",
    )(ckv_src, kv_norm, kr_src, w_kv, rope_tab)


_EXP2_SCALE = float((MLA_NOPE + MLA_ROPE) ** -0.5 * np.log2(np.e))


def _attn_kernel(*refs, n_seg, heads, tq):
    q_ref = refs[0]
    kv = refs[1:1 + 2 * n_seg]
    o_ref = refs[1 + 2 * n_seg]
    nt = q_ref.shape[0] // tq
    for h in range(heads):
        hs = slice(h * HEAD_W, (h + 1) * HEAD_W)

        def scores(t):
            q = q_ref[t * tq:(t + 1) * tq, hs]
            return [lax.dot_general(q, kv[2 * g][:, hs], _NT, preferred_element_type=F32) for g in range(n_seg)]

        s_next = scores(0)
        for t in range(nt):
            s = s_next
            if t + 1 < nt:
                s_next = scores(t + 1)
            m = s[0].max(axis=-1, keepdims=True)
            for g in range(1, n_seg):
                m = jnp.maximum(m, s[g].max(axis=-1, keepdims=True))
            acc = 0.0
            for g in range(n_seg):
                p = jnp.exp2((s[g] - m) * _EXP2_SCALE).astype(BF16)
                acc = acc + jnp.dot(p, kv[2 * g + 1][:, hs], preferred_element_type=F32)
            out = acc[:, :MLA_V] / acc[:, MLA_V:MLA_V + 1]
            o_ref[t * tq:(t + 1) * tq, h * MLA_V:(h + 1) * MLA_V] = out.astype(BF16)


def _attention(q, q_row0, segs, *, n_batch, seq, tq, heads):
    qb0 = q_row0 // seq
    n_hblk = MLA_HEADS // heads
    in_specs = [pl.BlockSpec((seq, heads * HEAD_W), lambda b, h: (qb0 + b, h))]
    args = [q]
    for keys, vals, n_keys in segs:
        in_specs += [pl.BlockSpec((n_keys, heads * HEAD_W), lambda b, h: (b, h)),
                     pl.BlockSpec((n_keys, heads * HEAD_W), lambda b, h: (b, h))]
        args += [keys, vals]
    return pl.pallas_call(
        functools.partial(_attn_kernel, n_seg=len(segs), heads=heads, tq=tq),
        grid=(n_batch, n_hblk),
        in_specs=in_specs,
        out_specs=pl.BlockSpec((seq, heads * MLA_V), lambda b, h: (b, h)),
        out_shape=jax.ShapeDtypeStruct((n_batch * seq, MLA_HEADS * MLA_V), BF16),
        compiler_params=_cparams(2),
        name=f"mla_attn_{seq}",
    )(*args)


def _merge_kernel(ogc_ref, ogl_ref, omc_ref, oml_ref, ga_ref, gb_ref, x_ref, mod_ref, g_ref,
                  wg_ref, wm_ref, wo_ref, o_ref, *, tm):
    is_ctx = pl.program_id(0) < N_CTX // tm
    og = jnp.where(is_ctx, ogc_ref[...], ogl_ref[...])
    om = jnp.where(is_ctx, omc_ref[...], oml_ref[...])
    yg = jnp.dot(og, wg_ref[...], preferred_element_type=F32)
    ym = jnp.dot(om, wm_ref[...], preferred_element_type=F32)
    m = (jax.nn.sigmoid(ga_ref[...]) * yg + jax.nn.sigmoid(gb_ref[...]) * ym).astype(BF16)
    y = _rms(jnp.dot(m, wo_ref[...], preferred_element_type=F32), g_ref[3:4, :])
    o_ref[...] = x_ref[...] + mod_ref[5:6, :] * y


def _merge(o_gla, o_mla, proj, x, mod, gains, w_gla_out, w_mla_out, w_out):
    tm = MERGE_TM
    tile = lambda c0: pl.BlockSpec((tm, D_MODEL), lambda i: (i, c0 // D_MODEL))
    return pl.pallas_call(
        functools.partial(_merge_kernel, tm=tm),
        grid=(N_TOK // tm,),
        in_specs=[_ctx_spec(tm, D_MODEL), _lat_spec(tm, D_MODEL), _ctx_spec(tm, D_MODEL), _lat_spec(tm, D_MODEL),
                  tile(COL_GA), tile(COL_GB), tile(0),
                  pl.BlockSpec((None, N_MOD, D_MODEL), lambda i: (_mod_row(i, tm), 0, 0)),
                  _resident((6, D_MODEL)),
                  _resident((D_MODEL, D_MODEL)), _resident((D_MODEL, D_MODEL)), _resident((D_MODEL, D_MODEL))],
        out_specs=tile(0),
        out_shape=jax.ShapeDtypeStruct((N_TOK, D_MODEL), F32),
        compiler_params=_cparams(1),
        name="mixer_merge",
    )(*o_gla, *o_mla, proj, proj, x, mod, gains, w_gla_out, w_mla_out, w_out)


_SWAP = np.arange(MLA_ROPE)
_SWAP = np.where(_SWAP % 32 < 16, _SWAP + 16, _SWAP - 16)


_FF_BLOCKS = D_FF // LANES
_U_PER_STEP = FF_TILE // LANES


def _prep_ffn_in_kernel(g_ref, *refs):
    u_refs, o_ref = refs[:-1], refs[-1]
    j = pl.program_id(0)
    col = j * FF_TILE + lax.broadcasted_iota(jnp.int32, (1, FF_TILE), 1)
    o_ref[:, :FF_TILE] = jnp.where(col < D_FF, g_ref[...], 0.0).astype(BF16)
    lane = lax.broadcasted_iota(jnp.int32, (1, LANES), 1)
    for q, u_ref in enumerate(u_refs):
        ucol = j * FF_TILE + q * LANES + lane
        o_ref[:, FF_TILE + q * LANES:FF_TILE + (q + 1) * LANES] = jnp.where(ucol < D_FF, u_ref[...], 0.0).astype(BF16)


def _prep_ffn_in(w, name):
    last = 2 * _FF_BLOCKS - 1
    u_spec = lambda q: pl.BlockSpec((D_MODEL, LANES),
                                    lambda j: (0, jnp.minimum(_FF_BLOCKS + _U_PER_STEP * j + q, last)))
    return pl.pallas_call(
        _prep_ffn_in_kernel,
        grid=(N_FF,),
        in_specs=[pl.BlockSpec((D_MODEL, FF_TILE), lambda j: (0, j))] + [u_spec(q) for q in range(_U_PER_STEP)],
        out_specs=pl.BlockSpec((D_MODEL, 2 * FF_TILE), lambda j: (0, j)),
        out_shape=jax.ShapeDtypeStruct((D_MODEL, N_FF * 2 * FF_TILE), BF16),
        compiler_params=_cparams(1),
        name=name,
    )(w, *([w] * _U_PER_STEP))


def _prep_ffn_out_kernel(w_ref, o_ref):
    o_ref[...] = jnp.where(pl.program_id(0) < _FF_BLOCKS, w_ref[...], 0.0).astype(BF16)


def _prep_ffn_out(w, name):
    return pl.pallas_call(
        _prep_ffn_out_kernel,
        grid=(FF_PAD // LANES,),
        in_specs=[pl.BlockSpec((LANES, D_MODEL), lambda j: (jnp.minimum(j, _FF_BLOCKS - 1), 0))],
        out_specs=pl.BlockSpec((LANES, D_MODEL), lambda j: (j, 0)),
        out_shape=jax.ShapeDtypeStruct((FF_PAD, D_MODEL), BF16),
        compiler_params=_cparams(1),
        name=name,
    )(w)


def _prep_w_in(w):
    sl = lambda a, n: w[:, a:a + n]
    q, k, v, r = sl(0, 1024), sl(1024, 1024), sl(2048, 2048), sl(4096, 2048)
    af, ab, cq, ckv, kr = sl(6144, 16), sl(6160, 16), sl(6176, 512), sl(6688, 512), sl(7200, 64)
    ga, gb = sl(7264, 2048), sl(9312, 2048)
    main = jnp.concatenate([q, k, v, r, ga, gb, cq, ckv], axis=1).astype(BF16)
    small = jnp.concatenate([kr, kr[:, _SWAP], af, ab,
                             jnp.zeros((D_MODEL, PROJ_SMALL - 2 * MLA_ROPE - 2 * GLA_RANK), F32)], axis=1).astype(BF16)
    return main, small


def _prep_w_uq(w):
    w = w.astype(BF16).reshape(Q_LORA, MLA_HEADS, MLA_NOPE + MLA_ROPE)
    rope = w[..., MLA_NOPE:]
    return jnp.concatenate([w[..., :MLA_NOPE], rope, rope[..., _SWAP]], axis=-1).reshape(Q_LORA, MLA_HEADS * HEAD_W)


def _prep_w_ukv(w):
    w = w.astype(BF16).reshape(KV_LORA, MLA_HEADS, MLA_NOPE + MLA_V)
    return jnp.concatenate([w[..., :MLA_NOPE].reshape(KV_LORA, -1), w[..., MLA_NOPE:].reshape(KV_LORA, -1)], axis=1)


def _prep_w_alpha(w):
    out = jnp.zeros((2, LANES, GLA_HEADS * GLA_DK), BF16)
    out = out.at[0, 0:GLA_RANK].set(w[0].astype(BF16))
    return out.at[1, GLA_RANK:2 * GLA_RANK].set(w[1].astype(BF16))


def _rope_table(tm):
    rows = DEC_SEQ // GRID_W
    row = jnp.repeat(jnp.arange(rows), GRID_W).astype(F32)
    col = jnp.tile(jnp.arange(GRID_W), rows).astype(F32)
    half = MLA_ROPE // 2
    inv_freq = ROPE_THETA ** (-jnp.arange(0, half, 2, dtype=F32) / half)
    ang_r, ang_c = row[:, None] * inv_freq, col[:, None] * inv_freq
    cos = jnp.concatenate([jnp.cos(ang_r), jnp.cos(ang_r), jnp.cos(ang_c), jnp.cos(ang_c)], axis=1)
    sin = jnp.concatenate([-jnp.sin(ang_r), jnp.sin(ang_r), -jnp.sin(ang_c), jnp.sin(ang_c)], axis=1)
    z = jnp.zeros((DEC_SEQ, LANES - MLA_ROPE), F32)
    lat = jnp.concatenate([cos, z, sin, z], axis=1)
    ident = jnp.concatenate([jnp.ones((tm, MLA_ROPE), F32), jnp.zeros((tm, 2 * LANES - MLA_ROPE), F32)], axis=1)
    return jnp.concatenate([ident, lat], axis=0)


def kernel(x_prompt, x_sample, cache_ckv, cache_krope, state_gla_fwd, state_gla_bwd, c, c_ctx, w_ada, b_ada, norm_gains, w_ffn1_in, w_ffn1_out, w_ffn2_in, w_ffn2_out, w_in, w_gla_alpha, b_gla_alpha, gla_norm, w_gla_out, q_norm, kv_norm, w_uq, w_ukv, w_mla_out, w_out):
    c_all = jnp.concatenate([c_ctx[None, :], c, jnp.zeros((8 - 1 - DEC_BATCH, D_MODEL), F32)], axis=0)
    mod = _modulation(c_all, w_ada[0], b_ada[0]).reshape(8, N_MOD, D_MODEL)
    gains = norm_gains[0]

    x = _ffn((x_prompt.reshape(N_CTX, D_MODEL), x_sample.reshape(N_LAT, D_MODEL)), mod, gains,
             _prep_ffn_in(w_ffn1_in[0], "prep_ffn0_in"), _prep_ffn_out(w_ffn1_out[0], "prep_ffn0_out"),
             sub=0, split_out=False)

    w_main, w_small = _prep_w_in(w_in[0])
    proj, small = _mixer_proj(x, mod, gains, w_main, w_small)

    wa = _prep_w_alpha(w_gla_alpha[0])
    ba = b_gla_alpha[0]
    gn = gla_norm[0].reshape(1, GLA_DV)
    og_ctx, s_f, s_b = _gla(proj, small, wa, ba, gn, seq=SEQ, n_batch=BATCH, row0=0, emit_state=True)
    (og_lat,) = _gla(proj, small, wa, ba, gn, seq=DEC_SEQ, n_batch=DEC_BATCH, row0=N_CTX,
                     s0=(state_gla_fwd[:, 0], state_gla_bwd[:, 0]))

    tm = MLA_TM
    rope_tab = _rope_table(tm)
    q = _q_proj(proj, q_norm[0].reshape(1, Q_LORA), _prep_w_uq(w_uq[0]), rope_tab)
    w_kv = _prep_w_ukv(w_ukv[0])
    kvn = kv_norm[0].reshape(1, KV_LORA)
    ckv_col = COL_CKV // KV_LORA
    k_ctx, v_ctx, ckv_new = _kv_proj(proj, ckv_col, small, kvn, w_kv, rope_tab, n_rows=N_CTX, row0=0,
                                     rope_row=lambda i: 0, norm=True, emit_ckv=True)
    k_lat, v_lat = _kv_proj(proj, ckv_col, small, kvn, w_kv, rope_tab, n_rows=N_LAT, row0=N_CTX,
                            rope_row=lambda i: 1 + i % (DEC_SEQ // tm), norm=True, emit_ckv=False)
    n_past = DEC_BATCH * PAST_LEN
    kr_past = jnp.pad(cache_krope[:, 0].reshape(n_past, MLA_ROPE), ((0, 0), (0, LANES - MLA_ROPE)))
    k_past, v_past = _kv_proj(cache_ckv[:, 0].reshape(n_past, KV_LORA), 0, kr_past, kvn, w_kv, rope_tab,
                              n_rows=n_past, row0=0, rope_row=lambda i: 0, norm=False, emit_ckv=False)
    om_ctx = _attention(q, 0, [(k_ctx, v_ctx, SEQ)], n_batch=BATCH, seq=SEQ, tq=SEQ, heads=MLA_HEADS)
    om_lat = _attention(q, N_CTX, [(k_past, v_past, PAST_LEN), (k_lat, v_lat, DEC_SEQ)],
                        n_batch=DEC_BATCH, seq=DEC_SEQ, tq=ATT_TQ, heads=1)

    x = _merge((og_ctx, og_lat), (om_ctx, om_lat), proj, x, mod, gains, w_gla_out[0].astype(BF16),
               w_mla_out[0].astype(BF16), w_out[0].astype(BF16))
    y_ctx, y_lat = _ffn((x,), mod, gains, _prep_ffn_in(w_ffn2_in[0], "prep_ffn2_in"),
                        _prep_ffn_out(w_ffn2_out[0], "prep_ffn2_out"), sub=2, split_out=True)

    y_prompt = y_ctx.reshape(BATCH, SEQ, D_MODEL)
    y_sample = y_lat.reshape(DEC_BATCH, DEC_SEQ, D_MODEL)
    new_ckv = ckv_new.reshape(BATCH, 1, SEQ, KV_LORA)
    new_krope = small[:N_CTX, :MLA_ROPE].reshape(BATCH, 1, SEQ, MLA_ROPE)
    return (y_prompt, y_sample, new_ckv, new_krope,
            s_f.reshape(BATCH, 1, GLA_HEADS, GLA_DK, GLA_DV), s_b.reshape(BATCH, 1, GLA_HEADS, GLA_DK, GLA_DV))
```

```python
import functools

import numpy as np
import jax
import jax.numpy as jnp
from jax import lax
from jax.experimental import pallas as pl
from jax.experimental.pallas import tpu as pltpu

F32 = jnp.float32
BF16 = jnp.bfloat16

D_MODEL = 2048
BATCH, SEQ = 16, 256
DEC_BATCH, DEC_SEQ = 4, 2048
PAST_LEN = 256
GRID_W = 64
D_FF = 5504
N_MOD = 9
GLA_HEADS, GLA_DK, GLA_DV = 4, 256, 512
GLA_RANK = 16
GLA_TAU = 16.0
GLA_CHUNK = 64
GLA_BLOCK = 256
GLA_CHUNKS_PER_ITER = 4
MLA_HEADS, MLA_NOPE, MLA_ROPE, MLA_V = 16, 128, 64, 128
Q_LORA = KV_LORA = 512
ROPE_THETA = 10000.0
NORM_EPS = 1e-6

N_CTX = BATCH * SEQ
N_LAT = DEC_BATCH * DEC_SEQ
N_TOK = N_CTX + N_LAT

LANES = 128
V7X_VMEM_LIMIT_BYTES = 60000 * 1024

FF_TILE = 512
FF_PAD = -(-D_FF // FF_TILE) * FF_TILE
N_FF = FF_PAD // FF_TILE
FFN_TM = 512
PROJ_TM, PROJ_TN = 1024, 1024
MLA_TM = 512
MERGE_TM = 256
ATT_TQ = 512
HEAD_W = 2 * LANES

COL_Q, COL_K, COL_V, COL_R = 0, 1024, 2048, 4096
COL_GA, COL_GB, COL_CQ, COL_CKV = 6144, 8192, 10240, 10752
PROJ_MAIN = 11264
PROJ_SMALL = 256


def _mod_row(i, tm):
    n_ctx_tiles = N_CTX // tm
    return jnp.where(i < n_ctx_tiles, 0, 1 + (i - n_ctx_tiles) // (DEC_SEQ // tm))


def _rms(x, gain):
    ms = jnp.mean(x * x, axis=-1, keepdims=True)
    return x * lax.rsqrt(ms + NORM_EPS) * gain


def _silu(x):
    return x * jax.nn.sigmoid(x)


def _cparams(n_axes):
    return pltpu.CompilerParams(dimension_semantics=("arbitrary",) * n_axes,
                                vmem_limit_bytes=V7X_VMEM_LIMIT_BYTES)


def _resident(shape):
    nd = len(shape)
    return pl.BlockSpec(shape, lambda *_: (0,) * nd, pipeline_mode=pl.Buffered(1))


def _mod_kernel(c_ref, w_ref, b_ref, o_ref):
    s = _silu(c_ref[...]).astype(BF16)
    o_ref[...] = jnp.dot(s, w_ref[...].astype(BF16), preferred_element_type=F32) + b_ref[...]


def _modulation(c_all, w_ada, b_ada):
    n = w_ada.shape[1]
    tn = 1024
    return pl.pallas_call(
        _mod_kernel,
        grid=(n // tn,),
        in_specs=[pl.BlockSpec((8, D_MODEL), lambda j: (0, 0)),
                  pl.BlockSpec((D_MODEL, tn), lambda j: (0, j)),
                  pl.BlockSpec((1, tn), lambda j: (0, j))],
        out_specs=pl.BlockSpec((8, tn), lambda j: (0, j)),
        out_shape=jax.ShapeDtypeStruct((8, n), F32),
        compiler_params=_cparams(1),
        name="adaln_mod",
    )(c_all, w_ada, b_ada.reshape(1, n))


def _ctx_spec(tm, width):
    last = N_CTX // tm - 1
    return pl.BlockSpec((tm, width), lambda i, *_: (jnp.minimum(i, last), 0))


def _lat_spec(tm, width):
    n_ctx_tiles = N_CTX // tm
    return pl.BlockSpec((tm, width), lambda i, *_: (jnp.maximum(i - n_ctx_tiles, 0), 0))


def _ffn_kernel(*refs, sub, split_in, split_out, tm):
    n_x = 2 if split_in else 1
    n_o = 2 if split_out else 1
    x_refs = refs[:n_x]
    mod_ref, g_ref, wgu_ref, wo_ref = refs[n_x:n_x + 4]
    o_refs = refs[n_x + 4:n_x + 4 + n_o]
    h_ref, acc_ref = refs[n_x + 4 + n_o:]
    i, j = pl.program_id(0), pl.program_id(1)
    is_ctx = i < N_CTX // tm

    def load_x():
        if split_in:
            return jnp.where(is_ctx, x_refs[0][...], x_refs[1][...])
        return x_refs[0][...]

    @pl.when(j == 0)
    def _():
        h = _rms(load_x(), g_ref[2 * sub:2 * sub + 1, :])
        h = h * (1.0 + mod_ref[3 * sub + 1:3 * sub + 2, :]) + mod_ref[3 * sub:3 * sub + 1, :]
        h_ref[...] = h.astype(BF16)
        acc_ref[...] = jnp.zeros_like(acc_ref)

    gu = jnp.dot(h_ref[...], wgu_ref[...], preferred_element_type=F32)
    a = (_silu(gu[:, :FF_TILE]) * gu[:, FF_TILE:]).astype(BF16)
    acc_ref[...] += jnp.dot(a, wo_ref[...], preferred_element_type=F32)

    @pl.when(j == pl.num_programs(1) - 1)
    def _():
        y = _rms(acc_ref[...], g_ref[2 * sub + 1:2 * sub + 2, :])
        out = load_x() + (0.5 * mod_ref[3 * sub + 2:3 * sub + 3, :]) * y
        if split_out:
            @pl.when(is_ctx)
            def _():
                o_refs[0][...] = out

            @pl.when(jnp.logical_not(is_ctx))
            def _():
                o_refs[1][...] = out
        else:
            o_refs[0][...] = out


def _ffn(xs, mod, gains, w_gu, w_o, sub, *, split_out):
    tm = FFN_TM
    split_in = len(xs) == 2
    tok_spec = lambda: pl.BlockSpec((tm, D_MODEL), lambda i, j: (i, 0))
    pair_specs = lambda: [_ctx_spec(tm, D_MODEL), _lat_spec(tm, D_MODEL)]
    pair_shapes = [jax.ShapeDtypeStruct((N_CTX, D_MODEL), F32), jax.ShapeDtypeStruct((N_LAT, D_MODEL), F32)]
    return pl.pallas_call(
        functools.partial(_ffn_kernel, sub=sub, split_in=split_in, split_out=split_out, tm=tm),
        grid=(N_TOK // tm, N_FF),
        in_specs=(pair_specs() if split_in else [tok_spec()]) + [
            pl.BlockSpec((None, N_MOD, D_MODEL), lambda i, j: (_mod_row(i, tm), 0, 0)),
            pl.BlockSpec((6, D_MODEL), lambda i, j: (0, 0)),
            pl.BlockSpec((D_MODEL, 2 * FF_TILE), lambda i, j: (0, j)),
            pl.BlockSpec((FF_TILE, D_MODEL), lambda i, j: (j, 0))],
        out_specs=pair_specs() if split_out else tok_spec(),
        out_shape=pair_shapes if split_out else jax.ShapeDtypeStruct((N_TOK, D_MODEL), F32),
        scratch_shapes=[pltpu.VMEM((tm, D_MODEL), BF16), pltpu.VMEM((tm, D_MODEL), F32)],
        compiler_params=_cparams(2),
        name=f"ffn{sub}",
    )(*xs, mod, gains, w_gu, w_o)


def _proj_kernel(x_ref, mod_ref, g_ref, w_ref, ws_ref, o_ref, os_ref, h_ref):
    j = pl.program_id(1)

    @pl.when(j == 0)
    def _():
        h = _rms(x_ref[...], g_ref[2:3, :])
        h = h * (1.0 + mod_ref[4:5, :]) + mod_ref[3:4, :]
        h_ref[...] = h.astype(BF16)
        os_ref[...] = jnp.dot(h_ref[...], ws_ref[...], preferred_element_type=F32)

    o_ref[...] = jnp.dot(h_ref[...], w_ref[...], preferred_element_type=F32)


def _mixer_proj(x, mod, gains, w_main, w_small):
    tm, tn = PROJ_TM, PROJ_TN
    return pl.pallas_call(
        _proj_kernel,
        grid=(N_TOK // tm, PROJ_MAIN // tn),
        in_specs=[pl.BlockSpec((tm, D_MODEL), lambda i, j: (i, 0)),
                  pl.BlockSpec((None, N_MOD, D_MODEL), lambda i, j: (_mod_row(i, tm), 0, 0)),
                  pl.BlockSpec((6, D_MODEL), lambda i, j: (0, 0)),
                  pl.BlockSpec((D_MODEL, tn), lambda i, j: (0, j)),
                  pl.BlockSpec((D_MODEL, PROJ_SMALL), lambda i, j: (0, 0))],
        out_specs=[pl.BlockSpec((tm, tn), lambda i, j: (i, j)),
                   pl.BlockSpec((tm, PROJ_SMALL), lambda i, j: (i, 0))],
        out_shape=[jax.ShapeDtypeStruct((N_TOK, PROJ_MAIN), F32),
                   jax.ShapeDtypeStruct((N_TOK, PROJ_SMALL), F32)],
        scratch_shapes=[pltpu.VMEM((tm, D_MODEL), BF16)],
        compiler_params=_cparams(2),
        name="mixer_proj",
    )(x, mod, gains, w_main, w_small)


_NT = (((1,), (1,)), ((), ()))
_TN = (((0,), (0,)), ((), ()))


def _log_sigmoid(x):
    return jnp.minimum(x, 0.0) - jnp.log1p(jnp.exp(-jnp.abs(x)))


def _gla_kernel(*refs, seq, has_s0, emit_state):
    q_ref, k_ref, v_ref, r_ref, a_ref, wa_ref, ba_ref, gn_ref = refs[:8]
    pos = 8
    if has_s0:
        s0_refs = refs[pos:pos + 2]
        pos += 2
    o_ref = refs[pos]
    pos += 1
    if emit_state:
        s_out_refs = refs[pos:pos + 2]
        pos += 2
    qd_scr, ke_scr, vb_scr, dec_scr, o_scr, st_scr = refs[pos:pos + 6]

    c, blk_rows = GLA_CHUNK, GLA_BLOCK
    cpb = blk_rows // c
    nc = seq // c
    for d in range(2):
        st_scr[d] = s0_refs[d][...] if has_s0 else jnp.zeros((GLA_DK, GLA_DV), F32)

    row = lax.broadcasted_iota(jnp.int32, (blk_rows, blk_rows), 0)
    col = lax.broadcasted_iota(jnp.int32, (blk_rows, blk_rows), 1)
    same_chunk = (row // c) == (col // c)
    masks = (same_chunk & (col <= row), same_chunk & (col >= row))

    tri = [jnp.where(masks[d], 1.0, 0.0).astype(BF16) for d in range(2)]
    n_blocks = seq // blk_rows
    blocks_per_iter = 2 if n_blocks % 2 == 0 else 1

    def bulk(it, carry):
        blks = [it * blocks_per_iter + j for j in range(blocks_per_iter)]
        rows = [pl.ds(pl.multiple_of(blk * blk_rows, blk_rows), blk_rows) for blk in blks]
        nb = range(blocks_per_iter)
        a_in = [a_ref[rows[j], :].astype(BF16) for j in nb]
        q = [q_ref[rows[j], :] * (GLA_DK ** -0.5) for j in nb]
        k = [k_ref[rows[j], :] for j in nb]
        vb = [v_ref[rows[j], :].astype(BF16) for j in nb]
        units = [(j, d) for j in nb for d in range(2)]
        x = {(j, d): jnp.dot(a_in[j], wa_ref[d], preferred_element_type=F32) + ba_ref[d:d + 1, :] for j, d in units}
        la = {u: _log_sigmoid(x[u]) * (1.0 / GLA_TAU) for u in units}
        la1 = {u: la[u].astype(BF16) for u in units}
        rem = {u: la[u] - la1[u].astype(F32) for u in units}
        la2 = {u: rem[u].astype(BF16) for u in units}
        la3 = {u: (rem[u] - la2[u].astype(F32)).astype(BF16) for u in units}
        b = {(j, d): jnp.dot(tri[d], la1[j, d], preferred_element_type=F32)
             + jnp.dot(tri[d], la2[j, d], preferred_element_type=F32)
             + jnp.dot(tri[d], la3[j, d], preferred_element_type=F32) for j, d in units}
        b3 = {u: b[u].reshape(cpb, c, GLA_DK) for u in units}
        b_last = {(j, d): b3[j, d][:, c - 1:c, :] if d == 0 else b3[j, d][:, 0:1, :] for j, d in units}
        dec = {u: jnp.exp(b_last[u]) for u in units}
        dec_rows = {u: jnp.broadcast_to(dec[u], (cpb, c, GLA_DK)).reshape(blk_rows, GLA_DK) for u in units}
        q_dec = {(j, d): (q[j] * jnp.exp(b[j, d])).astype(BF16) for j, d in units}
        k_inv = {(j, d): k[j] * jnp.exp(-b[j, d]) for j, d in units}
        k_end = {u: (k_inv[u] * dec_rows[u]).astype(BF16) for u in units}
        a = {u: lax.dot_general(q_dec[u], k_inv[u].astype(BF16), _NT, preferred_element_type=F32) for u in units}
        a = {(j, d): jnp.where(masks[d], a[j, d], 0.0).astype(BF16) for j, d in units}
        o = {(j, d): jnp.dot(a[j, d], vb[j], preferred_element_type=F32) for j, d in units}
        for j in nb:
            vb_scr[rows[j], :] = vb[j]
        for j, d in units:
            o_scr[d, rows[j], :] = o[j, d]
            qd_scr[d, rows[j], :] = q_dec[j, d]
            ke_scr[d, rows[j], :] = k_end[j, d]
            dec8 = pl.ds(pl.multiple_of(blks[j] * (cpb * 8), cpb * 8), cpb * 8)
            dec_scr[d, dec8, :] = jnp.broadcast_to(dec[j, d], (cpb, 8, GLA_DK)).reshape(cpb * 8, GLA_DK)
        return carry

    lax.fori_loop(0, n_blocks // blocks_per_iter, bulk, 0)

    cpi = GLA_CHUNKS_PER_ITER

    def recur(i, carry):
        steps = [(u, d) for u in range(cpi) for d in range(2)]
        chunk = {(u, d): cpi * i + u if d == 0 else nc - 1 - (cpi * i + u) for u, d in steps}
        rows = {s: pl.ds(pl.multiple_of(chunk[s] * c, c), c) for s in steps}
        dec_rows = {s: pl.ds(pl.multiple_of(chunk[s] * 8, 8), 8) for s in steps}
        kv = {(u, d): lax.dot_general(ke_scr[d, rows[u, d], :], vb_scr[rows[u, d], :], _TN,
                                      preferred_element_type=F32) for u, d in steps}
        dec_col = {s: jnp.broadcast_to(dec_scr[s[1], dec_rows[s], :][0:1, :], (LANES, GLA_DK)).T for s in steps}
        dec_full = {s: jnp.concatenate([dec_col[s]] * (GLA_DV // LANES), axis=1) for s in steps}
        s_t = [st_scr[d] for d in range(2)]
        for u, d in steps:
            inter = jnp.dot(qd_scr[d, rows[u, d], :], s_t[d].astype(BF16), preferred_element_type=F32)
            o_scr[d, rows[u, d], :] += inter
            s_t[d] = s_t[d] * dec_full[u, d] + kv[u, d]
        for d in range(2):
            st_scr[d] = s_t[d]
        return carry

    lax.fori_loop(0, nc // cpi, recur, 0)

    def finish(i, carry):
        rows = pl.ds(pl.multiple_of(i * blk_rows, blk_rows), blk_rows)
        o = _rms(o_scr[0, rows, :] + o_scr[1, rows, :], gn_ref[...])
        o_ref[rows, :] = (o * _silu(r_ref[rows, :])).astype(BF16)
        return carry

    lax.fori_loop(0, seq // blk_rows, finish, 0)
    if emit_state:
        for d in range(2):
            s_out_refs[d][...] = st_scr[d]


def _gla(proj, small, wa, ba, gn, *, seq, n_batch, row0, s0=None, emit_state=False):
    rb = row0 // seq
    dk, dv = GLA_DK, GLA_DV
    in_specs = [pl.BlockSpec((seq, dk), lambda b, h: (rb + b, COL_Q // dk + h)),
                pl.BlockSpec((seq, dk), lambda b, h: (rb + b, COL_K // dk + h)),
                pl.BlockSpec((seq, dv), lambda b, h: (rb + b, COL_V // dv + h)),
                pl.BlockSpec((seq, dv), lambda b, h: (rb + b, COL_R // dv + h)),
                pl.BlockSpec((seq, LANES), lambda b, h: (rb + b, 1)),
                pl.BlockSpec((2, LANES, dk), lambda b, h: (0, 0, h)),
                pl.BlockSpec((2, dk), lambda b, h: (0, h)),
                pl.BlockSpec((1, dv), lambda b, h: (0, 0))]
    args = [proj, proj, proj, proj, small, wa, ba, gn]
    state_spec = pl.BlockSpec((None, None, dk, dv), lambda b, h: (b, h, 0, 0))
    if s0 is not None:
        in_specs += [state_spec, state_spec]
        args += list(s0)
    out_specs = [pl.BlockSpec((seq, dv), lambda b, h: (b, h))]
    out_shape = [jax.ShapeDtypeStruct((n_batch * seq, GLA_HEADS * dv), BF16)]
    if emit_state:
        out_specs += [state_spec, state_spec]
        out_shape += [jax.ShapeDtypeStruct((n_batch, GLA_HEADS, dk, dv), F32)] * 2
    return pl.pallas_call(
        functools.partial(_gla_kernel, seq=seq, has_s0=s0 is not None, emit_state=emit_state),
        grid=(n_batch, GLA_HEADS),
        in_specs=in_specs, out_specs=out_specs, out_shape=out_shape,
        scratch_shapes=[pltpu.VMEM((2, seq, dk), BF16),
                        pltpu.VMEM((2, seq, dk), BF16),
                        pltpu.VMEM((seq, dv), BF16),
                        pltpu.VMEM((2, seq // GLA_CHUNK * 8, dk), F32),
                        pltpu.VMEM((2, seq, dv), F32),
                        pltpu.VMEM((2, dk, dv), F32)],
        compiler_params=_cparams(2),
        name=f"gla_{seq}",
    )(*args)


def _rope_block(x, cs_ref):
    return x * cs_ref[:, :LANES] + pltpu.roll(x, MLA_ROPE, axis=1) * cs_ref[:, LANES:]


def _q_kernel(cq_ref, qn_ref, w_ref, cs_ref, o_ref):
    n = _rms(cq_ref[...], qn_ref[...]).astype(BF16)
    for h in range(MLA_HEADS):
        q = jnp.dot(n, w_ref[:, h * HEAD_W:(h + 1) * HEAD_W], preferred_element_type=F32)
        o_ref[:, h * HEAD_W:h * HEAD_W + LANES] = q[:, :LANES].astype(BF16)
        o_ref[:, h * HEAD_W + LANES:(h + 1) * HEAD_W] = _rope_block(q[:, LANES:], cs_ref).astype(BF16)


def _rope_row(i, tm):
    n_ctx_tiles = N_CTX // tm
    return jnp.where(i < n_ctx_tiles, 0, 1 + (i - n_ctx_tiles) % (DEC_SEQ // tm))


def _q_proj(proj, q_norm, w_q, rope_tab):
    tm = MLA_TM
    return pl.pallas_call(
        _q_kernel,
        grid=(N_TOK // tm,),
        in_specs=[pl.BlockSpec((tm, Q_LORA), lambda i: (i, COL_CQ // Q_LORA)),
                  _resident((1, Q_LORA)),
                  _resident((Q_LORA, MLA_HEADS * HEAD_W)),
                  pl.BlockSpec((tm, 2 * LANES), lambda i: (_rope_row(i, tm), 0))],
        out_specs=pl.BlockSpec((tm, MLA_HEADS * HEAD_W), lambda i: (i, 0)),
        out_shape=jax.ShapeDtypeStruct((N_TOK, MLA_HEADS * HEAD_W), BF16),
        compiler_params=_cparams(1),
        name="mla_q",
    )(proj, q_norm, w_q, rope_tab)


def _kv_kernel(*refs, norm, emit_ckv):
    ckv_ref, kn_ref, kr_ref, w_ref, cs_ref, k_ref, v_ref = refs[:7]
    c = ckv_ref[...]
    if norm:
        c = _rms(c, kn_ref[...])
    if emit_ckv:
        refs[7][...] = c
    cb = c.astype(BF16)
    nk = MLA_HEADS * MLA_NOPE
    kr = _rope_block(kr_ref[...], cs_ref).astype(BF16)
    v_ref[...] = jnp.dot(cb, w_ref[:, nk:], preferred_element_type=F32).astype(BF16)
    for h in range(MLA_HEADS):
        kn = jnp.dot(cb, w_ref[:, h * MLA_NOPE:(h + 1) * MLA_NOPE], preferred_element_type=F32)
        k_ref[:, h * HEAD_W:h * HEAD_W + LANES] = kn.astype(BF16)
        k_ref[:, h * HEAD_W + LANES:(h + 1) * HEAD_W] = kr


def _kv_proj(ckv_src, ckv_col, kr_src, kv_norm, w_kv, rope_tab, *, n_rows, row0, rope_row, norm, emit_ckv):
    tm = MLA_TM
    r0 = row0 // tm
    out_specs = [pl.BlockSpec((tm, MLA_HEADS * HEAD_W), lambda i: (i, 0)),
                 pl.BlockSpec((tm, MLA_HEADS * MLA_V), lambda i: (i, 0))]
    out_shape = [jax.ShapeDtypeStruct((n_rows, MLA_HEADS * HEAD_W), BF16),
                 jax.ShapeDtypeStruct((n_rows, MLA_HEADS * MLA_V), BF16)]
    if emit_ckv:
        out_specs.append(pl.BlockSpec((tm, KV_LORA), lambda i: (i, 0)))
        out_shape.append(jax.ShapeDtypeStruct((n_rows, KV_LORA), F32))
    return pl.pallas_call(
        functools.partial(_kv_kernel, norm=norm, emit_ckv=emit_ckv),
        grid=(n_rows // tm,),
        in_specs=[pl.BlockSpec((tm, KV_LORA), lambda i: (r0 + i, ckv_col)),
                  _resident((1, KV_LORA)),
                  pl.BlockSpec((tm, LANES), lambda i: (r0 + i, 0)),
                  _resident((KV_LORA, MLA_HEADS * (MLA_NOPE + MLA_V))),
                  pl.BlockSpec((tm, 2 * LANES), lambda i: (rope_row(i), 0))],
        out_specs=out_specs, out_shape=out_shape,
        compiler_params=_cparams(1),
        name=f"mla_kv_{row0}_{n_rows}",
    )(ckv_src, kv_norm, kr_src, w_kv, rope_tab)


_EXP2_SCALE = float((MLA_NOPE + MLA_ROPE) ** -0.5 * np.log2(np.e))


def _attn_kernel(*refs, n_seg, heads, tq, lookahead):
    q_ref = refs[0]
    kv = refs[1:1 + 2 * n_seg]
    o_ref = refs[1 + 2 * n_seg]
    nt = q_ref.shape[0] // tq
    work = [(h, t) for h in range(heads) for t in range(nt)]

    def scores(h, t):
        hs = slice(h * HEAD_W, (h + 1) * HEAD_W)
        q = q_ref[t * tq:(t + 1) * tq, hs]
        return [lax.dot_general(q, kv[2 * g][:, hs], _NT, preferred_element_type=F32) for g in range(n_seg)]

    def row_max(s):
        m = s[0].max(axis=-1, keepdims=True)
        for g in range(1, n_seg):
            m = jnp.maximum(m, s[g].max(axis=-1, keepdims=True))
        return m

    def probs(s, m):
        return [jnp.exp2((s[g] - m) * _EXP2_SCALE).astype(BF16) for g in range(n_seg)]

    def values(g, h):
        v = kv[2 * g + 1][:, h * MLA_V:(h + 1) * MLA_V]
        one_col = jnp.where(lax.broadcasted_iota(jnp.int32, v.shape, 1) == 0, 1.0, 0.0).astype(BF16)
        return jnp.concatenate([v, one_col], axis=1)

    v_aug = {(g, h): values(g, h) for g in range(n_seg) for h in range(heads)}

    def weighted(p, h):
        acc = 0.0
        for g in range(n_seg):
            acc = acc + jnp.dot(p[g], v_aug[g, h], preferred_element_type=F32)
        return acc

    def store(acc, h, t):
        out = acc[:, :MLA_V] / acc[:, MLA_V:MLA_V + 1]
        o_ref[t * tq:(t + 1) * tq, h * MLA_V:(h + 1) * MLA_V] = out.astype(BF16)

    if lookahead >= len(work):
        s = [scores(*w) for w in work]
        m = [row_max(x) for x in s]
        p = [probs(x, y) for x, y in zip(s, m)]
        acc = [weighted(x, w[0]) for x, w in zip(p, work)]
        for x, w in zip(acc, work):
            store(x, *w)
    else:
        s_next = scores(*work[0])
        for idx, (h, t) in enumerate(work):
            s = s_next
            if idx + 1 < len(work):
                s_next = scores(*work[idx + 1])
            store(weighted(probs(s, row_max(s)), h), h, t)


def _attention(q, q_row0, segs, *, n_batch, seq, tq, heads):
    qb0 = q_row0 // seq
    n_hblk = MLA_HEADS // heads
    in_specs = [pl.BlockSpec((seq, heads * HEAD_W), lambda b, h: (qb0 + b, h))]
    args = [q]
    for keys, vals, n_keys in segs:
        in_specs += [pl.BlockSpec((n_keys, heads * HEAD_W), lambda b, h: (b, h)),
                     pl.BlockSpec((n_keys, heads * MLA_V), lambda b, h: (b, h))]
        args += [keys, vals]
    return pl.pallas_call(
        functools.partial(_attn_kernel, n_seg=len(segs), heads=heads, tq=tq,
                          lookahead=1 if seq > tq else heads),
        grid=(n_batch, n_hblk),
        in_specs=in_specs,
        out_specs=pl.BlockSpec((seq, heads * MLA_V), lambda b, h: (b, h)),
        out_shape=jax.ShapeDtypeStruct((n_batch * seq, MLA_HEADS * MLA_V), BF16),
        compiler_params=_cparams(2),
        name=f"mla_attn_{seq}",
    )(*args)


def _merge_kernel(ogc_ref, ogl_ref, omc_ref, oml_ref, ga_ref, gb_ref, x_ref, mod_ref, g_ref,
                  wg_ref, wm_ref, wo_ref, o_ref, *, tm):
    is_ctx = pl.program_id(0) < N_CTX // tm
    og = jnp.where(is_ctx, ogc_ref[...], ogl_ref[...])
    om = jnp.where(is_ctx, omc_ref[...], oml_ref[...])
    yg = jnp.dot(og, wg_ref[...], preferred_element_type=F32)
    ym = jnp.dot(om, wm_ref[...], preferred_element_type=F32)
    m = (jax.nn.sigmoid(ga_ref[...]) * yg + jax.nn.sigmoid(gb_ref[...]) * ym).astype(BF16)
    y = _rms(jnp.dot(m, wo_ref[...], preferred_element_type=F32), g_ref[3:4, :])
    o_ref[...] = x_ref[...] + mod_ref[5:6, :] * y


def _merge(o_gla, o_mla, proj, x, mod, gains, w_gla_out, w_mla_out, w_out):
    tm = MERGE_TM
    tile = lambda c0: pl.BlockSpec((tm, D_MODEL), lambda i: (i, c0 // D_MODEL))
    return pl.pallas_call(
        functools.partial(_merge_kernel, tm=tm),
        grid=(N_TOK // tm,),
        in_specs=[_ctx_spec(tm, D_MODEL), _lat_spec(tm, D_MODEL), _ctx_spec(tm, D_MODEL), _lat_spec(tm, D_MODEL),
                  tile(COL_GA), tile(COL_GB), tile(0),
                  pl.BlockSpec((None, N_MOD, D_MODEL), lambda i: (_mod_row(i, tm), 0, 0)),
                  _resident((6, D_MODEL)),
                  _resident((D_MODEL, D_MODEL)), _resident((D_MODEL, D_MODEL)), _resident((D_MODEL, D_MODEL))],
        out_specs=tile(0),
        out_shape=jax.ShapeDtypeStruct((N_TOK, D_MODEL), F32),
        compiler_params=_cparams(1),
        name="mixer_merge",
    )(*o_gla, *o_mla, proj, proj, x, mod, gains, w_gla_out, w_mla_out, w_out)


_SWAP = np.arange(MLA_ROPE)
_SWAP = np.where(_SWAP % 32 < 16, _SWAP + 16, _SWAP - 16)


_FF_BLOCKS = D_FF // LANES
_U_PER_STEP = FF_TILE // LANES


def _prep_ffn_in_kernel(g_ref, *refs):
    u_refs, o_ref = refs[:-1], refs[-1]
    j = pl.program_id(0)
    col = j * FF_TILE + lax.broadcasted_iota(jnp.int32, (1, FF_TILE), 1)
    o_ref[:, :FF_TILE] = jnp.where(col < D_FF, g_ref[...], 0.0).astype(BF16)
    lane = lax.broadcasted_iota(jnp.int32, (1, LANES), 1)
    for q, u_ref in enumerate(u_refs):
        ucol = j * FF_TILE + q * LANES + lane
        o_ref[:, FF_TILE + q * LANES:FF_TILE + (q + 1) * LANES] = jnp.where(ucol < D_FF, u_ref[...], 0.0).astype(BF16)


def _prep_ffn_in(w, name):
    last = 2 * _FF_BLOCKS - 1
    u_spec = lambda q: pl.BlockSpec((D_MODEL, LANES),
                                    lambda j: (0, jnp.minimum(_FF_BLOCKS + _U_PER_STEP * j + q, last)))
    return pl.pallas_call(
        _prep_ffn_in_kernel,
        grid=(N_FF,),
        in_specs=[pl.BlockSpec((D_MODEL, FF_TILE), lambda j: (0, j))] + [u_spec(q) for q in range(_U_PER_STEP)],
        out_specs=pl.BlockSpec((D_MODEL, 2 * FF_TILE), lambda j: (0, j)),
        out_shape=jax.ShapeDtypeStruct((D_MODEL, N_FF * 2 * FF_TILE), BF16),
        compiler_params=_cparams(1),
        name=name,
    )(w, *([w] * _U_PER_STEP))


def _prep_ffn_out_kernel(w_ref, o_ref):
    o_ref[...] = jnp.where(pl.program_id(0) < _FF_BLOCKS, w_ref[...], 0.0).astype(BF16)


def _prep_ffn_out(w, name):
    return pl.pallas_call(
        _prep_ffn_out_kernel,
        grid=(FF_PAD // LANES,),
        in_specs=[pl.BlockSpec((LANES, D_MODEL), lambda j: (jnp.minimum(j, _FF_BLOCKS - 1), 0))],
        out_specs=pl.BlockSpec((LANES, D_MODEL), lambda j: (j, 0)),
        out_shape=jax.ShapeDtypeStruct((FF_PAD, D_MODEL), BF16),
        compiler_params=_cparams(1),
        name=name,
    )(w)


def _prep_w_in(w):
    sl = lambda a, n: w[:, a:a + n]
    q, k, v, r = sl(0, 1024), sl(1024, 1024), sl(2048, 2048), sl(4096, 2048)
    af, ab, cq, ckv, kr = sl(6144, 16), sl(6160, 16), sl(6176, 512), sl(6688, 512), sl(7200, 64)
    ga, gb = sl(7264, 2048), sl(9312, 2048)
    main = jnp.concatenate([q, k, v, r, ga, gb, cq, ckv], axis=1).astype(BF16)
    small = jnp.concatenate([kr, kr[:, _SWAP], af, ab,
                             jnp.zeros((D_MODEL, PROJ_SMALL - 2 * MLA_ROPE - 2 * GLA_RANK), F32)], axis=1).astype(BF16)
    return main, small


def _prep_w_uq(w):
    w = w.astype(BF16).reshape(Q_LORA, MLA_HEADS, MLA_NOPE + MLA_ROPE)
    rope = w[..., MLA_NOPE:]
    return jnp.concatenate([w[..., :MLA_NOPE], rope, rope[..., _SWAP]], axis=-1).reshape(Q_LORA, MLA_HEADS * HEAD_W)


def _prep_w_ukv(w):
    w = w.astype(BF16).reshape(KV_LORA, MLA_HEADS, MLA_NOPE + MLA_V)
    return jnp.concatenate([w[..., :MLA_NOPE].reshape(KV_LORA, -1), w[..., MLA_NOPE:].reshape(KV_LORA, -1)], axis=1)


def _prep_w_alpha(w):
    out = jnp.zeros((2, LANES, GLA_HEADS * GLA_DK), BF16)
    out = out.at[0, 0:GLA_RANK].set(w[0].astype(BF16))
    return out.at[1, GLA_RANK:2 * GLA_RANK].set(w[1].astype(BF16))


def _rope_table(tm):
    rows = DEC_SEQ // GRID_W
    row = jnp.repeat(jnp.arange(rows), GRID_W).astype(F32)
    col = jnp.tile(jnp.arange(GRID_W), rows).astype(F32)
    half = MLA_ROPE // 2
    inv_freq = ROPE_THETA ** (-jnp.arange(0, half, 2, dtype=F32) / half)
    ang_r, ang_c = row[:, None] * inv_freq, col[:, None] * inv_freq
    cos = jnp.concatenate([jnp.cos(ang_r), jnp.cos(ang_r), jnp.cos(ang_c), jnp.cos(ang_c)], axis=1)
    sin = jnp.concatenate([-jnp.sin(ang_r), jnp.sin(ang_r), -jnp.sin(ang_c), jnp.sin(ang_c)], axis=1)
    z = jnp.zeros((DEC_SEQ, LANES - MLA_ROPE), F32)
    lat = jnp.concatenate([cos, z, sin, z], axis=1)
    ident = jnp.concatenate([jnp.ones((tm, MLA_ROPE), F32), jnp.zeros((tm, 2 * LANES - MLA_ROPE), F32)], axis=1)
    return jnp.concatenate([ident, lat], axis=0)


def kernel(x_prompt, x_sample, cache_ckv, cache_krope, state_gla_fwd, state_gla_bwd, c, c_ctx, w_ada, b_ada, norm_gains, w_ffn1_in, w_ffn1_out, w_ffn2_in, w_ffn2_out, w_in, w_gla_alpha, b_gla_alpha, gla_norm, w_gla_out, q_norm, kv_norm, w_uq, w_ukv, w_mla_out, w_out):
    c_all = jnp.concatenate([c_ctx[None, :], c, jnp.zeros((8 - 1 - DEC_BATCH, D_MODEL), F32)], axis=0)
    mod = _modulation(c_all, w_ada[0], b_ada[0]).reshape(8, N_MOD, D_MODEL)
    gains = norm_gains[0]

    x = _ffn((x_prompt.reshape(N_CTX, D_MODEL), x_sample.reshape(N_LAT, D_MODEL)), mod, gains,
             _prep_ffn_in(w_ffn1_in[0], "prep_ffn0_in"), _prep_ffn_out(w_ffn1_out[0], "prep_ffn0_out"),
             sub=0, split_out=False)

    w_main, w_small = _prep_w_in(w_in[0])
    proj, small = _mixer_proj(x, mod, gains, w_main, w_small)

    wa = _prep_w_alpha(w_gla_alpha[0])
    ba = b_gla_alpha[0]
    gn = gla_norm[0].reshape(1, GLA_DV)
    og_ctx, s_f, s_b = _gla(proj, small, wa, ba, gn, seq=SEQ, n_batch=BATCH, row0=0, emit_state=True)
    (og_lat,) = _gla(proj, small, wa, ba, gn, seq=DEC_SEQ, n_batch=DEC_BATCH, row0=N_CTX,
                     s0=(state_gla_fwd[:, 0], state_gla_bwd[:, 0]))

    tm = MLA_TM
    rope_tab = _rope_table(tm)
    q = _q_proj(proj, q_norm[0].reshape(1, Q_LORA), _prep_w_uq(w_uq[0]), rope_tab)
    w_kv = _prep_w_ukv(w_ukv[0])
    kvn = kv_norm[0].reshape(1, KV_LORA)
    ckv_col = COL_CKV // KV_LORA
    k_ctx, v_ctx, ckv_new = _kv_proj(proj, ckv_col, small, kvn, w_kv, rope_tab, n_rows=N_CTX, row0=0,
                                     rope_row=lambda i: 0, norm=True, emit_ckv=True)
    k_lat, v_lat = _kv_proj(proj, ckv_col, small, kvn, w_kv, rope_tab, n_rows=N_LAT, row0=N_CTX,
                            rope_row=lambda i: 1 + i % (DEC_SEQ // tm), norm=True, emit_ckv=False)
    n_past = DEC_BATCH * PAST_LEN
    kr_past = jnp.pad(cache_krope[:, 0].reshape(n_past, MLA_ROPE), ((0, 0), (0, LANES - MLA_ROPE)))
    k_past, v_past = _kv_proj(cache_ckv[:, 0].reshape(n_past, KV_LORA), 0, kr_past, kvn, w_kv, rope_tab,
                              n_rows=n_past, row0=0, rope_row=lambda i: 0, norm=False, emit_ckv=False)
    om_ctx = _attention(q, 0, [(k_ctx, v_ctx, SEQ)], n_batch=BATCH, seq=SEQ, tq=SEQ, heads=MLA_HEADS)
    om_lat = _attention(q, N_CTX, [(k_past, v_past, PAST_LEN), (k_lat, v_lat, DEC_SEQ)],
                        n_batch=DEC_BATCH, seq=DEC_SEQ, tq=ATT_TQ, heads=1)

    x = _merge((og_ctx, og_lat), (om_ctx, om_lat), proj, x, mod, gains, w_gla_out[0].astype(BF16),
               w_mla_out[0].astype(BF16), w_out[0].astype(BF16))
    y_ctx, y_lat = _ffn((x,), mod, gains, _prep_ffn_in(w_ffn2_in[0], "prep_ffn2_in"),
                        _prep_ffn_out(w_ffn2_out[0], "prep_ffn2_out"), sub=2, split_out=True)

    y_prompt = y_ctx.reshape(BATCH, SEQ, D_MODEL)
    y_sample = y_lat.reshape(DEC_BATCH, DEC_SEQ, D_MODEL)
    new_ckv = ckv_new.reshape(BATCH, 1, SEQ, KV_LORA)
    new_krope = small[:N_CTX, :MLA_ROPE].reshape(BATCH, 1, SEQ, MLA_ROPE)
    return (y_prompt, y_sample, new_ckv, new_krope,
            s_f.reshape(BATCH, 1, GLA_HEADS, GLA_DK, GLA_DV), s_b.reshape(BATCH, 1, GLA_HEADS, GLA_DK, GLA_DV))
```

```python
import functools
from typing import NamedTuple

import numpy as np
import jax
import jax.numpy as jnp
from jax import lax
from jax.experimental import pallas as pl
from jax.experimental.pallas import tpu as pltpu

F32 = jnp.float32
BF16 = jnp.bfloat16

D_MODEL = 2048
BATCH, SEQ = 16, 256
DEC_BATCH, DEC_SEQ = 4, 2048
PAST_LEN = 256
GRID_W = 64
D_FF = 5504
N_MOD = 9
GLA_HEADS, GLA_DK, GLA_DV = 4, 256, 512
GLA_RANK = 16
GLA_TAU = 16.0
GLA_CHUNK = 64
GLA_BLOCK = 256
GLA_CHUNKS_PER_ITER = 4
MLA_HEADS, MLA_NOPE, MLA_ROPE, MLA_V = 16, 128, 64, 128
Q_LORA = KV_LORA = 512
ROPE_THETA = 10000.0
NORM_EPS = 1e-6

N_CTX = BATCH * SEQ
N_LAT = DEC_BATCH * DEC_SEQ
N_TOK = N_CTX + N_LAT

LANES = 128
V7X_VMEM_LIMIT_BYTES = 60000 * 1024

FF_TILE = 512
FF_PAD = -(-D_FF // FF_TILE) * FF_TILE
N_FF = FF_PAD // FF_TILE
FFN_TM = 512
_FFN_SIDE_ROWS = 64
PROJ_TM, PROJ_TN = 1024, 1024
MLA_TM = 512
MERGE_TM = 256
ATT_TQ = 512
HEAD_W = 2 * LANES

COL_Q, COL_K, COL_V, COL_R = 0, 1024, 2048, 4096
COL_GA, COL_GB, COL_CQ, COL_CKV = 6144, 8192, 10240, 10752
PROJ_MAIN = 11264
PROJ_SMALL = 256


class _Stream(NamedTuple):
    name: str
    n_batch: int
    seq: int
    mod_base: int
    per_batch_mod: bool
    rope: bool

    @property
    def n_tok(self):
        return self.n_batch * self.seq

    def mod_row(self, tile, tm):
        return self.mod_base + (tile * tm // self.seq if self.per_batch_mod else 0 * tile)

    def rope_row(self, tile, tm):
        return 1 + tile % (self.seq // tm) if self.rope else 0 * tile


_CTX = _Stream("ctx", BATCH, SEQ, 0, False, False)
_LAT = _Stream("lat", DEC_BATCH, DEC_SEQ, 1, True, True)


def _rms(x, gain):
    ms = jnp.mean(x * x, axis=-1, keepdims=True)
    return x * lax.rsqrt(ms + NORM_EPS) * gain


def _silu(x):
    return x * jax.nn.sigmoid(x)


def _cparams(n_axes):
    return pltpu.CompilerParams(dimension_semantics=("arbitrary",) * n_axes,
                                vmem_limit_bytes=V7X_VMEM_LIMIT_BYTES)


def _resident(shape):
    nd = len(shape)
    return pl.BlockSpec(shape, lambda *_: (0,) * nd, pipeline_mode=pl.Buffered(1))


def _mod_kernel(c_ref, w_ref, b_ref, o_ref):
    s = _silu(c_ref[...]).astype(BF16)
    o_ref[...] = jnp.dot(s, w_ref[...].astype(BF16), preferred_element_type=F32) + b_ref[...]


def _modulation(c_all, w_ada, b_ada):
    n = w_ada.shape[1]
    tn = 1024
    return pl.pallas_call(
        _mod_kernel,
        grid=(n // tn,),
        in_specs=[pl.BlockSpec((8, D_MODEL), lambda j: (0, 0)),
                  pl.BlockSpec((D_MODEL, tn), lambda j: (0, j)),
                  pl.BlockSpec((1, tn), lambda j: (0, j))],
        out_specs=pl.BlockSpec((8, tn), lambda j: (0, j)),
        out_shape=jax.ShapeDtypeStruct((8, n), F32),
        compiler_params=_cparams(1),
        name="adaln_mod",
    )(c_all, w_ada, b_ada.reshape(1, n))


def _ffn_kernel(xp_ref, xe_ref, modp_ref, mode_ref, g_ref, wgu_ref, wo_ref, o_ref,
                h_cur, h_next, acc_cur, acc_prev, *, sub, n_tiles):
    i, j = pl.program_id(0), pl.program_id(1)
    tm = h_cur.shape[0]
    rows = [slice(k * _FFN_SIDE_ROWS, (k + 1) * _FFN_SIDE_ROWS) for k in range(tm // _FFN_SIDE_ROWS)]
    first, second = rows[:len(rows) // 2], rows[len(rows) // 2:]
    in_range = i < n_tiles

    def prologue(r):
        h = _rms(xp_ref[r, :], g_ref[2 * sub:2 * sub + 1, :])
        h = h * (1.0 + modp_ref[3 * sub + 1:3 * sub + 2, :]) + modp_ref[3 * sub:3 * sub + 1, :]
        h_next[r, :] = h.astype(BF16)

    def epilogue(r):
        y = _rms(acc_prev[r, :], g_ref[2 * sub + 1:2 * sub + 2, :])
        o_ref[r, :] = xe_ref[r, :] + (0.5 * mode_ref[3 * sub + 2:3 * sub + 3, :]) * y

    def up(h):
        return jnp.dot(h, wgu_ref[...], preferred_element_type=F32)

    def down(gu):
        a = (_silu(gu[:, :FF_TILE]) * gu[:, FF_TILE:]).astype(BF16)
        return jnp.dot(a, wo_ref[...], preferred_element_type=F32)

    @pl.when((i == 0) & (j == 0))
    def _():
        for r in rows:
            prologue(r)
        acc_prev[...] = jnp.zeros_like(acc_prev)

    @pl.when((j == 0) & in_range)
    def _():
        h = h_next[...]
        gu = up(h)
        h_cur[...] = h
        for r in first:
            epilogue(r)
        part = down(gu)
        for r in second:
            epilogue(r)
        acc_cur[...] = part

    @pl.when((j == 0) & (i == n_tiles))
    def _():
        for r in rows:
            epilogue(r)

    @pl.when((j > 0) & (j < N_FF - 1) & in_range)
    def _():
        acc_cur[...] += down(up(h_cur[...]))

    @pl.when((j == N_FF - 1) & in_range)
    def _():
        gu = up(h_cur[...])
        for r in first:
            prologue(r)
        part = down(gu)
        for r in second:
            prologue(r)
        acc_prev[...] = acc_cur[...] + part


def _ffn(x, stream, mod, gains, w_gu, w_o, sub):
    tm = FFN_TM
    n_tiles = stream.n_tok // tm
    last = n_tiles - 1
    pro_tile = lambda i, j: jnp.minimum(i + (j == N_FF - 1).astype(jnp.int32), last)
    epi_tile = lambda i, j: jnp.clip(i - (j == 0).astype(jnp.int32), 0, last)
    chunk = lambda i, j: jnp.where(i == n_tiles, N_FF - 1, j)
    tile_spec = lambda tile: pl.BlockSpec((tm, D_MODEL), lambda i, j: (tile(i, j), 0))
    mod_spec = lambda tile: pl.BlockSpec((None, N_MOD, D_MODEL), lambda i, j: (stream.mod_row(tile(i, j), tm), 0, 0))
    return pl.pallas_call(
        functools.partial(_ffn_kernel, sub=sub, n_tiles=n_tiles),
        grid=(n_tiles + 1, N_FF),
        in_specs=[tile_spec(pro_tile), tile_spec(epi_tile), mod_spec(pro_tile), mod_spec(epi_tile),
                  pl.BlockSpec((6, D_MODEL), lambda i, j: (0, 0)),
                  pl.BlockSpec((D_MODEL, 2 * FF_TILE), lambda i, j: (0, chunk(i, j))),
                  pl.BlockSpec((FF_TILE, D_MODEL), lambda i, j: (chunk(i, j), 0))],
        out_specs=pl.BlockSpec((tm, D_MODEL), lambda i, j: (jnp.maximum(i - 1, 0), 0)),
        out_shape=jax.ShapeDtypeStruct((stream.n_tok, D_MODEL), F32),
        scratch_shapes=[pltpu.VMEM((tm, D_MODEL), BF16), pltpu.VMEM((tm, D_MODEL), BF16),
                        pltpu.VMEM((tm, D_MODEL), F32), pltpu.VMEM((tm, D_MODEL), F32)],
        compiler_params=_cparams(2),
        name=f"ffn{sub}_{stream.name}",
    )(x, x, mod, mod, gains, w_gu, w_o)


def _proj_kernel(x_ref, mod_ref, g_ref, w_ref, ws_ref, o_ref, os_ref, h_ref):
    j = pl.program_id(1)

    @pl.when(j == 0)
    def _():
        h = _rms(x_ref[...], g_ref[2:3, :])
        h = h * (1.0 + mod_ref[4:5, :]) + mod_ref[3:4, :]
        h_ref[...] = h.astype(BF16)
        os_ref[...] = jnp.dot(h_ref[...], ws_ref[...], preferred_element_type=F32)

    o_ref[...] = jnp.dot(h_ref[...], w_ref[...], preferred_element_type=F32)


def _mixer_proj(x, stream, mod, gains, w_main, w_small):
    tm, tn = PROJ_TM, PROJ_TN
    n_tok = stream.n_tok
    return pl.pallas_call(
        _proj_kernel,
        grid=(n_tok // tm, PROJ_MAIN // tn),
        in_specs=[pl.BlockSpec((tm, D_MODEL), lambda i, j: (i, 0)),
                  pl.BlockSpec((None, N_MOD, D_MODEL), lambda i, j: (stream.mod_row(i, tm), 0, 0)),
                  pl.BlockSpec((6, D_MODEL), lambda i, j: (0, 0)),
                  pl.BlockSpec((D_MODEL, tn), lambda i, j: (0, j)),
                  pl.BlockSpec((D_MODEL, PROJ_SMALL), lambda i, j: (0, 0))],
        out_specs=[pl.BlockSpec((tm, tn), lambda i, j: (i, j)),
                   pl.BlockSpec((tm, PROJ_SMALL), lambda i, j: (i, 0))],
        out_shape=[jax.ShapeDtypeStruct((n_tok, PROJ_MAIN), F32),
                   jax.ShapeDtypeStruct((n_tok, PROJ_SMALL), F32)],
        scratch_shapes=[pltpu.VMEM((tm, D_MODEL), BF16)],
        compiler_params=_cparams(2),
        name=f"mixer_proj_{stream.name}",
    )(x, mod, gains, w_main, w_small)


_NT = (((1,), (1,)), ((), ()))
_TN = (((0,), (0,)), ((), ()))


def _log_sigmoid(x):
    return jnp.minimum(x, 0.0) - jnp.log1p(jnp.exp(-jnp.abs(x)))


def _gla_kernel(*refs, seq, has_s0, emit_state):
    q_ref, k_ref, v_ref, r_ref, a_ref, wa_ref, ba_ref, gn_ref = refs[:8]
    pos = 8
    if has_s0:
        s0_refs = refs[pos:pos + 2]
        pos += 2
    o_ref = refs[pos]
    pos += 1
    if emit_state:
        s_out_refs = refs[pos:pos + 2]
        pos += 2
    qd_scr, ke_scr, vb_scr, dec_scr, o_scr, st_scr = refs[pos:pos + 6]

    c, blk_rows = GLA_CHUNK, GLA_BLOCK
    cpb = blk_rows // c
    nc = seq // c
    for d in range(2):
        st_scr[d] = s0_refs[d][...] if has_s0 else jnp.zeros((GLA_DK, GLA_DV), F32)

    row = lax.broadcasted_iota(jnp.int32, (blk_rows, blk_rows), 0)
    col = lax.broadcasted_iota(jnp.int32, (blk_rows, blk_rows), 1)
    same_chunk = (row // c) == (col // c)
    masks = (same_chunk & (col <= row), same_chunk & (col >= row))

    tri = [jnp.where(masks[d], 1.0, 0.0).astype(BF16) for d in range(2)]
    n_blocks = seq // blk_rows
    blocks_per_iter = 2 if n_blocks % 2 == 0 else 1

    def bulk(it, carry):
        blks = [it * blocks_per_iter + j for j in range(blocks_per_iter)]
        rows = [pl.ds(pl.multiple_of(blk * blk_rows, blk_rows), blk_rows) for blk in blks]
        nb = range(blocks_per_iter)
        a_in = [a_ref[rows[j], :].astype(BF16) for j in nb]
        q = [q_ref[rows[j], :] * (GLA_DK ** -0.5) for j in nb]
        k = [k_ref[rows[j], :] for j in nb]
        vb = [v_ref[rows[j], :].astype(BF16) for j in nb]
        units = [(j, d) for j in nb for d in range(2)]
        x = {(j, d): jnp.dot(a_in[j], wa_ref[d], preferred_element_type=F32) + ba_ref[d:d + 1, :] for j, d in units}
        la = {u: _log_sigmoid(x[u]) * (1.0 / GLA_TAU) for u in units}
        la1 = {u: la[u].astype(BF16) for u in units}
        rem = {u: la[u] - la1[u].astype(F32) for u in units}
        la2 = {u: rem[u].astype(BF16) for u in units}
        la3 = {u: (rem[u] - la2[u].astype(F32)).astype(BF16) for u in units}
        b = {(j, d): jnp.dot(tri[d], la1[j, d], preferred_element_type=F32)
             + jnp.dot(tri[d], la2[j, d], preferred_element_type=F32)
             + jnp.dot(tri[d], la3[j, d], preferred_element_type=F32) for j, d in units}
        b3 = {u: b[u].reshape(cpb, c, GLA_DK) for u in units}
        b_last = {(j, d): b3[j, d][:, c - 1:c, :] if d == 0 else b3[j, d][:, 0:1, :] for j, d in units}
        dec = {u: jnp.exp(b_last[u]) for u in units}
        dec_rows = {u: jnp.broadcast_to(dec[u], (cpb, c, GLA_DK)).reshape(blk_rows, GLA_DK) for u in units}
        q_dec = {(j, d): (q[j] * jnp.exp(b[j, d])).astype(BF16) for j, d in units}
        k_inv = {(j, d): k[j] * jnp.exp(-b[j, d]) for j, d in units}
        k_end = {u: (k_inv[u] * dec_rows[u]).astype(BF16) for u in units}
        a = {u: lax.dot_general(q_dec[u], k_inv[u].astype(BF16), _NT, preferred_element_type=F32) for u in units}
        a = {(j, d): jnp.where(masks[d], a[j, d], 0.0).astype(BF16) for j, d in units}
        o = {(j, d): jnp.dot(a[j, d], vb[j], preferred_element_type=F32) for j, d in units}
        for j in nb:
            vb_scr[rows[j], :] = vb[j]
        for j, d in units:
            o_scr[d, rows[j], :] = o[j, d]
            qd_scr[d, rows[j], :] = q_dec[j, d]
            ke_scr[d, rows[j], :] = k_end[j, d]
            dec8 = pl.ds(pl.multiple_of(blks[j] * (cpb * 8), cpb * 8), cpb * 8)
            dec_scr[d, dec8, :] = jnp.broadcast_to(dec[j, d], (cpb, 8, GLA_DK)).reshape(cpb * 8, GLA_DK)
        return carry

    lax.fori_loop(0, n_blocks // blocks_per_iter, bulk, 0)

    cpi = GLA_CHUNKS_PER_ITER

    def recur(i, carry):
        steps = [(u, d) for u in range(cpi) for d in range(2)]
        chunk = {(u, d): cpi * i + u if d == 0 else nc - 1 - (cpi * i + u) for u, d in steps}
        rows = {s: pl.ds(pl.multiple_of(chunk[s] * c, c), c) for s in steps}
        dec_rows = {s: pl.ds(pl.multiple_of(chunk[s] * 8, 8), 8) for s in steps}
        kv = {(u, d): lax.dot_general(ke_scr[d, rows[u, d], :], vb_scr[rows[u, d], :], _TN,
                                      preferred_element_type=F32) for u, d in steps}
        dec_col = {s: jnp.broadcast_to(dec_scr[s[1], dec_rows[s], :][0:1, :], (LANES, GLA_DK)).T for s in steps}
        dec_full = {s: jnp.concatenate([dec_col[s]] * (GLA_DV // LANES), axis=1) for s in steps}
        s_t = [st_scr[d] for d in range(2)]
        for u, d in steps:
            inter = jnp.dot(qd_scr[d, rows[u, d], :], s_t[d].astype(BF16), preferred_element_type=F32)
            o_scr[d, rows[u, d], :] += inter
            s_t[d] = s_t[d] * dec_full[u, d] + kv[u, d]
        for d in range(2):
            st_scr[d] = s_t[d]
        return carry

    lax.fori_loop(0, nc // cpi, recur, 0)

    def finish(i, carry):
        rows = pl.ds(pl.multiple_of(i * blk_rows, blk_rows), blk_rows)
        o = _rms(o_scr[0, rows, :] + o_scr[1, rows, :], gn_ref[...])
        o_ref[rows, :] = (o * _silu(r_ref[rows, :])).astype(BF16)
        return carry

    lax.fori_loop(0, seq // blk_rows, finish, 0)
    if emit_state:
        for d in range(2):
            s_out_refs[d][...] = st_scr[d]


def _gla(proj, small, wa, ba, gn, *, seq, n_batch, row0, s0=None, emit_state=False):
    rb = row0 // seq
    dk, dv = GLA_DK, GLA_DV
    in_specs = [pl.BlockSpec((seq, dk), lambda b, h: (rb + b, COL_Q // dk + h)),
                pl.BlockSpec((seq, dk), lambda b, h: (rb + b, COL_K // dk + h)),
                pl.BlockSpec((seq, dv), lambda b, h: (rb + b, COL_V // dv + h)),
                pl.BlockSpec((seq, dv), lambda b, h: (rb + b, COL_R // dv + h)),
                pl.BlockSpec((seq, LANES), lambda b, h: (rb + b, 1)),
                pl.BlockSpec((2, LANES, dk), lambda b, h: (0, 0, h)),
                pl.BlockSpec((2, dk), lambda b, h: (0, h)),
                pl.BlockSpec((1, dv), lambda b, h: (0, 0))]
    args = [proj, proj, proj, proj, small, wa, ba, gn]
    state_spec = pl.BlockSpec((None, None, dk, dv), lambda b, h: (b, h, 0, 0))
    if s0 is not None:
        in_specs += [state_spec, state_spec]
        args += list(s0)
    out_specs = [pl.BlockSpec((seq, dv), lambda b, h: (b, h))]
    out_shape = [jax.ShapeDtypeStruct((n_batch * seq, GLA_HEADS * dv), BF16)]
    if emit_state:
        out_specs += [state_spec, state_spec]
        out_shape += [jax.ShapeDtypeStruct((n_batch, GLA_HEADS, dk, dv), F32)] * 2
    return pl.pallas_call(
        functools.partial(_gla_kernel, seq=seq, has_s0=s0 is not None, emit_state=emit_state),
        grid=(n_batch, GLA_HEADS),
        in_specs=in_specs, out_specs=out_specs, out_shape=out_shape,
        scratch_shapes=[pltpu.VMEM((2, seq, dk), BF16),
                        pltpu.VMEM((2, seq, dk), BF16),
                        pltpu.VMEM((seq, dv), BF16),
                        pltpu.VMEM((2, seq // GLA_CHUNK * 8, dk), F32),
                        pltpu.VMEM((2, seq, dv), F32),
                        pltpu.VMEM((2, dk, dv), F32)],
        compiler_params=_cparams(2),
        name=f"gla_{seq}",
    )(*args)


def _rope_block(x, cs_ref):
    return x * cs_ref[:, :LANES] + pltpu.roll(x, MLA_ROPE, axis=1) * cs_ref[:, LANES:]


def _q_kernel(cq_ref, qn_ref, w_ref, cs_ref, o_ref):
    n = _rms(cq_ref[...], qn_ref[...]).astype(BF16)
    for h in range(MLA_HEADS):
        q = jnp.dot(n, w_ref[:, h * HEAD_W:(h + 1) * HEAD_W], preferred_element_type=F32)
        o_ref[:, h * HEAD_W:h * HEAD_W + LANES] = q[:, :LANES].astype(BF16)
        o_ref[:, h * HEAD_W + LANES:(h + 1) * HEAD_W] = _rope_block(q[:, LANES:], cs_ref).astype(BF16)


def _q_proj(proj, stream, q_norm, w_q, rope_tab):
    tm = MLA_TM
    return pl.pallas_call(
        _q_kernel,
        grid=(stream.n_tok // tm,),
        in_specs=[pl.BlockSpec((tm, Q_LORA), lambda i: (i, COL_CQ // Q_LORA)),
                  _resident((1, Q_LORA)),
                  _resident((Q_LORA, MLA_HEADS * HEAD_W)),
                  pl.BlockSpec((tm, 2 * LANES), lambda i: (stream.rope_row(i, tm), 0))],
        out_specs=pl.BlockSpec((tm, MLA_HEADS * HEAD_W), lambda i: (i, 0)),
        out_shape=jax.ShapeDtypeStruct((stream.n_tok, MLA_HEADS * HEAD_W), BF16),
        compiler_params=_cparams(1),
        name=f"mla_q_{stream.name}",
    )(proj, q_norm, w_q, rope_tab)


def _kv_kernel(*refs, norm, emit_ckv):
    ckv_ref, kn_ref, kr_ref, w_ref, cs_ref, k_ref, v_ref = refs[:7]
    c = ckv_ref[...]
    if norm:
        c = _rms(c, kn_ref[...])
    if emit_ckv:
        refs[7][...] = c
    cb = c.astype(BF16)
    nk = MLA_HEADS * MLA_NOPE
    kr = _rope_block(kr_ref[...], cs_ref).astype(BF16)
    v_ref[...] = jnp.dot(cb, w_ref[:, nk:], preferred_element_type=F32).astype(BF16)
    for h in range(MLA_HEADS):
        kn = jnp.dot(cb, w_ref[:, h * MLA_NOPE:(h + 1) * MLA_NOPE], preferred_element_type=F32)
        k_ref[:, h * HEAD_W:h * HEAD_W + LANES] = kn.astype(BF16)
        k_ref[:, h * HEAD_W + LANES:(h + 1) * HEAD_W] = kr


def _kv_proj(ckv_src, ckv_col, kr_src, kv_norm, w_kv, rope_tab, *, n_rows, row0, rope_row, norm, emit_ckv):
    tm = MLA_TM
    r0 = row0 // tm
    out_specs = [pl.BlockSpec((tm, MLA_HEADS * HEAD_W), lambda i: (i, 0)),
                 pl.BlockSpec((tm, MLA_HEADS * MLA_V), lambda i: (i, 0))]
    out_shape = [jax.ShapeDtypeStruct((n_rows, MLA_HEADS * HEAD_W), BF16),
                 jax.ShapeDtypeStruct((n_rows, MLA_HEADS * MLA_V), BF16)]
    if emit_ckv:
        out_specs.append(pl.BlockSpec((tm, KV_LORA), lambda i: (i, 0)))
        out_shape.append(jax.ShapeDtypeStruct((n_rows, KV_LORA), F32))
    return pl.pallas_call(
        functools.partial(_kv_kernel, norm=norm, emit_ckv=emit_ckv),
        grid=(n_rows // tm,),
        in_specs=[pl.BlockSpec((tm, KV_LORA), lambda i: (r0 + i, ckv_col)),
                  _resident((1, KV_LORA)),
                  pl.BlockSpec((tm, LANES), lambda i: (r0 + i, 0)),
                  _resident((KV_LORA, MLA_HEADS * (MLA_NOPE + MLA_V))),
                  pl.BlockSpec((tm, 2 * LANES), lambda i: (rope_row(i), 0))],
        out_specs=out_specs, out_shape=out_shape,
        compiler_params=_cparams(1),
        name=f"mla_kv_{row0}_{n_rows}",
    )(ckv_src, kv_norm, kr_src, w_kv, rope_tab)


_EXP2_SCALE = float((MLA_NOPE + MLA_ROPE) ** -0.5 * np.log2(np.e))


def _attn_kernel(*refs, n_seg, heads, tq, lookahead):
    q_ref = refs[0]
    kv = refs[1:1 + 2 * n_seg]
    o_ref = refs[1 + 2 * n_seg]
    nt = q_ref.shape[0] // tq
    work = [(h, t) for h in range(heads) for t in range(nt)]

    def scores(h, t):
        hs = slice(h * HEAD_W, (h + 1) * HEAD_W)
        q = q_ref[t * tq:(t + 1) * tq, hs]
        return [lax.dot_general(q, kv[2 * g][:, hs], _NT, preferred_element_type=F32) for g in range(n_seg)]

    def row_max(s):
        m = s[0].max(axis=-1, keepdims=True)
        for g in range(1, n_seg):
            m = jnp.maximum(m, s[g].max(axis=-1, keepdims=True))
        return m

    def probs(s, m):
        return [jnp.exp2((s[g] - m) * _EXP2_SCALE).astype(BF16) for g in range(n_seg)]

    def values(g, h):
        v = kv[2 * g + 1][:, h * MLA_V:(h + 1) * MLA_V]
        one_col = jnp.where(lax.broadcasted_iota(jnp.int32, v.shape, 1) == 0, 1.0, 0.0).astype(BF16)
        return jnp.concatenate([v, one_col], axis=1)

    v_aug = {(g, h): values(g, h) for g in range(n_seg) for h in range(heads)}

    def weighted(p, h):
        acc = 0.0
        for g in range(n_seg):
            acc = acc + jnp.dot(p[g], v_aug[g, h], preferred_element_type=F32)
        return acc

    def store(acc, h, t):
        out = acc[:, :MLA_V] / acc[:, MLA_V:MLA_V + 1]
        o_ref[t * tq:(t + 1) * tq, h * MLA_V:(h + 1) * MLA_V] = out.astype(BF16)

    if lookahead >= len(work):
        s = [scores(*w) for w in work]
        m = [row_max(x) for x in s]
        p = [probs(x, y) for x, y in zip(s, m)]
        acc = [weighted(x, w[0]) for x, w in zip(p, work)]
        for x, w in zip(acc, work):
            store(x, *w)
    else:
        s_next = scores(*work[0])
        for idx, (h, t) in enumerate(work):
            s = s_next
            if idx + 1 < len(work):
                s_next = scores(*work[idx + 1])
            store(weighted(probs(s, row_max(s)), h), h, t)


def _attention(q, q_row0, segs, *, n_batch, seq, tq, heads):
    qb0 = q_row0 // seq
    n_hblk = MLA_HEADS // heads
    in_specs = [pl.BlockSpec((seq, heads * HEAD_W), lambda b, h: (qb0 + b, h))]
    args = [q]
    for keys, vals, n_keys in segs:
        in_specs += [pl.BlockSpec((n_keys, heads * HEAD_W), lambda b, h: (b, h)),
                     pl.BlockSpec((n_keys, heads * MLA_V), lambda b, h: (b, h))]
        args += [keys, vals]
    return pl.pallas_call(
        functools.partial(_attn_kernel, n_seg=len(segs), heads=heads, tq=tq,
                          lookahead=1 if seq > tq else heads),
        grid=(n_batch, n_hblk),
        in_specs=in_specs,
        out_specs=pl.BlockSpec((seq, heads * MLA_V), lambda b, h: (b, h)),
        out_shape=jax.ShapeDtypeStruct((n_batch * seq, MLA_HEADS * MLA_V), BF16),
        compiler_params=_cparams(2),
        name=f"mla_attn_{seq}",
    )(*args)


def _merge_kernel(og_ref, om_ref, ga_ref, gb_ref, x_ref, mod_ref, g_ref, wg_ref, wm_ref, wo_ref, o_ref):
    yg = jnp.dot(og_ref[...], wg_ref[...], preferred_element_type=F32)
    ym = jnp.dot(om_ref[...], wm_ref[...], preferred_element_type=F32)
    m = (jax.nn.sigmoid(ga_ref[...]) * yg + jax.nn.sigmoid(gb_ref[...]) * ym).astype(BF16)
    y = _rms(jnp.dot(m, wo_ref[...], preferred_element_type=F32), g_ref[3:4, :])
    o_ref[...] = x_ref[...] + mod_ref[5:6, :] * y


def _merge(o_gla, o_mla, proj, x, stream, mod, gains, w_gla_out, w_mla_out, w_out):
    tm = MERGE_TM
    tile = lambda c0: pl.BlockSpec((tm, D_MODEL), lambda i: (i, c0 // D_MODEL))
    return pl.pallas_call(
        _merge_kernel,
        grid=(stream.n_tok // tm,),
        in_specs=[tile(0), tile(0), tile(COL_GA), tile(COL_GB), tile(0),
                  pl.BlockSpec((None, N_MOD, D_MODEL), lambda i: (stream.mod_row(i, tm), 0, 0)),
                  _resident((6, D_MODEL)),
                  _resident((D_MODEL, D_MODEL)), _resident((D_MODEL, D_MODEL)), _resident((D_MODEL, D_MODEL))],
        out_specs=tile(0),
        out_shape=jax.ShapeDtypeStruct((stream.n_tok, D_MODEL), F32),
        compiler_params=_cparams(1),
        name=f"mixer_merge_{stream.name}",
    )(o_gla, o_mla, proj, proj, x, mod, gains, w_gla_out, w_mla_out, w_out)


_SWAP = np.arange(MLA_ROPE)
_SWAP = np.where(_SWAP % 32 < 16, _SWAP + 16, _SWAP - 16)


_FF_BLOCKS = D_FF // LANES
_U_PER_STEP = FF_TILE // LANES


def _prep_ffn_in_kernel(g_ref, *refs):
    u_refs, o_ref = refs[:-1], refs[-1]
    j = pl.program_id(0)
    col = j * FF_TILE + lax.broadcasted_iota(jnp.int32, (1, FF_TILE), 1)
    o_ref[:, :FF_TILE] = jnp.where(col < D_FF, g_ref[...], 0.0).astype(BF16)
    lane = lax.broadcasted_iota(jnp.int32, (1, LANES), 1)
    for q, u_ref in enumerate(u_refs):
        ucol = j * FF_TILE + q * LANES + lane
        o_ref[:, FF_TILE + q * LANES:FF_TILE + (q + 1) * LANES] = jnp.where(ucol < D_FF, u_ref[...], 0.0).astype(BF16)


def _prep_ffn_in(w, name):
    last = 2 * _FF_BLOCKS - 1
    u_spec = lambda q: pl.BlockSpec((D_MODEL, LANES),
                                    lambda j: (0, jnp.minimum(_FF_BLOCKS + _U_PER_STEP * j + q, last)))
    return pl.pallas_call(
        _prep_ffn_in_kernel,
        grid=(N_FF,),
        in_specs=[pl.BlockSpec((D_MODEL, FF_TILE), lambda j: (0, j))] + [u_spec(q) for q in range(_U_PER_STEP)],
        out_specs=pl.BlockSpec((D_MODEL, 2 * FF_TILE), lambda j: (0, j)),
        out_shape=jax.ShapeDtypeStruct((D_MODEL, N_FF * 2 * FF_TILE), BF16),
        compiler_params=_cparams(1),
        name=name,
    )(w, *([w] * _U_PER_STEP))


_FF_FULL = D_FF // FF_TILE
_FF_TAIL = (D_FF - _FF_FULL * FF_TILE) // LANES


def _prep_ffn_out_kernel(w_ref, *refs):
    tail_refs, o_ref = refs[:-1], refs[-1]
    j = pl.program_id(0)

    @pl.when(j < _FF_FULL)
    def _():
        o_ref[...] = w_ref[...].astype(BF16)

    @pl.when(j == _FF_FULL)
    def _():
        for t, t_ref in enumerate(tail_refs):
            o_ref[t * LANES:(t + 1) * LANES, :] = t_ref[...].astype(BF16)
        o_ref[_FF_TAIL * LANES:, :] = jnp.zeros((FF_TILE - _FF_TAIL * LANES, D_MODEL), BF16)


def _prep_ffn_out(w, name):
    tail = lambda t: pl.BlockSpec((LANES, D_MODEL), lambda j: (_FF_FULL * (FF_TILE // LANES) + t, 0))
    return pl.pallas_call(
        _prep_ffn_out_kernel,
        grid=(N_FF,),
        in_specs=[pl.BlockSpec((FF_TILE, D_MODEL), lambda j: (jnp.minimum(j, _FF_FULL - 1), 0))]
        + [tail(t) for t in range(_FF_TAIL)],
        out_specs=pl.BlockSpec((FF_TILE, D_MODEL), lambda j: (j, 0)),
        out_shape=jax.ShapeDtypeStruct((FF_PAD, D_MODEL), BF16),
        compiler_params=_cparams(1),
        name=name,
    )(w, *([w] * _FF_TAIL))


_W_IN_GROUPS = ((0, 6144), (7264, 2048), (9312, 2048), (6176, 512), (6688, 512))
_W_IN_BLOCK = 512
_W_IN_SRC = np.array([s + o for s, w in _W_IN_GROUPS for o in range(0, w, _W_IN_BLOCK)], np.int32)
_W_IN_NSRC = _W_IN_BLOCK // LANES + 1


def _prep_w_in_kernel(src_ref, *refs):
    srcs, o_ref = refs[:-1], refs[-1]
    shift = src_ref[pl.program_id(0)] % LANES

    @pl.when(shift == 0)
    def _():
        for k in range(_W_IN_NSRC - 1):
            o_ref[:, k * LANES:(k + 1) * LANES] = srcs[k][...].astype(BF16)

    @pl.when(shift != 0)
    def _():
        back = LANES - shift
        lane = lax.broadcasted_iota(jnp.int32, (1, LANES), 1)
        rolled = [pltpu.roll(s[...], back, axis=1) for s in srcs]
        for k in range(_W_IN_NSRC - 1):
            o_ref[:, k * LANES:(k + 1) * LANES] = jnp.where(lane < back, rolled[k], rolled[k + 1]).astype(BF16)


def _prep_w_in(w):
    last = (w.shape[1] - 1) // LANES
    src = lambda k: pl.BlockSpec((D_MODEL, LANES), lambda j, tab: (0, jnp.minimum(tab[j] // LANES + k, last)))
    main = pl.pallas_call(
        _prep_w_in_kernel,
        grid_spec=pltpu.PrefetchScalarGridSpec(
            num_scalar_prefetch=1,
            grid=(PROJ_MAIN // _W_IN_BLOCK,),
            in_specs=[src(k) for k in range(_W_IN_NSRC)],
            out_specs=pl.BlockSpec((D_MODEL, _W_IN_BLOCK), lambda j, tab: (0, j))),
        out_shape=jax.ShapeDtypeStruct((D_MODEL, PROJ_MAIN), BF16),
        compiler_params=_cparams(1),
        name="prep_w_in",
    )(jnp.asarray(_W_IN_SRC), *([w] * _W_IN_NSRC))
    sl = lambda a, n: w[:, a:a + n]
    af, ab, kr = sl(6144, 16), sl(6160, 16), sl(7200, 64)
    small = jnp.concatenate([kr, kr[:, _SWAP], af, ab,
                             jnp.zeros((D_MODEL, PROJ_SMALL - 2 * MLA_ROPE - 2 * GLA_RANK), F32)], axis=1).astype(BF16)
    return main, small


def _prep_w_uq(w):
    w = w.astype(BF16).reshape(Q_LORA, MLA_HEADS, MLA_NOPE + MLA_ROPE)
    rope = w[..., MLA_NOPE:]
    return jnp.concatenate([w[..., :MLA_NOPE], rope, rope[..., _SWAP]], axis=-1).reshape(Q_LORA, MLA_HEADS * HEAD_W)


def _prep_w_ukv(w):
    w = w.astype(BF16).reshape(KV_LORA, MLA_HEADS, MLA_NOPE + MLA_V)
    return jnp.concatenate([w[..., :MLA_NOPE].reshape(KV_LORA, -1), w[..., MLA_NOPE:].reshape(KV_LORA, -1)], axis=1)


def _prep_w_alpha(w):
    out = jnp.zeros((2, LANES, GLA_HEADS * GLA_DK), BF16)
    out = out.at[0, 0:GLA_RANK].set(w[0].astype(BF16))
    return out.at[1, GLA_RANK:2 * GLA_RANK].set(w[1].astype(BF16))


def _rope_table(tm):
    rows = DEC_SEQ // GRID_W
    row = jnp.repeat(jnp.arange(rows), GRID_W).astype(F32)
    col = jnp.tile(jnp.arange(GRID_W), rows).astype(F32)
    half = MLA_ROPE // 2
    inv_freq = ROPE_THETA ** (-jnp.arange(0, half, 2, dtype=F32) / half)
    ang_r, ang_c = row[:, None] * inv_freq, col[:, None] * inv_freq
    cos = jnp.concatenate([jnp.cos(ang_r), jnp.cos(ang_r), jnp.cos(ang_c), jnp.cos(ang_c)], axis=1)
    sin = jnp.concatenate([-jnp.sin(ang_r), jnp.sin(ang_r), -jnp.sin(ang_c), jnp.sin(ang_c)], axis=1)
    z = jnp.zeros((DEC_SEQ, LANES - MLA_ROPE), F32)
    lat = jnp.concatenate([cos, z, sin, z], axis=1)
    ident = jnp.concatenate([jnp.ones((tm, MLA_ROPE), F32), jnp.zeros((tm, 2 * LANES - MLA_ROPE), F32)], axis=1)
    return jnp.concatenate([ident, lat], axis=0)


def kernel(x_prompt, x_sample, cache_ckv, cache_krope, state_gla_fwd, state_gla_bwd, c, c_ctx, w_ada, b_ada, norm_gains, w_ffn1_in, w_ffn1_out, w_ffn2_in, w_ffn2_out, w_in, w_gla_alpha, b_gla_alpha, gla_norm, w_gla_out, q_norm, kv_norm, w_uq, w_ukv, w_mla_out, w_out):
    c_all = jnp.concatenate([c_ctx[None, :], c, jnp.zeros((8 - 1 - DEC_BATCH, D_MODEL), F32)], axis=0)
    mod = _modulation(c_all, w_ada[0], b_ada[0]).reshape(8, N_MOD, D_MODEL)
    gains = norm_gains[0]

    streams = (_CTX, _LAT)
    x = {_CTX: x_prompt.reshape(N_CTX, D_MODEL), _LAT: x_sample.reshape(N_LAT, D_MODEL)}

    w_gu, w_o = _prep_ffn_in(w_ffn1_in[0], "prep_ffn0_in"), _prep_ffn_out(w_ffn1_out[0], "prep_ffn0_out")
    x = {s: _ffn(x[s], s, mod, gains, w_gu, w_o, sub=0) for s in streams}

    w_main, w_small = _prep_w_in(w_in[0])
    proj, small = {}, {}
    for s in streams:
        proj[s], small[s] = _mixer_proj(x[s], s, mod, gains, w_main, w_small)

    wa = _prep_w_alpha(w_gla_alpha[0])
    ba = b_gla_alpha[0]
    gn = gla_norm[0].reshape(1, GLA_DV)
    o_gla = {}
    o_gla[_CTX], s_f, s_b = _gla(proj[_CTX], small[_CTX], wa, ba, gn, seq=SEQ, n_batch=BATCH, row0=0,
                                 emit_state=True)
    (o_gla[_LAT],) = _gla(proj[_LAT], small[_LAT], wa, ba, gn, seq=DEC_SEQ, n_batch=DEC_BATCH, row0=0,
                          s0=(state_gla_fwd[:, 0], state_gla_bwd[:, 0]))

    tm = MLA_TM
    rope_tab = _rope_table(tm)
    w_q = _prep_w_uq(w_uq[0])
    q = {s: _q_proj(proj[s], s, q_norm[0].reshape(1, Q_LORA), w_q, rope_tab) for s in streams}
    w_kv = _prep_w_ukv(w_ukv[0])
    kvn = kv_norm[0].reshape(1, KV_LORA)
    ckv_col = COL_CKV // KV_LORA
    k_ctx, v_ctx, ckv_new = _kv_proj(proj[_CTX], ckv_col, small[_CTX], kvn, w_kv, rope_tab, n_rows=N_CTX, row0=0,
                                     rope_row=lambda i: _CTX.rope_row(i, tm), norm=True, emit_ckv=True)
    k_lat, v_lat = _kv_proj(proj[_LAT], ckv_col, small[_LAT], kvn, w_kv, rope_tab, n_rows=N_LAT, row0=0,
                            rope_row=lambda i: _LAT.rope_row(i, tm), norm=True, emit_ckv=False)
    n_past = DEC_BATCH * PAST_LEN
    kr_past = jnp.pad(cache_krope[:, 0].reshape(n_past, MLA_ROPE), ((0, 0), (0, LANES - MLA_ROPE)))
    k_past, v_past = _kv_proj(cache_ckv[:, 0].reshape(n_past, KV_LORA), 0, kr_past, kvn, w_kv, rope_tab,
                              n_rows=n_past, row0=0, rope_row=lambda i: 0 * i, norm=False, emit_ckv=False)
    o_mla = {
        _CTX: _attention(q[_CTX], 0, [(k_ctx, v_ctx, SEQ)], n_batch=BATCH, seq=SEQ, tq=SEQ, heads=MLA_HEADS),
        _LAT: _attention(q[_LAT], 0, [(k_past, v_past, PAST_LEN), (k_lat, v_lat, DEC_SEQ)],
                         n_batch=DEC_BATCH, seq=DEC_SEQ, tq=ATT_TQ, heads=1)}

    w_merge = (w_gla_out[0].astype(BF16), w_mla_out[0].astype(BF16), w_out[0].astype(BF16))
    x = {s: _merge(o_gla[s], o_mla[s], proj[s], x[s], s, mod, gains, *w_merge) for s in streams}
    w_gu, w_o = _prep_ffn_in(w_ffn2_in[0], "prep_ffn2_in"), _prep_ffn_out(w_ffn2_out[0], "prep_ffn2_out")
    y = {s: _ffn(x[s], s, mod, gains, w_gu, w_o, sub=2) for s in streams}

    y_prompt = y[_CTX].reshape(BATCH, SEQ, D_MODEL)
    y_sample = y[_LAT].reshape(DEC_BATCH, DEC_SEQ, D_MODEL)
    new_ckv = ckv_new.reshape(BATCH, 1, SEQ, KV_LORA)
    new_krope = small[_CTX][:, :MLA_ROPE].reshape(BATCH, 1, SEQ, MLA_ROPE)
    return (y_prompt, y_sample, new_ckv, new_krope,
            s_f.reshape(BATCH, 1, GLA_HEADS, GLA_DK, GLA_DV), s_b.reshape(BATCH, 1, GLA_HEADS, GLA_DK, GLA_DV))
```

```python
import functools
from typing import NamedTuple

import numpy as np
import jax
import jax.numpy as jnp
from jax import lax
from jax.experimental import pallas as pl
from jax.experimental.pallas import tpu as pltpu

F32 = jnp.float32
BF16 = jnp.bfloat16

D_MODEL = 2048
BATCH, SEQ = 16, 256
DEC_BATCH, DEC_SEQ = 4, 2048
PAST_LEN = 256
GRID_W = 64
D_FF = 5504
N_MOD = 9
GLA_HEADS, GLA_DK, GLA_DV = 4, 256, 512
GLA_RANK = 16
GLA_TAU = 16.0
GLA_CHUNK = 64
GLA_BLOCK = 256
GLA_CHUNKS_PER_ITER = 4
MLA_HEADS, MLA_NOPE, MLA_ROPE, MLA_V = 16, 128, 64, 128
Q_LORA = KV_LORA = 512
ROPE_THETA = 10000.0
NORM_EPS = 1e-6

N_CTX = BATCH * SEQ
N_LAT = DEC_BATCH * DEC_SEQ
N_TOK = N_CTX + N_LAT

LANES = 128
V7X_VMEM_BYTES = 64 * 1024 * 1024
V7X_VMEM_LIMIT_BYTES = 60000 * 1024

FF_TILE = 512
FF_PAD = -(-D_FF // FF_TILE) * FF_TILE
N_FF = FF_PAD // FF_TILE
FFN_TM = 1024
FFN_ROWS = 512
PROJ_TM, PROJ_TN = 1024, 1024
MLA_TM = 512
MERGE_TM = 256
ATT_TQ = 512
HEAD_W = 2 * LANES

COL_Q, COL_K, COL_V, COL_R = 0, 1024, 2048, 4096
COL_GA, COL_GB, COL_CQ, COL_CKV = 6144, 8192, 10240, 10752
PROJ_MAIN = 11264
PROJ_SMALL = 256


class _Stream(NamedTuple):
    name: str
    n_batch: int
    seq: int
    mod_base: int
    per_batch_mod: bool
    rope: bool

    @property
    def n_tok(self):
        return self.n_batch * self.seq

    def mod_row(self, tile, tm):
        return self.mod_base + (tile * tm // self.seq if self.per_batch_mod else 0 * tile)

    def rope_row(self, tile, tm):
        return 1 + tile % (self.seq // tm) if self.rope else 0 * tile


_CTX = _Stream("ctx", BATCH, SEQ, 0, False, False)
_LAT = _Stream("lat", DEC_BATCH, DEC_SEQ, 1, True, True)


def _rms(x, gain):
    ms = jnp.mean(x * x, axis=-1, keepdims=True)
    return x * lax.rsqrt(ms + NORM_EPS) * gain


def _silu(x):
    return x * jax.nn.sigmoid(x)


_NT = (((1,), (1,)), ((), ()))
_TN = (((0,), (0,)), ((), ()))


def _cparams(n_axes, vmem_limit_bytes=V7X_VMEM_LIMIT_BYTES):
    return pltpu.CompilerParams(dimension_semantics=("arbitrary",) * n_axes, vmem_limit_bytes=vmem_limit_bytes)


def _resident(shape):
    nd = len(shape)
    return pl.BlockSpec(shape, lambda *_: (0,) * nd, pipeline_mode=pl.Buffered(1))


def _mod_kernel(c_ref, w_ref, b_ref, o_ref):
    s = _silu(c_ref[...]).astype(BF16)
    o_ref[...] = jnp.dot(s, w_ref[...].astype(BF16), preferred_element_type=F32) + b_ref[...]


def _modulation(c_all, w_ada, b_ada):
    n = w_ada.shape[1]
    tn = 1024
    return pl.pallas_call(
        _mod_kernel,
        grid=(n // tn,),
        in_specs=[pl.BlockSpec((8, D_MODEL), lambda j: (0, 0)),
                  pl.BlockSpec((D_MODEL, tn), lambda j: (0, j)),
                  pl.BlockSpec((1, tn), lambda j: (0, j))],
        out_specs=pl.BlockSpec((8, tn), lambda j: (0, j)),
        out_shape=jax.ShapeDtypeStruct((8, n), F32),
        compiler_params=_cparams(1),
        name="adaln_mod",
    )(c_all, w_ada, b_ada.reshape(1, n))


def _ffn_kernel(x_ref, mod_ref, g_ref, wgu_ref, wo_ref, o_ref, h_ref, *, sub):
    j = pl.program_id(1)

    @pl.when(j == 0)
    def _():
        h = _rms(x_ref[...], g_ref[2 * sub:2 * sub + 1, :])
        h = h * (1.0 + mod_ref[3 * sub + 1:3 * sub + 2, :]) + mod_ref[3 * sub:3 * sub + 1, :]
        h_ref[...] = h.astype(BF16)
        o_ref[...] = jnp.zeros_like(o_ref)

    for r in range(0, h_ref.shape[0], FFN_ROWS):
        rows = slice(r, r + FFN_ROWS)
        gu = jnp.dot(h_ref[rows, :], wgu_ref[...], preferred_element_type=F32)
        a = (_silu(gu[:, :FF_TILE]) * gu[:, FF_TILE:]).astype(BF16)
        o_ref[rows, :] += jnp.dot(a, wo_ref[...], preferred_element_type=F32)

    @pl.when(j == pl.num_programs(1) - 1)
    def _():
        y = _rms(o_ref[...], g_ref[2 * sub + 1:2 * sub + 2, :])
        o_ref[...] = x_ref[...] + (0.5 * mod_ref[3 * sub + 2:3 * sub + 3, :]) * y


def _ffn(x, stream, mod, gains, w_gu, w_o, sub):
    tm = FFN_TM
    return pl.pallas_call(
        functools.partial(_ffn_kernel, sub=sub),
        grid=(stream.n_tok // tm, N_FF),
        in_specs=[pl.BlockSpec((tm, D_MODEL), lambda i, j: (i, 0)),
                  pl.BlockSpec((None, N_MOD, D_MODEL), lambda i, j: (stream.mod_row(i, tm), 0, 0)),
                  pl.BlockSpec((6, D_MODEL), lambda i, j: (0, 0)),
                  pl.BlockSpec((D_MODEL, 2 * FF_TILE), lambda i, j: (0, j)),
                  pl.BlockSpec((FF_TILE, D_MODEL), lambda i, j: (j, 0))],
        out_specs=pl.BlockSpec((tm, D_MODEL), lambda i, j: (i, 0)),
        out_shape=jax.ShapeDtypeStruct((stream.n_tok, D_MODEL), F32),
        scratch_shapes=[pltpu.VMEM((tm, D_MODEL), BF16)],
        compiler_params=_cparams(2, V7X_VMEM_BYTES - 2 * 1024 * 1024),
        name=f"ffn{sub}_{stream.name}",
    )(x, mod, gains, w_gu, w_o)


def _proj_kernel(x_ref, mod_ref, g_ref, wt_ref, wst_ref, o_ref, os_ref, h_ref):
    j = pl.program_id(1)

    @pl.when(j == 0)
    def _():
        h = _rms(x_ref[...], g_ref[2:3, :])
        h = h * (1.0 + mod_ref[4:5, :]) + mod_ref[3:4, :]
        h_ref[...] = h.astype(BF16)
        os_ref[...] = lax.dot_general(h_ref[...], wst_ref[...], _NT, preferred_element_type=F32)

    o_ref[...] = lax.dot_general(h_ref[...], wt_ref[...], _NT, preferred_element_type=F32)


def _mixer_proj(x, stream, mod, gains, w_main, w_small):
    tm, tn = PROJ_TM, PROJ_TN
    n_tok = stream.n_tok
    return pl.pallas_call(
        _proj_kernel,
        grid=(n_tok // tm, PROJ_MAIN // tn),
        in_specs=[pl.BlockSpec((tm, D_MODEL), lambda i, j: (i, 0)),
                  pl.BlockSpec((None, N_MOD, D_MODEL), lambda i, j: (stream.mod_row(i, tm), 0, 0)),
                  pl.BlockSpec((6, D_MODEL), lambda i, j: (0, 0)),
                  pl.BlockSpec((tn, D_MODEL), lambda i, j: (j, 0)),
                  pl.BlockSpec((PROJ_SMALL, D_MODEL), lambda i, j: (0, 0))],
        out_specs=[pl.BlockSpec((tm, tn), lambda i, j: (i, j)),
                   pl.BlockSpec((tm, PROJ_SMALL), lambda i, j: (i, 0))],
        out_shape=[jax.ShapeDtypeStruct((n_tok, PROJ_MAIN), F32),
                   jax.ShapeDtypeStruct((n_tok, PROJ_SMALL), F32)],
        scratch_shapes=[pltpu.VMEM((tm, D_MODEL), BF16)],
        compiler_params=_cparams(2),
        name=f"mixer_proj_{stream.name}",
    )(x, mod, gains, w_main, w_small)


def _log_sigmoid(x):
    return jnp.minimum(x, 0.0) - jnp.log1p(jnp.exp(-jnp.abs(x)))


def _gla_kernel(*refs, seq, has_s0, emit_state):
    q_ref, k_ref, v_ref, r_ref, a_ref, wa_ref, ba_ref, gn_ref = refs[:8]
    pos = 8
    if has_s0:
        s0_refs = refs[pos:pos + 2]
        pos += 2
    o_ref = refs[pos]
    pos += 1
    if emit_state:
        s_out_refs = refs[pos:pos + 2]
        pos += 2
    qd_scr, ke_scr, vb_scr, dec_scr, o_scr, st_scr = refs[pos:pos + 6]

    c, blk_rows = GLA_CHUNK, GLA_BLOCK
    cpb = blk_rows // c
    nc = seq // c
    for d in range(2):
        st_scr[d] = s0_refs[d][...] if has_s0 else jnp.zeros((GLA_DK, GLA_DV), F32)

    row = lax.broadcasted_iota(jnp.int32, (blk_rows, blk_rows), 0)
    col = lax.broadcasted_iota(jnp.int32, (blk_rows, blk_rows), 1)
    same_chunk = (row // c) == (col // c)
    masks = (same_chunk & (col <= row), same_chunk & (col >= row))

    tri = [jnp.where(masks[d], 1.0, 0.0).astype(BF16) for d in range(2)]
    n_blocks = seq // blk_rows
    blocks_per_iter = 2 if n_blocks % 2 == 0 else 1

    def bulk(it, carry):
        blks = [it * blocks_per_iter + j for j in range(blocks_per_iter)]
        rows = [pl.ds(pl.multiple_of(blk * blk_rows, blk_rows), blk_rows) for blk in blks]
        nb = range(blocks_per_iter)
        a_in = [a_ref[rows[j], :].astype(BF16) for j in nb]
        q = [q_ref[rows[j], :] * (GLA_DK ** -0.5) for j in nb]
        k = [k_ref[rows[j], :] for j in nb]
        vb = [v_ref[rows[j], :].astype(BF16) for j in nb]
        units = [(j, d) for j in nb for d in range(2)]
        x = {(j, d): jnp.dot(a_in[j], wa_ref[d], preferred_element_type=F32) + ba_ref[d:d + 1, :] for j, d in units}
        la = {u: _log_sigmoid(x[u]) * (1.0 / GLA_TAU) for u in units}
        la1 = {u: la[u].astype(BF16) for u in units}
        rem = {u: la[u] - la1[u].astype(F32) for u in units}
        la2 = {u: rem[u].astype(BF16) for u in units}
        la3 = {u: (rem[u] - la2[u].astype(F32)).astype(BF16) for u in units}
        b = {(j, d): jnp.dot(tri[d], la1[j, d], preferred_element_type=F32)
             + jnp.dot(tri[d], la2[j, d], preferred_element_type=F32)
             + jnp.dot(tri[d], la3[j, d], preferred_element_type=F32) for j, d in units}
        b3 = {u: b[u].reshape(cpb, c, GLA_DK) for u in units}
        b_last = {(j, d): b3[j, d][:, c - 1:c, :] if d == 0 else b3[j, d][:, 0:1, :] for j, d in units}
        dec = {u: jnp.exp(b_last[u]) for u in units}
        dec_rows = {u: jnp.broadcast_to(dec[u], (cpb, c, GLA_DK)).reshape(blk_rows, GLA_DK) for u in units}
        q_dec = {(j, d): (q[j] * jnp.exp(b[j, d])).astype(BF16) for j, d in units}
        k_inv = {(j, d): k[j] * jnp.exp(-b[j, d]) for j, d in units}
        k_end = {u: (k_inv[u] * dec_rows[u]).astype(BF16) for u in units}
        a = {u: lax.dot_general(q_dec[u], k_inv[u].astype(BF16), _NT, preferred_element_type=F32) for u in units}
        a = {(j, d): jnp.where(masks[d], a[j, d], 0.0).astype(BF16) for j, d in units}
        o = {(j, d): jnp.dot(a[j, d], vb[j], preferred_element_type=F32) for j, d in units}
        for j in nb:
            vb_scr[rows[j], :] = vb[j]
        for j, d in units:
            o_scr[d, rows[j], :] = o[j, d]
            qd_scr[d, rows[j], :] = q_dec[j, d]
            ke_scr[d, rows[j], :] = k_end[j, d]
            dec8 = pl.ds(pl.multiple_of(blks[j] * (cpb * 8), cpb * 8), cpb * 8)
            dec_scr[d, dec8, :] = jnp.broadcast_to(dec[j, d], (cpb, 8, GLA_DK)).reshape(cpb * 8, GLA_DK)
        return carry

    lax.fori_loop(0, n_blocks // blocks_per_iter, bulk, 0)

    cpi = GLA_CHUNKS_PER_ITER

    def recur(i, carry):
        steps = [(u, d) for u in range(cpi) for d in range(2)]
        chunk = {(u, d): cpi * i + u if d == 0 else nc - 1 - (cpi * i + u) for u, d in steps}
        rows = {s: pl.ds(pl.multiple_of(chunk[s] * c, c), c) for s in steps}
        dec_rows = {s: pl.ds(pl.multiple_of(chunk[s] * 8, 8), 8) for s in steps}
        kv = {(u, d): lax.dot_general(ke_scr[d, rows[u, d], :], vb_scr[rows[u, d], :], _TN,
                                      preferred_element_type=F32) for u, d in steps}
        dec_col = {s: jnp.broadcast_to(dec_scr[s[1], dec_rows[s], :][0:1, :], (LANES, GLA_DK)).T for s in steps}
        dec_full = {s: jnp.concatenate([dec_col[s]] * (GLA_DV // LANES), axis=1) for s in steps}
        s_t = [st_scr[d] for d in range(2)]
        for u, d in steps:
            inter = jnp.dot(qd_scr[d, rows[u, d], :], s_t[d].astype(BF16), preferred_element_type=F32)
            o_scr[d, rows[u, d], :] += inter
            s_t[d] = s_t[d] * dec_full[u, d] + kv[u, d]
        for d in range(2):
            st_scr[d] = s_t[d]
        return carry

    lax.fori_loop(0, nc // cpi, recur, 0)

    def finish(i, carry):
        rows = pl.ds(pl.multiple_of(i * blk_rows, blk_rows), blk_rows)
        o = _rms(o_scr[0, rows, :] + o_scr[1, rows, :], gn_ref[...])
        o_ref[rows, :] = (o * _silu(r_ref[rows, :])).astype(BF16)
        return carry

    lax.fori_loop(0, seq // blk_rows, finish, 0)
    if emit_state:
        for d in range(2):
            s_out_refs[d][...] = st_scr[d]


def _gla(proj, small, wa, ba, gn, *, seq, n_batch, row0, s0=None, emit_state=False):
    rb = row0 // seq
    dk, dv = GLA_DK, GLA_DV
    in_specs = [pl.BlockSpec((seq, dk), lambda b, h: (rb + b, COL_Q // dk + h)),
                pl.BlockSpec((seq, dk), lambda b, h: (rb + b, COL_K // dk + h)),
                pl.BlockSpec((seq, dv), lambda b, h: (rb + b, COL_V // dv + h)),
                pl.BlockSpec((seq, dv), lambda b, h: (rb + b, COL_R // dv + h)),
                pl.BlockSpec((seq, LANES), lambda b, h: (rb + b, 1)),
                pl.BlockSpec((2, LANES, dk), lambda b, h: (0, 0, h)),
                pl.BlockSpec((2, dk), lambda b, h: (0, h)),
                pl.BlockSpec((1, dv), lambda b, h: (0, 0))]
    args = [proj, proj, proj, proj, small, wa, ba, gn]
    state_spec = pl.BlockSpec((None, None, dk, dv), lambda b, h: (b, h, 0, 0))
    if s0 is not None:
        in_specs += [state_spec, state_spec]
        args += list(s0)
    out_specs = [pl.BlockSpec((seq, dv), lambda b, h: (b, h))]
    out_shape = [jax.ShapeDtypeStruct((n_batch * seq, GLA_HEADS * dv), BF16)]
    if emit_state:
        out_specs += [state_spec, state_spec]
        out_shape += [jax.ShapeDtypeStruct((n_batch, GLA_HEADS, dk, dv), F32)] * 2
    return pl.pallas_call(
        functools.partial(_gla_kernel, seq=seq, has_s0=s0 is not None, emit_state=emit_state),
        grid=(n_batch, GLA_HEADS),
        in_specs=in_specs, out_specs=out_specs, out_shape=out_shape,
        scratch_shapes=[pltpu.VMEM((2, seq, dk), BF16),
                        pltpu.VMEM((2, seq, dk), BF16),
                        pltpu.VMEM((seq, dv), BF16),
                        pltpu.VMEM((2, seq // GLA_CHUNK * 8, dk), F32),
                        pltpu.VMEM((2, seq, dv), F32),
                        pltpu.VMEM((2, dk, dv), F32)],
        compiler_params=_cparams(2),
        name=f"gla_{seq}",
    )(*args)


def _rope_block(x, cs_ref):
    return x * cs_ref[:, :LANES] + pltpu.roll(x, MLA_ROPE, axis=1) * cs_ref[:, LANES:]


def _q_kernel(cq_ref, qn_ref, w_ref, cs_ref, o_ref):
    n = _rms(cq_ref[...], qn_ref[...]).astype(BF16)
    for h in range(MLA_HEADS):
        q = jnp.dot(n, w_ref[:, h * HEAD_W:(h + 1) * HEAD_W], preferred_element_type=F32)
        o_ref[:, h * HEAD_W:h * HEAD_W + LANES] = q[:, :LANES].astype(BF16)
        o_ref[:, h * HEAD_W + LANES:(h + 1) * HEAD_W] = _rope_block(q[:, LANES:], cs_ref).astype(BF16)


def _q_proj(proj, stream, q_norm, w_q, rope_tab):
    tm = MLA_TM
    return pl.pallas_call(
        _q_kernel,
        grid=(stream.n_tok // tm,),
        in_specs=[pl.BlockSpec((tm, Q_LORA), lambda i: (i, COL_CQ // Q_LORA)),
                  _resident((1, Q_LORA)),
                  _resident((Q_LORA, MLA_HEADS * HEAD_W)),
                  pl.BlockSpec((tm, 2 * LANES), lambda i: (stream.rope_row(i, tm), 0))],
        out_specs=pl.BlockSpec((tm, MLA_HEADS * HEAD_W), lambda i: (i, 0)),
        out_shape=jax.ShapeDtypeStruct((stream.n_tok, MLA_HEADS * HEAD_W), BF16),
        compiler_params=_cparams(1),
        name=f"mla_q_{stream.name}",
    )(proj, q_norm, w_q, rope_tab)


def _kv_kernel(*refs, norm, emit_ckv):
    ckv_ref, kn_ref, kr_ref, w_ref, cs_ref, k_ref, v_ref = refs[:7]
    c = ckv_ref[...]
    if norm:
        c = _rms(c, kn_ref[...])
    if emit_ckv:
        refs[7][...] = c
    cb = c.astype(BF16)
    nk = MLA_HEADS * MLA_NOPE
    kr = _rope_block(kr_ref[...], cs_ref).astype(BF16)
    v_ref[...] = jnp.dot(cb, w_ref[:, nk:], preferred_element_type=F32).astype(BF16)
    for h in range(MLA_HEADS):
        kn = jnp.dot(cb, w_ref[:, h * MLA_NOPE:(h + 1) * MLA_NOPE], preferred_element_type=F32)
        k_ref[:, h * HEAD_W:h * HEAD_W + LANES] = kn.astype(BF16)
        k_ref[:, h * HEAD_W + LANES:(h + 1) * HEAD_W] = kr


def _kv_proj(ckv_src, ckv_col, kr_src, kv_norm, w_kv, rope_tab, *, n_rows, row0, rope_row, norm, emit_ckv):
    tm = MLA_TM
    r0 = row0 // tm
    out_specs = [pl.BlockSpec((tm, MLA_HEADS * HEAD_W), lambda i: (i, 0)),
                 pl.BlockSpec((tm, MLA_HEADS * MLA_V), lambda i: (i, 0))]
    out_shape = [jax.ShapeDtypeStruct((n_rows, MLA_HEADS * HEAD_W), BF16),
                 jax.ShapeDtypeStruct((n_rows, MLA_HEADS * MLA_V), BF16)]
    if emit_ckv:
        out_specs.append(pl.BlockSpec((tm, KV_LORA), lambda i: (i, 0)))
        out_shape.append(jax.ShapeDtypeStruct((n_rows, KV_LORA), F32))
    return pl.pallas_call(
        functools.partial(_kv_kernel, norm=norm, emit_ckv=emit_ckv),
        grid=(n_rows // tm,),
        in_specs=[pl.BlockSpec((tm, KV_LORA), lambda i: (r0 + i, ckv_col)),
                  _resident((1, KV_LORA)),
                  pl.BlockSpec((tm, LANES), lambda i: (r0 + i, 0)),
                  _resident((KV_LORA, MLA_HEADS * (MLA_NOPE + MLA_V))),
                  pl.BlockSpec((tm, 2 * LANES), lambda i: (rope_row(i), 0))],
        out_specs=out_specs, out_shape=out_shape,
        compiler_params=_cparams(1),
        name=f"mla_kv_{row0}_{n_rows}",
    )(ckv_src, kv_norm, kr_src, w_kv, rope_tab)


_EXP2_SCALE = float((MLA_NOPE + MLA_ROPE) ** -0.5 * np.log2(np.e))


def _attn_kernel(*refs, n_seg, heads, tq, lookahead):
    q_ref = refs[0]
    kv = refs[1:1 + 2 * n_seg]
    o_ref = refs[1 + 2 * n_seg]
    nt = q_ref.shape[0] // tq
    work = [(h, t) for h in range(heads) for t in range(nt)]

    def scores(h, t):
        hs = slice(h * HEAD_W, (h + 1) * HEAD_W)
        q = q_ref[t * tq:(t + 1) * tq, hs]
        return [lax.dot_general(q, kv[2 * g][:, hs], _NT, preferred_element_type=F32) for g in range(n_seg)]

    def row_max(s):
        m = s[0].max(axis=-1, keepdims=True)
        for g in range(1, n_seg):
            m = jnp.maximum(m, s[g].max(axis=-1, keepdims=True))
        return m

    def probs(s, m):
        return [jnp.exp2((s[g] - m) * _EXP2_SCALE).astype(BF16) for g in range(n_seg)]

    def values(g, h):
        v = kv[2 * g + 1][:, h * MLA_V:(h + 1) * MLA_V]
        one_col = jnp.where(lax.broadcasted_iota(jnp.int32, v.shape, 1) == 0, 1.0, 0.0).astype(BF16)
        return jnp.concatenate([v, one_col], axis=1)

    v_aug = {(g, h): values(g, h) for g in range(n_seg) for h in range(heads)}

    def weighted(p, h):
        acc = 0.0
        for g in range(n_seg):
            acc = acc + jnp.dot(p[g], v_aug[g, h], preferred_element_type=F32)
        return acc

    def store(acc, h, t):
        out = acc[:, :MLA_V] / acc[:, MLA_V:MLA_V + 1]
        o_ref[t * tq:(t + 1) * tq, h * MLA_V:(h + 1) * MLA_V] = out.astype(BF16)

    if lookahead >= len(work):
        s = [scores(*w) for w in work]
        m = [row_max(x) for x in s]
        p = [probs(x, y) for x, y in zip(s, m)]
        acc = [weighted(x, w[0]) for x, w in zip(p, work)]
        for x, w in zip(acc, work):
            store(x, *w)
    else:
        s_next = scores(*work[0])
        for idx, (h, t) in enumerate(work):
            s = s_next
            if idx + 1 < len(work):
                s_next = scores(*work[idx + 1])
            store(weighted(probs(s, row_max(s)), h), h, t)


def _attention(q, q_row0, segs, *, n_batch, seq, tq, heads):
    qb0 = q_row0 // seq
    n_hblk = MLA_HEADS // heads
    in_specs = [pl.BlockSpec((seq, heads * HEAD_W), lambda b, h: (qb0 + b, h))]
    args = [q]
    for keys, vals, n_keys in segs:
        in_specs += [pl.BlockSpec((n_keys, heads * HEAD_W), lambda b, h: (b, h)),
                     pl.BlockSpec((n_keys, heads * MLA_V), lambda b, h: (b, h))]
        args += [keys, vals]
    return pl.pallas_call(
        functools.partial(_attn_kernel, n_seg=len(segs), heads=heads, tq=tq,
                          lookahead=1 if seq > tq else heads),
        grid=(n_batch, n_hblk),
        in_specs=in_specs,
        out_specs=pl.BlockSpec((seq, heads * MLA_V), lambda b, h: (b, h)),
        out_shape=jax.ShapeDtypeStruct((n_batch * seq, MLA_HEADS * MLA_V), BF16),
        compiler_params=_cparams(2),
        name=f"mla_attn_{seq}",
    )(*args)


def _merge_kernel(og_ref, om_ref, ga_ref, gb_ref, x_ref, mod_ref, g_ref, wg_ref, wm_ref, wo_ref, o_ref):
    yg = jnp.dot(og_ref[...], wg_ref[...], preferred_element_type=F32)
    ym = jnp.dot(om_ref[...], wm_ref[...], preferred_element_type=F32)
    m = (jax.nn.sigmoid(ga_ref[...]) * yg + jax.nn.sigmoid(gb_ref[...]) * ym).astype(BF16)
    y = _rms(jnp.dot(m, wo_ref[...], preferred_element_type=F32), g_ref[3:4, :])
    o_ref[...] = x_ref[...] + mod_ref[5:6, :] * y


def _merge(o_gla, o_mla, proj, x, stream, mod, gains, w_gla_out, w_mla_out, w_out):
    tm = MERGE_TM
    tile = lambda c0: pl.BlockSpec((tm, D_MODEL), lambda i: (i, c0 // D_MODEL))
    return pl.pallas_call(
        _merge_kernel,
        grid=(stream.n_tok // tm,),
        in_specs=[tile(0), tile(0), tile(COL_GA), tile(COL_GB), tile(0),
                  pl.BlockSpec((None, N_MOD, D_MODEL), lambda i: (stream.mod_row(i, tm), 0, 0)),
                  _resident((6, D_MODEL)),
                  _resident((D_MODEL, D_MODEL)), _resident((D_MODEL, D_MODEL)), _resident((D_MODEL, D_MODEL))],
        out_specs=tile(0),
        out_shape=jax.ShapeDtypeStruct((stream.n_tok, D_MODEL), F32),
        compiler_params=_cparams(1),
        name=f"mixer_merge_{stream.name}",
    )(o_gla, o_mla, proj, proj, x, mod, gains, w_gla_out, w_mla_out, w_out)


_SWAP = np.arange(MLA_ROPE)
_SWAP = np.where(_SWAP % 32 < 16, _SWAP + 16, _SWAP - 16)


_FF_BLOCKS = D_FF // LANES
_U_PER_STEP = FF_TILE // LANES


def _prep_ffn_in_kernel(g_ref, *refs):
    u_refs, o_ref = refs[:-1], refs[-1]
    j = pl.program_id(0)
    col = j * FF_TILE + lax.broadcasted_iota(jnp.int32, (1, FF_TILE), 1)
    o_ref[:, :FF_TILE] = jnp.where(col < D_FF, g_ref[...], 0.0).astype(BF16)
    lane = lax.broadcasted_iota(jnp.int32, (1, LANES), 1)
    for q, u_ref in enumerate(u_refs):
        ucol = j * FF_TILE + q * LANES + lane
        o_ref[:, FF_TILE + q * LANES:FF_TILE + (q + 1) * LANES] = jnp.where(ucol < D_FF, u_ref[...], 0.0).astype(BF16)


def _prep_ffn_in(w, name):
    last = 2 * _FF_BLOCKS - 1
    u_spec = lambda q: pl.BlockSpec((D_MODEL, LANES),
                                    lambda j: (0, jnp.minimum(_FF_BLOCKS + _U_PER_STEP * j + q, last)))
    return pl.pallas_call(
        _prep_ffn_in_kernel,
        grid=(N_FF,),
        in_specs=[pl.BlockSpec((D_MODEL, FF_TILE), lambda j: (0, j))] + [u_spec(q) for q in range(_U_PER_STEP)],
        out_specs=pl.BlockSpec((D_MODEL, 2 * FF_TILE), lambda j: (0, j)),
        out_shape=jax.ShapeDtypeStruct((D_MODEL, N_FF * 2 * FF_TILE), BF16),
        compiler_params=_cparams(1),
        name=name,
    )(w, *([w] * _U_PER_STEP))


_FF_FULL = D_FF // FF_TILE
_FF_TAIL = (D_FF - _FF_FULL * FF_TILE) // LANES


def _prep_ffn_out_kernel(w_ref, *refs):
    tail_refs, o_ref = refs[:-1], refs[-1]
    j = pl.program_id(0)

    @pl.when(j < _FF_FULL)
    def _():
        o_ref[...] = w_ref[...].astype(BF16)

    @pl.when(j == _FF_FULL)
    def _():
        for t, t_ref in enumerate(tail_refs):
            o_ref[t * LANES:(t + 1) * LANES, :] = t_ref[...].astype(BF16)
        o_ref[_FF_TAIL * LANES:, :] = jnp.zeros((FF_TILE - _FF_TAIL * LANES, D_MODEL), BF16)


def _prep_ffn_out(w, name):
    tail = lambda t: pl.BlockSpec((LANES, D_MODEL), lambda j: (_FF_FULL * (FF_TILE // LANES) + t, 0))
    return pl.pallas_call(
        _prep_ffn_out_kernel,
        grid=(N_FF,),
        in_specs=[pl.BlockSpec((FF_TILE, D_MODEL), lambda j: (jnp.minimum(j, _FF_FULL - 1), 0))]
        + [tail(t) for t in range(_FF_TAIL)],
        out_specs=pl.BlockSpec((FF_TILE, D_MODEL), lambda j: (j, 0)),
        out_shape=jax.ShapeDtypeStruct((FF_PAD, D_MODEL), BF16),
        compiler_params=_cparams(1),
        name=name,
    )(w, *([w] * _FF_TAIL))


_W_IN_GROUPS = ((0, 6144), (7264, 2048), (9312, 2048), (6176, 512), (6688, 512))


def _prep_w_in(w):
    wt = jnp.swapaxes(w, 0, 1)
    main = jnp.concatenate([wt[s:s + n] for s, n in _W_IN_GROUPS], axis=0).astype(BF16)
    af, ab, kr = wt[6144:6160], wt[6160:6176], wt[7200:7264]
    small = jnp.concatenate([kr, kr[_SWAP], af, ab,
                             jnp.zeros((PROJ_SMALL - 2 * MLA_ROPE - 2 * GLA_RANK, D_MODEL), F32)], axis=0).astype(BF16)
    return main, small


def _prep_w_uq(w):
    w = w.astype(BF16).reshape(Q_LORA, MLA_HEADS, MLA_NOPE + MLA_ROPE)
    rope = w[..., MLA_NOPE:]
    return jnp.concatenate([w[..., :MLA_NOPE], rope, rope[..., _SWAP]], axis=-1).reshape(Q_LORA, MLA_HEADS * HEAD_W)


def _prep_w_ukv(w):
    w = w.astype(BF16).reshape(KV_LORA, MLA_HEADS, MLA_NOPE + MLA_V)
    return jnp.concatenate([w[..., :MLA_NOPE].reshape(KV_LORA, -1), w[..., MLA_NOPE:].reshape(KV_LORA, -1)], axis=1)


def _prep_w_alpha(w):
    out = jnp.zeros((2, LANES, GLA_HEADS * GLA_DK), BF16)
    out = out.at[0, 0:GLA_RANK].set(w[0].astype(BF16))
    return out.at[1, GLA_RANK:2 * GLA_RANK].set(w[1].astype(BF16))


def _rope_table(tm):
    rows = DEC_SEQ // GRID_W
    row = jnp.repeat(jnp.arange(rows), GRID_W).astype(F32)
    col = jnp.tile(jnp.arange(GRID_W), rows).astype(F32)
    half = MLA_ROPE // 2
    inv_freq = ROPE_THETA ** (-jnp.arange(0, half, 2, dtype=F32) / half)
    ang_r, ang_c = row[:, None] * inv_freq, col[:, None] * inv_freq
    cos = jnp.concatenate([jnp.cos(ang_r), jnp.cos(ang_r), jnp.cos(ang_c), jnp.cos(ang_c)], axis=1)
    sin = jnp.concatenate([-jnp.sin(ang_r), jnp.sin(ang_r), -jnp.sin(ang_c), jnp.sin(ang_c)], axis=1)
    z = jnp.zeros((DEC_SEQ, LANES - MLA_ROPE), F32)
    lat = jnp.concatenate([cos, z, sin, z], axis=1)
    ident = jnp.concatenate([jnp.ones((tm, MLA_ROPE), F32), jnp.zeros((tm, 2 * LANES - MLA_ROPE), F32)], axis=1)
    return jnp.concatenate([ident, lat], axis=0)


def kernel(x_prompt, x_sample, cache_ckv, cache_krope, state_gla_fwd, state_gla_bwd, c, c_ctx, w_ada, b_ada, norm_gains, w_ffn1_in, w_ffn1_out, w_ffn2_in, w_ffn2_out, w_in, w_gla_alpha, b_gla_alpha, gla_norm, w_gla_out, q_norm, kv_norm, w_uq, w_ukv, w_mla_out, w_out):
    c_all = jnp.concatenate([c_ctx[None, :], c, jnp.zeros((8 - 1 - DEC_BATCH, D_MODEL), F32)], axis=0)
    mod = _modulation(c_all, w_ada[0], b_ada[0]).reshape(8, N_MOD, D_MODEL)
    gains = norm_gains[0]

    streams = (_CTX, _LAT)
    x = {_CTX: x_prompt.reshape(N_CTX, D_MODEL), _LAT: x_sample.reshape(N_LAT, D_MODEL)}

    w_gu, w_o = _prep_ffn_in(w_ffn1_in[0], "prep_ffn0_in"), _prep_ffn_out(w_ffn1_out[0], "prep_ffn0_out")
    x = {s: _ffn(x[s], s, mod, gains, w_gu, w_o, sub=0) for s in streams}

    w_main, w_small = _prep_w_in(w_in[0])
    proj, small = {}, {}
    for s in streams:
        proj[s], small[s] = _mixer_proj(x[s], s, mod, gains, w_main, w_small)

    wa = _prep_w_alpha(w_gla_alpha[0])
    ba = b_gla_alpha[0]
    gn = gla_norm[0].reshape(1, GLA_DV)
    o_gla = {}
    o_gla[_CTX], s_f, s_b = _gla(proj[_CTX], small[_CTX], wa, ba, gn, seq=SEQ, n_batch=BATCH, row0=0,
                                 emit_state=True)
    (o_gla[_LAT],) = _gla(proj[_LAT], small[_LAT], wa, ba, gn, seq=DEC_SEQ, n_batch=DEC_BATCH, row0=0,
                          s0=(state_gla_fwd[:, 0], state_gla_bwd[:, 0]))

    tm = MLA_TM
    rope_tab = _rope_table(tm)
    w_q = _prep_w_uq(w_uq[0])
    q = {s: _q_proj(proj[s], s, q_norm[0].reshape(1, Q_LORA), w_q, rope_tab) for s in streams}
    w_kv = _prep_w_ukv(w_ukv[0])
    kvn = kv_norm[0].reshape(1, KV_LORA)
    ckv_col = COL_CKV // KV_LORA
    k_ctx, v_ctx, ckv_new = _kv_proj(proj[_CTX], ckv_col, small[_CTX], kvn, w_kv, rope_tab, n_rows=N_CTX, row0=0,
                                     rope_row=lambda i: _CTX.rope_row(i, tm), norm=True, emit_ckv=True)
    k_lat, v_lat = _kv_proj(proj[_LAT], ckv_col, small[_LAT], kvn, w_kv, rope_tab, n_rows=N_LAT, row0=0,
                            rope_row=lambda i: _LAT.rope_row(i, tm), norm=True, emit_ckv=False)
    n_past = DEC_BATCH * PAST_LEN
    kr_past = jnp.pad(cache_krope[:, 0].reshape(n_past, MLA_ROPE), ((0, 0), (0, LANES - MLA_ROPE)))
    k_past, v_past = _kv_proj(cache_ckv[:, 0].reshape(n_past, KV_LORA), 0, kr_past, kvn, w_kv, rope_tab,
                              n_rows=n_past, row0=0, rope_row=lambda i: 0 * i, norm=False, emit_ckv=False)
    o_mla = {
        _CTX: _attention(q[_CTX], 0, [(k_ctx, v_ctx, SEQ)], n_batch=BATCH, seq=SEQ, tq=SEQ, heads=MLA_HEADS),
        _LAT: _attention(q[_LAT], 0, [(k_past, v_past, PAST_LEN), (k_lat, v_lat, DEC_SEQ)],
                         n_batch=DEC_BATCH, seq=DEC_SEQ, tq=ATT_TQ, heads=1)}

    w_merge = (w_gla_out[0].astype(BF16), w_mla_out[0].astype(BF16), w_out[0].astype(BF16))
    x = {s: _merge(o_gla[s], o_mla[s], proj[s], x[s], s, mod, gains, *w_merge) for s in streams}
    w_gu, w_o = _prep_ffn_in(w_ffn2_in[0], "prep_ffn2_in"), _prep_ffn_out(w_ffn2_out[0], "prep_ffn2_out")
    y = {s: _ffn(x[s], s, mod, gains, w_gu, w_o, sub=2) for s in streams}

    y_prompt = y[_CTX].reshape(BATCH, SEQ, D_MODEL)
    y_sample = y[_LAT].reshape(DEC_BATCH, DEC_SEQ, D_MODEL)
    new_ckv = ckv_new.reshape(BATCH, 1, SEQ, KV_LORA)
    new_krope = small[_CTX][:, :MLA_ROPE].reshape(BATCH, 1, SEQ, MLA_ROPE)
    return (y_prompt, y_sample, new_ckv, new_krope,
            s_f.reshape(BATCH, 1, GLA_HEADS, GLA_DK, GLA_DV), s_b.reshape(BATCH, 1, GLA_HEADS, GLA_DK, GLA_DV))
```

```python
import functools
from typing import NamedTuple

import numpy as np
import jax
import jax.numpy as jnp
from jax import lax
from jax.experimental import pallas as pl
from jax.experimental.pallas import tpu as pltpu

F32 = jnp.float32
BF16 = jnp.bfloat16

D_MODEL = 2048
BATCH, SEQ = 16, 256
DEC_BATCH, DEC_SEQ = 4, 2048
PAST_LEN = 256
GRID_W = 64
D_FF = 5504
N_MOD = 9
GLA_HEADS, GLA_DK, GLA_DV = 4, 256, 512
GLA_RANK = 16
GLA_TAU = 16.0
GLA_CHUNK = 64
GLA_BLOCK = 256
GLA_CHUNKS_PER_ITER = 4
MLA_HEADS, MLA_NOPE, MLA_ROPE, MLA_V = 16, 128, 64, 128
Q_LORA = KV_LORA = 512
ROPE_THETA = 10000.0
NORM_EPS = 1e-6

N_CTX = BATCH * SEQ
N_LAT = DEC_BATCH * DEC_SEQ
N_TOK = N_CTX + N_LAT

LANES = 128
V7X_VMEM_BYTES = 64 * 1024 * 1024
V7X_VMEM_LIMIT_BYTES = 60000 * 1024

FF_TILE = 512
FF_PAD = -(-D_FF // FF_TILE) * FF_TILE
N_FF = FF_PAD // FF_TILE
FFN_TM = 1024
FFN_ROWS = 512
PROJ_TM, PROJ_TN = 1024, 1024
MLA_TM = 512
MERGE_TM = 256
ATT_TQ = 512
HEAD_W = 2 * LANES

COL_Q, COL_K, COL_V, COL_R = 0, 1024, 2048, 4096
COL_GA, COL_GB, COL_CQ, COL_CKV = 6144, 8192, 10240, 10752
PROJ_MAIN = 11264
PROJ_SMALL = 256


class _Stream(NamedTuple):
    name: str
    n_batch: int
    seq: int
    mod_base: int
    per_batch_mod: bool
    rope: bool

    @property
    def n_tok(self):
        return self.n_batch * self.seq

    def mod_row(self, tile, tm):
        return self.mod_base + (tile * tm // self.seq if self.per_batch_mod else 0 * tile)

    def rope_row(self, tile, tm):
        return 1 + tile % (self.seq // tm) if self.rope else 0 * tile


_CTX = _Stream("ctx", BATCH, SEQ, 0, False, False)
_LAT = _Stream("lat", DEC_BATCH, DEC_SEQ, 1, True, True)


def _rms(x, gain):
    ms = jnp.mean(x * x, axis=-1, keepdims=True)
    return x * lax.rsqrt(ms + NORM_EPS) * gain


def _silu(x):
    return x * jax.nn.sigmoid(x)


_NT = (((1,), (1,)), ((), ()))
_TN = (((0,), (0,)), ((), ()))


def _cparams(n_axes, vmem_limit_bytes=V7X_VMEM_LIMIT_BYTES):
    return pltpu.CompilerParams(dimension_semantics=("arbitrary",) * n_axes, vmem_limit_bytes=vmem_limit_bytes)


def _resident(shape):
    nd = len(shape)
    return pl.BlockSpec(shape, lambda *_: (0,) * nd, pipeline_mode=pl.Buffered(1))


def _mod_kernel(c_ref, w_ref, b_ref, o_ref):
    s = _silu(c_ref[...]).astype(BF16)
    o_ref[...] = jnp.dot(s, w_ref[...].astype(BF16), preferred_element_type=F32) + b_ref[...]


def _modulation(c_all, w_ada, b_ada):
    n = w_ada.shape[1]
    tn = 1024
    return pl.pallas_call(
        _mod_kernel,
        grid=(n // tn,),
        in_specs=[pl.BlockSpec((8, D_MODEL), lambda j: (0, 0)),
                  pl.BlockSpec((D_MODEL, tn), lambda j: (0, j)),
                  pl.BlockSpec((1, tn), lambda j: (0, j))],
        out_specs=pl.BlockSpec((8, tn), lambda j: (0, j)),
        out_shape=jax.ShapeDtypeStruct((8, n), F32),
        compiler_params=_cparams(1),
        name="adaln_mod",
    )(c_all, w_ada, b_ada.reshape(1, n))


def _row(ref, r):
    return ref[r:r + 1, :]


def _norm_mod(x_ref, h_ref, g_ref, mod_ref, sub):
    h = _rms(x_ref[...], _row(g_ref, 2 * sub)) * (1.0 + _row(mod_ref, 3 * sub + 1)) + _row(mod_ref, 3 * sub)
    h_ref[...] = h.astype(BF16)


def _residual(x_ref, o_ref, g_ref, mod_ref, sub, gate_scale):
    y = _rms(o_ref[...], _row(g_ref, 2 * sub + 1))
    o_ref[...] = x_ref[...] + (gate_scale * _row(mod_ref, 3 * sub + 2)) * y


def _ffn_kernel(x_ref, mod_ref, g_ref, wgu_ref, wo_ref, o_ref, h_ref, *, sub):
    j = pl.program_id(1)

    @pl.when(j == 0)
    def _():
        _norm_mod(x_ref, h_ref, g_ref, mod_ref, sub)
        o_ref[...] = jnp.zeros_like(o_ref)

    for r in range(0, h_ref.shape[0], FFN_ROWS):
        rows = slice(r, r + FFN_ROWS)
        gu = jnp.dot(h_ref[rows, :], wgu_ref[...], preferred_element_type=F32)
        a = (_silu(gu[:, :FF_TILE]) * gu[:, FF_TILE:]).astype(BF16)
        o_ref[rows, :] += jnp.dot(a, wo_ref[...], preferred_element_type=F32)

    @pl.when(j == pl.num_programs(1) - 1)
    def _():
        _residual(x_ref, o_ref, g_ref, mod_ref, sub, 0.5)


def _ffn(x, stream, mod, gains, w_gu, w_o, sub):
    tm = FFN_TM
    return pl.pallas_call(
        functools.partial(_ffn_kernel, sub=sub),
        grid=(stream.n_tok // tm, N_FF),
        in_specs=[pl.BlockSpec((tm, D_MODEL), lambda i, j: (i, 0)),
                  pl.BlockSpec((None, N_MOD, D_MODEL), lambda i, j: (stream.mod_row(i, tm), 0, 0)),
                  pl.BlockSpec((6, D_MODEL), lambda i, j: (0, 0)),
                  pl.BlockSpec((D_MODEL, 2 * FF_TILE), lambda i, j: (0, j)),
                  pl.BlockSpec((FF_TILE, D_MODEL), lambda i, j: (j, 0))],
        out_specs=pl.BlockSpec((tm, D_MODEL), lambda i, j: (i, 0)),
        out_shape=jax.ShapeDtypeStruct((stream.n_tok, D_MODEL), F32),
        scratch_shapes=[pltpu.VMEM((tm, D_MODEL), BF16)],
        compiler_params=_cparams(2, V7X_VMEM_BYTES - 2 * 1024 * 1024),
        name=f"ffn{sub}_{stream.name}",
    )(x, mod, gains, w_gu, w_o)


def _proj_kernel(x_ref, mod_ref, g_ref, wt_ref, wst_ref, o_ref, os_ref, h_ref):
    j = pl.program_id(1)

    @pl.when(j == 0)
    def _():
        _norm_mod(x_ref, h_ref, g_ref, mod_ref, 1)
        os_ref[...] = lax.dot_general(h_ref[...], wst_ref[...], _NT, preferred_element_type=F32)

    o_ref[...] = lax.dot_general(h_ref[...], wt_ref[...], _NT, preferred_element_type=F32)


def _mixer_proj(x, stream, mod, gains, w_main, w_small):
    tm, tn = PROJ_TM, PROJ_TN
    n_tok = stream.n_tok
    return pl.pallas_call(
        _proj_kernel,
        grid=(n_tok // tm, PROJ_MAIN // tn),
        in_specs=[pl.BlockSpec((tm, D_MODEL), lambda i, j: (i, 0)),
                  pl.BlockSpec((None, N_MOD, D_MODEL), lambda i, j: (stream.mod_row(i, tm), 0, 0)),
                  pl.BlockSpec((6, D_MODEL), lambda i, j: (0, 0)),
                  pl.BlockSpec((tn, D_MODEL), lambda i, j: (j, 0)),
                  pl.BlockSpec((PROJ_SMALL, D_MODEL), lambda i, j: (0, 0))],
        out_specs=[pl.BlockSpec((tm, tn), lambda i, j: (i, j)),
                   pl.BlockSpec((tm, PROJ_SMALL), lambda i, j: (i, 0))],
        out_shape=[jax.ShapeDtypeStruct((n_tok, PROJ_MAIN), F32),
                   jax.ShapeDtypeStruct((n_tok, PROJ_SMALL), F32)],
        scratch_shapes=[pltpu.VMEM((tm, D_MODEL), BF16)],
        compiler_params=_cparams(2),
        name=f"mixer_proj_{stream.name}",
    )(x, mod, gains, w_main, w_small)


def _log_sigmoid(x):
    return jnp.minimum(x, 0.0) - jnp.log1p(jnp.exp(-jnp.abs(x)))


def _gla_kernel(*refs, seq, has_s0, emit_state):
    q_ref, k_ref, v_ref, r_ref, a_ref, wa_ref, ba_ref, gn_ref = refs[:8]
    pos = 8
    if has_s0:
        s0_refs = refs[pos:pos + 2]
        pos += 2
    o_ref = refs[pos]
    pos += 1
    if emit_state:
        s_out_refs = refs[pos:pos + 2]
        pos += 2
    qd_scr, ke_scr, vb_scr, dec_scr, o_scr, st_scr = refs[pos:pos + 6]

    c, blk_rows = GLA_CHUNK, GLA_BLOCK
    cpb = blk_rows // c
    nc = seq // c
    for d in range(2):
        st_scr[d] = s0_refs[d][...] if has_s0 else jnp.zeros((GLA_DK, GLA_DV), F32)

    row = lax.broadcasted_iota(jnp.int32, (blk_rows, blk_rows), 0)
    col = lax.broadcasted_iota(jnp.int32, (blk_rows, blk_rows), 1)
    same_chunk = (row // c) == (col // c)
    masks = (same_chunk & (col <= row), same_chunk & (col >= row))

    tri = [jnp.where(masks[d], 1.0, 0.0).astype(BF16) for d in range(2)]
    n_blocks = seq // blk_rows
    blocks_per_iter = 2 if n_blocks % 2 == 0 else 1

    def bulk(it, carry):
        blks = [it * blocks_per_iter + j for j in range(blocks_per_iter)]
        rows = [pl.ds(pl.multiple_of(blk * blk_rows, blk_rows), blk_rows) for blk in blks]
        nb = range(blocks_per_iter)
        a_in = [a_ref[rows[j], :].astype(BF16) for j in nb]
        q = [q_ref[rows[j], :] * (GLA_DK ** -0.5) for j in nb]
        k = [k_ref[rows[j], :] for j in nb]
        vb = [v_ref[rows[j], :].astype(BF16) for j in nb]
        units = [(j, d) for j in nb for d in range(2)]
        x = {(j, d): jnp.dot(a_in[j], wa_ref[d], preferred_element_type=F32) + ba_ref[d:d + 1, :] for j, d in units}
        la = {u: _log_sigmoid(x[u]) * (1.0 / GLA_TAU) for u in units}
        la1 = {u: la[u].astype(BF16) for u in units}
        rem = {u: la[u] - la1[u].astype(F32) for u in units}
        la2 = {u: rem[u].astype(BF16) for u in units}
        la3 = {u: (rem[u] - la2[u].astype(F32)).astype(BF16) for u in units}
        b = {(j, d): jnp.dot(tri[d], la1[j, d], preferred_element_type=F32)
             + jnp.dot(tri[d], la2[j, d], preferred_element_type=F32)
             + jnp.dot(tri[d], la3[j, d], preferred_element_type=F32) for j, d in units}
        b3 = {u: b[u].reshape(cpb, c, GLA_DK) for u in units}
        b_last = {(j, d): b3[j, d][:, c - 1:c, :] if d == 0 else b3[j, d][:, 0:1, :] for j, d in units}
        dec = {u: jnp.exp(b_last[u]) for u in units}
        dec_rows = {u: jnp.broadcast_to(dec[u], (cpb, c, GLA_DK)).reshape(blk_rows, GLA_DK) for u in units}
        q_dec = {(j, d): (q[j] * jnp.exp(b[j, d])).astype(BF16) for j, d in units}
        k_inv = {(j, d): k[j] * jnp.exp(-b[j, d]) for j, d in units}
        k_end = {u: (k_inv[u] * dec_rows[u]).astype(BF16) for u in units}
        a = {u: lax.dot_general(q_dec[u], k_inv[u].astype(BF16), _NT, preferred_element_type=F32) for u in units}
        a = {(j, d): jnp.where(masks[d], a[j, d], 0.0).astype(BF16) for j, d in units}
        o = {(j, d): jnp.dot(a[j, d], vb[j], preferred_element_type=F32) for j, d in units}
        for j in nb:
            vb_scr[rows[j], :] = vb[j]
        for j, d in units:
            o_scr[d, rows[j], :] = o[j, d]
            qd_scr[d, rows[j], :] = q_dec[j, d]
            ke_scr[d, rows[j], :] = k_end[j, d]
            dec8 = pl.ds(pl.multiple_of(blks[j] * (cpb * 8), cpb * 8), cpb * 8)
            dec_scr[d, dec8, :] = jnp.broadcast_to(dec[j, d], (cpb, 8, GLA_DK)).reshape(cpb * 8, GLA_DK)
        return carry

    lax.fori_loop(0, n_blocks // blocks_per_iter, bulk, 0)

    cpi = GLA_CHUNKS_PER_ITER

    def recur(i, carry):
        steps = [(u, d) for u in range(cpi) for d in range(2)]
        chunk = {(u, d): cpi * i + u if d == 0 else nc - 1 - (cpi * i + u) for u, d in steps}
        rows = {s: pl.ds(pl.multiple_of(chunk[s] * c, c), c) for s in steps}
        dec_rows = {s: pl.ds(pl.multiple_of(chunk[s] * 8, 8), 8) for s in steps}
        kv = {(u, d): lax.dot_general(ke_scr[d, rows[u, d], :], vb_scr[rows[u, d], :], _TN,
                                      preferred_element_type=F32) for u, d in steps}
        dec_col = {s: jnp.broadcast_to(dec_scr[s[1], dec_rows[s], :][0:1, :], (LANES, GLA_DK)).T for s in steps}
        dec_full = {s: jnp.concatenate([dec_col[s]] * (GLA_DV // LANES), axis=1) for s in steps}
        s_t = [st_scr[d] for d in range(2)]
        for u, d in steps:
            inter = jnp.dot(qd_scr[d, rows[u, d], :], s_t[d].astype(BF16), preferred_element_type=F32)
            o_scr[d, rows[u, d], :] += inter
            s_t[d] = s_t[d] * dec_full[u, d] + kv[u, d]
        for d in range(2):
            st_scr[d] = s_t[d]
        return carry

    lax.fori_loop(0, nc // cpi, recur, 0)

    def finish(i, carry):
        rows = pl.ds(pl.multiple_of(i * blk_rows, blk_rows), blk_rows)
        o = _rms(o_scr[0, rows, :] + o_scr[1, rows, :], gn_ref[...])
        o_ref[rows, :] = (o * _silu(r_ref[rows, :])).astype(BF16)
        return carry

    lax.fori_loop(0, seq // blk_rows, finish, 0)
    if emit_state:
        for d in range(2):
            s_out_refs[d][...] = st_scr[d]


def _gla(proj, small, wa, ba, gn, *, seq, n_batch, row0, s0=None, emit_state=False):
    rb = row0 // seq
    dk, dv = GLA_DK, GLA_DV
    in_specs = [pl.BlockSpec((seq, dk), lambda b, h: (rb + b, COL_Q // dk + h)),
                pl.BlockSpec((seq, dk), lambda b, h: (rb + b, COL_K // dk + h)),
                pl.BlockSpec((seq, dv), lambda b, h: (rb + b, COL_V // dv + h)),
                pl.BlockSpec((seq, dv), lambda b, h: (rb + b, COL_R // dv + h)),
                pl.BlockSpec((seq, LANES), lambda b, h: (rb + b, 1)),
                pl.BlockSpec((2, LANES, dk), lambda b, h: (0, 0, h)),
                pl.BlockSpec((2, dk), lambda b, h: (0, h)),
                pl.BlockSpec((1, dv), lambda b, h: (0, 0))]
    args = [proj, proj, proj, proj, small, wa, ba, gn]
    state_spec = pl.BlockSpec((None, None, dk, dv), lambda b, h: (b, h, 0, 0))
    if s0 is not None:
        in_specs += [state_spec, state_spec]
        args += list(s0)
    out_specs = [pl.BlockSpec((seq, dv), lambda b, h: (b, h))]
    out_shape = [jax.ShapeDtypeStruct((n_batch * seq, GLA_HEADS * dv), BF16)]
    if emit_state:
        out_specs += [state_spec, state_spec]
        out_shape += [jax.ShapeDtypeStruct((n_batch, GLA_HEADS, dk, dv), F32)] * 2
    return pl.pallas_call(
        functools.partial(_gla_kernel, seq=seq, has_s0=s0 is not None, emit_state=emit_state),
        grid=(n_batch, GLA_HEADS),
        in_specs=in_specs, out_specs=out_specs, out_shape=out_shape,
        scratch_shapes=[pltpu.VMEM((2, seq, dk), BF16),
                        pltpu.VMEM((2, seq, dk), BF16),
                        pltpu.VMEM((seq, dv), BF16),
                        pltpu.VMEM((2, seq // GLA_CHUNK * 8, dk), F32),
                        pltpu.VMEM((2, seq, dv), F32),
                        pltpu.VMEM((2, dk, dv), F32)],
        compiler_params=_cparams(2),
        name=f"gla_{seq}",
    )(*args)


def _rope_block(x, cs_ref):
    return x * cs_ref[:, :LANES] + pltpu.roll(x, MLA_ROPE, axis=1) * cs_ref[:, LANES:]


def _q_kernel(cq_ref, qn_ref, w_ref, cs_ref, o_ref):
    n = _rms(cq_ref[...], qn_ref[...]).astype(BF16)
    for h in range(MLA_HEADS):
        q = jnp.dot(n, w_ref[:, h * HEAD_W:(h + 1) * HEAD_W], preferred_element_type=F32)
        o_ref[:, h * HEAD_W:h * HEAD_W + LANES] = q[:, :LANES].astype(BF16)
        o_ref[:, h * HEAD_W + LANES:(h + 1) * HEAD_W] = _rope_block(q[:, LANES:], cs_ref).astype(BF16)


def _q_proj(proj, stream, q_norm, w_q, rope_tab):
    tm = MLA_TM
    return pl.pallas_call(
        _q_kernel,
        grid=(stream.n_tok // tm,),
        in_specs=[pl.BlockSpec((tm, Q_LORA), lambda i: (i, COL_CQ // Q_LORA)),
                  _resident((1, Q_LORA)),
                  _resident((Q_LORA, MLA_HEADS * HEAD_W)),
                  pl.BlockSpec((tm, 2 * LANES), lambda i: (stream.rope_row(i, tm), 0))],
        out_specs=pl.BlockSpec((tm, MLA_HEADS * HEAD_W), lambda i: (i, 0)),
        out_shape=jax.ShapeDtypeStruct((stream.n_tok, MLA_HEADS * HEAD_W), BF16),
        compiler_params=_cparams(1),
        name=f"mla_q_{stream.name}",
    )(proj, q_norm, w_q, rope_tab)


def _kv_kernel(*refs, norm, emit_ckv):
    ckv_ref, kn_ref, kr_ref, w_ref, cs_ref, k_ref, v_ref = refs[:7]
    c = ckv_ref[...]
    if norm:
        c = _rms(c, kn_ref[...])
    if emit_ckv:
        refs[7][...] = c
    cb = c.astype(BF16)
    nk = MLA_HEADS * MLA_NOPE
    kr = _rope_block(kr_ref[...], cs_ref).astype(BF16)
    v_ref[...] = jnp.dot(cb, w_ref[:, nk:], preferred_element_type=F32).astype(BF16)
    for h0 in range(0, MLA_HEADS, 2):
        kn = jnp.dot(cb, w_ref[:, h0 * MLA_NOPE:(h0 + 2) * MLA_NOPE], preferred_element_type=F32).astype(BF16)
        for h in (h0, h0 + 1):
            k_ref[:, h * HEAD_W:h * HEAD_W + LANES] = kn[:, (h - h0) * MLA_NOPE:(h - h0 + 1) * MLA_NOPE]
            k_ref[:, h * HEAD_W + LANES:(h + 1) * HEAD_W] = kr


def _kv_proj(ckv_src, ckv_col, kr_src, kv_norm, w_kv, rope_tab, *, n_rows, row0, rope_row, norm, emit_ckv):
    tm = MLA_TM
    r0 = row0 // tm
    out_specs = [pl.BlockSpec((tm, MLA_HEADS * HEAD_W), lambda i: (i, 0)),
                 pl.BlockSpec((tm, MLA_HEADS * MLA_V), lambda i: (i, 0))]
    out_shape = [jax.ShapeDtypeStruct((n_rows, MLA_HEADS * HEAD_W), BF16),
                 jax.ShapeDtypeStruct((n_rows, MLA_HEADS * MLA_V), BF16)]
    if emit_ckv:
        out_specs.append(pl.BlockSpec((tm, KV_LORA), lambda i: (i, 0)))
        out_shape.append(jax.ShapeDtypeStruct((n_rows, KV_LORA), F32))
    return pl.pallas_call(
        functools.partial(_kv_kernel, norm=norm, emit_ckv=emit_ckv),
        grid=(n_rows // tm,),
        in_specs=[pl.BlockSpec((tm, KV_LORA), lambda i: (r0 + i, ckv_col)),
                  _resident((1, KV_LORA)),
                  pl.BlockSpec((tm, LANES), lambda i: (r0 + i, 0)),
                  _resident((KV_LORA, MLA_HEADS * (MLA_NOPE + MLA_V))),
                  pl.BlockSpec((tm, 2 * LANES), lambda i: (rope_row(i), 0))],
        out_specs=out_specs, out_shape=out_shape,
        compiler_params=_cparams(1),
        name=f"mla_kv_{row0}_{n_rows}",
    )(ckv_src, kv_norm, kr_src, w_kv, rope_tab)


_EXP2_SCALE = float((MLA_NOPE + MLA_ROPE) ** -0.5 * np.log2(np.e))


def _attn_kernel(*refs, n_seg, heads, tq, lookahead):
    q_ref = refs[0]
    kv = refs[1:1 + 2 * n_seg]
    o_ref = refs[1 + 2 * n_seg]
    nt = q_ref.shape[0] // tq
    work = [(h, t) for h in range(heads) for t in range(nt)]

    def scores(h, t):
        hs = slice(h * HEAD_W, (h + 1) * HEAD_W)
        q = q_ref[t * tq:(t + 1) * tq, hs]
        return [lax.dot_general(q, kv[2 * g][:, hs], _NT, preferred_element_type=F32) for g in range(n_seg)]

    def row_max(s):
        m = s[0].max(axis=-1, keepdims=True)
        for g in range(1, n_seg):
            m = jnp.maximum(m, s[g].max(axis=-1, keepdims=True))
        return m

    def probs(s, m):
        return [jnp.exp2((s[g] - m) * _EXP2_SCALE).astype(BF16) for g in range(n_seg)]

    def values(g, h):
        v = kv[2 * g + 1][:, h * MLA_V:(h + 1) * MLA_V]
        one_col = jnp.where(lax.broadcasted_iota(jnp.int32, v.shape, 1) == 0, 1.0, 0.0).astype(BF16)
        return jnp.concatenate([v, one_col], axis=1)

    v_aug = {(g, h): values(g, h) for g in range(n_seg) for h in range(heads)}

    def weighted(p, h):
        acc = 0.0
        for g in range(n_seg):
            acc = acc + jnp.dot(p[g], v_aug[g, h], preferred_element_type=F32)
        return acc

    def store(acc, h, t):
        out = acc[:, :MLA_V] / acc[:, MLA_V:MLA_V + 1]
        o_ref[t * tq:(t + 1) * tq, h * MLA_V:(h + 1) * MLA_V] = out.astype(BF16)

    if lookahead >= len(work):
        s = [scores(*w) for w in work]
        m = [row_max(x) for x in s]
        p = [probs(x, y) for x, y in zip(s, m)]
        acc = [weighted(x, w[0]) for x, w in zip(p, work)]
        for x, w in zip(acc, work):
            store(x, *w)
    else:
        s_next = scores(*work[0])
        for idx, (h, t) in enumerate(work):
            s = s_next
            if idx + 1 < len(work):
                s_next = scores(*work[idx + 1])
            store(weighted(probs(s, row_max(s)), h), h, t)


def _attention(q, q_row0, segs, *, n_batch, seq, tq, heads):
    qb0 = q_row0 // seq
    n_hblk = MLA_HEADS // heads
    in_specs = [pl.BlockSpec((seq, heads * HEAD_W), lambda b, h: (qb0 + b, h))]
    args = [q]
    for keys, vals, n_keys in segs:
        in_specs += [pl.BlockSpec((n_keys, heads * HEAD_W), lambda b, h: (b, h)),
                     pl.BlockSpec((n_keys, heads * MLA_V), lambda b, h: (b, h))]
        args += [keys, vals]
    return pl.pallas_call(
        functools.partial(_attn_kernel, n_seg=len(segs), heads=heads, tq=tq,
                          lookahead=1 if seq > tq else heads),
        grid=(n_batch, n_hblk),
        in_specs=in_specs,
        out_specs=pl.BlockSpec((seq, heads * MLA_V), lambda b, h: (b, h)),
        out_shape=jax.ShapeDtypeStruct((n_batch * seq, MLA_HEADS * MLA_V), BF16),
        compiler_params=_cparams(2),
        name=f"mla_attn_{seq}",
    )(*args)


def _merge_kernel(og_ref, om_ref, ga_ref, gb_ref, x_ref, mod_ref, g_ref, wg_ref, wm_ref, wo_ref, o_ref):
    yg = jnp.dot(og_ref[...], wg_ref[...], preferred_element_type=F32)
    ym = jnp.dot(om_ref[...], wm_ref[...], preferred_element_type=F32)
    m = (jax.nn.sigmoid(ga_ref[...]) * yg + jax.nn.sigmoid(gb_ref[...]) * ym).astype(BF16)
    o_ref[...] = jnp.dot(m, wo_ref[...], preferred_element_type=F32)
    _residual(x_ref, o_ref, g_ref, mod_ref, 1, 1.0)


def _merge(o_gla, o_mla, proj, x, stream, mod, gains, w_gla_out, w_mla_out, w_out):
    tm = MERGE_TM
    tile = lambda c0: pl.BlockSpec((tm, D_MODEL), lambda i: (i, c0 // D_MODEL))
    return pl.pallas_call(
        _merge_kernel,
        grid=(stream.n_tok // tm,),
        in_specs=[tile(0), tile(0), tile(COL_GA), tile(COL_GB), tile(0),
                  pl.BlockSpec((None, N_MOD, D_MODEL), lambda i: (stream.mod_row(i, tm), 0, 0)),
                  _resident((6, D_MODEL)),
                  _resident((D_MODEL, D_MODEL)), _resident((D_MODEL, D_MODEL)), _resident((D_MODEL, D_MODEL))],
        out_specs=tile(0),
        out_shape=jax.ShapeDtypeStruct((stream.n_tok, D_MODEL), F32),
        compiler_params=_cparams(1),
        name=f"mixer_merge_{stream.name}",
    )(o_gla, o_mla, proj, proj, x, mod, gains, w_gla_out, w_mla_out, w_out)


_SWAP = np.arange(MLA_ROPE)
_SWAP = np.where(_SWAP % 32 < 16, _SWAP + 16, _SWAP - 16)


_FF_BLOCKS = D_FF // LANES
_U_PER_STEP = FF_TILE // LANES


def _prep_ffn_in_kernel(g_ref, *refs):
    u_refs, o_ref = refs[:-1], refs[-1]
    j = pl.program_id(0)
    col = j * FF_TILE + lax.broadcasted_iota(jnp.int32, (1, FF_TILE), 1)
    o_ref[:, :FF_TILE] = jnp.where(col < D_FF, g_ref[...], 0.0).astype(BF16)
    lane = lax.broadcasted_iota(jnp.int32, (1, LANES), 1)
    for q, u_ref in enumerate(u_refs):
        ucol = j * FF_TILE + q * LANES + lane
        o_ref[:, FF_TILE + q * LANES:FF_TILE + (q + 1) * LANES] = jnp.where(ucol < D_FF, u_ref[...], 0.0).astype(BF16)


def _prep_ffn_in(w, name):
    last = 2 * _FF_BLOCKS - 1
    u_spec = lambda q: pl.BlockSpec((D_MODEL, LANES),
                                    lambda j: (0, jnp.minimum(_FF_BLOCKS + _U_PER_STEP * j + q, last)))
    return pl.pallas_call(
        _prep_ffn_in_kernel,
        grid=(N_FF,),
        in_specs=[pl.BlockSpec((D_MODEL, FF_TILE), lambda j: (0, j))] + [u_spec(q) for q in range(_U_PER_STEP)],
        out_specs=pl.BlockSpec((D_MODEL, 2 * FF_TILE), lambda j: (0, j)),
        out_shape=jax.ShapeDtypeStruct((D_MODEL, N_FF * 2 * FF_TILE), BF16),
        compiler_params=_cparams(1),
        name=name,
    )(w, *([w] * _U_PER_STEP))


_FF_FULL = D_FF // FF_TILE
_FF_TAIL = (D_FF - _FF_FULL * FF_TILE) // LANES


def _prep_ffn_out_kernel(w_ref, *refs):
    tail_refs, o_ref = refs[:-1], refs[-1]
    j = pl.program_id(0)

    @pl.when(j < _FF_FULL)
    def _():
        o_ref[...] = w_ref[...].astype(BF16)

    @pl.when(j == _FF_FULL)
    def _():
        for t, t_ref in enumerate(tail_refs):
            o_ref[t * LANES:(t + 1) * LANES, :] = t_ref[...].astype(BF16)
        o_ref[_FF_TAIL * LANES:, :] = jnp.zeros((FF_TILE - _FF_TAIL * LANES, D_MODEL), BF16)


def _prep_ffn_out(w, name):
    tail = lambda t: pl.BlockSpec((LANES, D_MODEL), lambda j: (_FF_FULL * (FF_TILE // LANES) + t, 0))
    return pl.pallas_call(
        _prep_ffn_out_kernel,
        grid=(N_FF,),
        in_specs=[pl.BlockSpec((FF_TILE, D_MODEL), lambda j: (jnp.minimum(j, _FF_FULL - 1), 0))]
        + [tail(t) for t in range(_FF_TAIL)],
        out_specs=pl.BlockSpec((FF_TILE, D_MODEL), lambda j: (j, 0)),
        out_shape=jax.ShapeDtypeStruct((FF_PAD, D_MODEL), BF16),
        compiler_params=_cparams(1),
        name=name,
    )(w, *([w] * _FF_TAIL))


_W_IN_GROUPS = ((0, 6144), (7264, 2048), (9312, 2048), (6176, 512), (6688, 512))
_W_IN_GROUPS_T = ((0, 6144), (7264, 2048), (9312, 2048), (6176, 1024))
_W_IN_ROW_UNIT = 32


def _prep_w_in_kernel(start_ref, w_ref, o_ref):
    del start_ref
    o_ref[...] = w_ref[...].astype(BF16)


def _prep_w_in(w):
    wt = jnp.swapaxes(w, 0, 1)
    tn = PROJ_TN
    starts = np.array([(s + o) // _W_IN_ROW_UNIT for s, n in _W_IN_GROUPS_T for o in range(0, n, tn)], np.int32)
    main = pl.pallas_call(
        _prep_w_in_kernel,
        grid_spec=pltpu.PrefetchScalarGridSpec(
            num_scalar_prefetch=1,
            grid=(PROJ_MAIN // tn,),
            in_specs=[pl.BlockSpec((pl.Element(tn), pl.Element(D_MODEL)),
                                   lambda j, start: (start[j] * _W_IN_ROW_UNIT, 0))],
            out_specs=pl.BlockSpec((tn, D_MODEL), lambda j, start: (j, 0))),
        out_shape=jax.ShapeDtypeStruct((PROJ_MAIN, D_MODEL), BF16),
        compiler_params=_cparams(1),
        name="prep_w_in",
    )(jnp.asarray(starts), wt)
    af, ab, kr = wt[6144:6160], wt[6160:6176], wt[7200:7264]
    small = jnp.concatenate([kr, kr[_SWAP], af, ab,
                             jnp.zeros((PROJ_SMALL - 2 * MLA_ROPE - 2 * GLA_RANK, D_MODEL), F32)], axis=0).astype(BF16)
    return main, small


def _prep_w_uq(w):
    w = w.astype(BF16).reshape(Q_LORA, MLA_HEADS, MLA_NOPE + MLA_ROPE)
    rope = w[..., MLA_NOPE:]
    return jnp.concatenate([w[..., :MLA_NOPE], rope, rope[..., _SWAP]], axis=-1).reshape(Q_LORA, MLA_HEADS * HEAD_W)


def _prep_w_ukv(w):
    w = w.astype(BF16).reshape(KV_LORA, MLA_HEADS, MLA_NOPE + MLA_V)
    return jnp.concatenate([w[..., :MLA_NOPE].reshape(KV_LORA, -1), w[..., MLA_NOPE:].reshape(KV_LORA, -1)], axis=1)


def _prep_w_alpha(w):
    out = jnp.zeros((2, LANES, GLA_HEADS * GLA_DK), BF16)
    out = out.at[0, 0:GLA_RANK].set(w[0].astype(BF16))
    return out.at[1, GLA_RANK:2 * GLA_RANK].set(w[1].astype(BF16))


def _rope_table(tm):
    rows = DEC_SEQ // GRID_W
    row = jnp.repeat(jnp.arange(rows), GRID_W).astype(F32)
    col = jnp.tile(jnp.arange(GRID_W), rows).astype(F32)
    half = MLA_ROPE // 2
    inv_freq = ROPE_THETA ** (-jnp.arange(0, half, 2, dtype=F32) / half)
    ang_r, ang_c = row[:, None] * inv_freq, col[:, None] * inv_freq
    cos = jnp.concatenate([jnp.cos(ang_r), jnp.cos(ang_r), jnp.cos(ang_c), jnp.cos(ang_c)], axis=1)
    sin = jnp.concatenate([-jnp.sin(ang_r), jnp.sin(ang_r), -jnp.sin(ang_c), jnp.sin(ang_c)], axis=1)
    z = jnp.zeros((DEC_SEQ, LANES - MLA_ROPE), F32)
    lat = jnp.concatenate([cos, z, sin, z], axis=1)
    ident = jnp.concatenate([jnp.ones((tm, MLA_ROPE), F32), jnp.zeros((tm, 2 * LANES - MLA_ROPE), F32)], axis=1)
    return jnp.concatenate([ident, lat], axis=0)


def kernel(x_prompt, x_sample, cache_ckv, cache_krope, state_gla_fwd, state_gla_bwd, c, c_ctx, w_ada, b_ada, norm_gains, w_ffn1_in, w_ffn1_out, w_ffn2_in, w_ffn2_out, w_in, w_gla_alpha, b_gla_alpha, gla_norm, w_gla_out, q_norm, kv_norm, w_uq, w_ukv, w_mla_out, w_out):
    c_all = jnp.concatenate([c_ctx[None, :], c, jnp.zeros((8 - 1 - DEC_BATCH, D_MODEL), F32)], axis=0)
    mod = _modulation(c_all, w_ada[0], b_ada[0]).reshape(8, N_MOD, D_MODEL)
    gains = norm_gains[0]

    streams = (_CTX, _LAT)
    x = {_CTX: x_prompt.reshape(N_CTX, D_MODEL), _LAT: x_sample.reshape(N_LAT, D_MODEL)}

    w_gu, w_o = _prep_ffn_in(w_ffn1_in[0], "prep_ffn0_in"), _prep_ffn_out(w_ffn1_out[0], "prep_ffn0_out")
    x = {s: _ffn(x[s], s, mod, gains, w_gu, w_o, sub=0) for s in streams}

    w_main, w_small = _prep_w_in(w_in[0])
    proj, small = {}, {}
    for s in streams:
        proj[s], small[s] = _mixer_proj(x[s], s, mod, gains, w_main, w_small)

    wa = _prep_w_alpha(w_gla_alpha[0])
    ba = b_gla_alpha[0]
    gn = gla_norm[0].reshape(1, GLA_DV)
    o_gla = {}
    o_gla[_CTX], s_f, s_b = _gla(proj[_CTX], small[_CTX], wa, ba, gn, seq=SEQ, n_batch=BATCH, row0=0,
                                 emit_state=True)
    (o_gla[_LAT],) = _gla(proj[_LAT], small[_LAT], wa, ba, gn, seq=DEC_SEQ, n_batch=DEC_BATCH, row0=0,
                          s0=(state_gla_fwd[:, 0], state_gla_bwd[:, 0]))

    tm = MLA_TM
    rope_tab = _rope_table(tm)
    w_q = _prep_w_uq(w_uq[0])
    q = {s: _q_proj(proj[s], s, q_norm[0].reshape(1, Q_LORA), w_q, rope_tab) for s in streams}
    w_kv = _prep_w_ukv(w_ukv[0])
    kvn = kv_norm[0].reshape(1, KV_LORA)
    ckv_col = COL_CKV // KV_LORA
    k_ctx, v_ctx, ckv_new = _kv_proj(proj[_CTX], ckv_col, small[_CTX], kvn, w_kv, rope_tab, n_rows=N_CTX, row0=0,
                                     rope_row=lambda i: _CTX.rope_row(i, tm), norm=True, emit_ckv=True)
    k_lat, v_lat = _kv_proj(proj[_LAT], ckv_col, small[_LAT], kvn, w_kv, rope_tab, n_rows=N_LAT, row0=0,
                            rope_row=lambda i: _LAT.rope_row(i, tm), norm=True, emit_ckv=False)
    n_past = DEC_BATCH * PAST_LEN
    kr_past = jnp.pad(cache_krope[:, 0].reshape(n_past, MLA_ROPE), ((0, 0), (0, LANES - MLA_ROPE)))
    k_past, v_past = _kv_proj(cache_ckv[:, 0].reshape(n_past, KV_LORA), 0, kr_past, kvn, w_kv, rope_tab,
                              n_rows=n_past, row0=0, rope_row=lambda i: 0 * i, norm=False, emit_ckv=False)
    o_mla = {
        _CTX: _attention(q[_CTX], 0, [(k_ctx, v_ctx, SEQ)], n_batch=BATCH, seq=SEQ, tq=SEQ, heads=MLA_HEADS),
        _LAT: _attention(q[_LAT], 0, [(k_past, v_past, PAST_LEN), (k_lat, v_lat, DEC_SEQ)],
                         n_batch=DEC_BATCH, seq=DEC_SEQ, tq=ATT_TQ, heads=1)}

    w_merge = (w_gla_out[0].astype(BF16), w_mla_out[0].astype(BF16), w_out[0].astype(BF16))
    x = {s: _merge(o_gla[s], o_mla[s], proj[s], x[s], s, mod, gains, *w_merge) for s in streams}
    w_gu, w_o = _prep_ffn_in(w_ffn2_in[0], "prep_ffn2_in"), _prep_ffn_out(w_ffn2_out[0], "prep_ffn2_out")
    y = {s: _ffn(x[s], s, mod, gains, w_gu, w_o, sub=2) for s in streams}

    y_prompt = y[_CTX].reshape(BATCH, SEQ, D_MODEL)
    y_sample = y[_LAT].reshape(DEC_BATCH, DEC_SEQ, D_MODEL)
    new_ckv = ckv_new.reshape(BATCH, 1, SEQ, KV_LORA)
    new_krope = small[_CTX][:, :MLA_ROPE].reshape(BATCH, 1, SEQ, MLA_ROPE)
    return (y_prompt, y_sample, new_ckv, new_krope,
            s_f.reshape(BATCH, 1, GLA_HEADS, GLA_DK, GLA_DV), s_b.reshape(BATCH, 1, GLA_HEADS, GLA_DK, GLA_DV))
```

```python
import functools
from typing import NamedTuple

import numpy as np
import jax
import jax.numpy as jnp
from jax import lax
from jax.experimental import pallas as pl
from jax.experimental.pallas import tpu as pltpu

F32 = jnp.float32
BF16 = jnp.bfloat16

D_MODEL = 2048
BATCH, SEQ = 16, 256
DEC_BATCH, DEC_SEQ = 4, 2048
PAST_LEN = 256
GRID_W = 64
D_FF = 5504
N_MOD = 9
GLA_HEADS, GLA_DK, GLA_DV = 4, 256, 512
GLA_RANK = 16
GLA_TAU = 16.0
GLA_CHUNK = 64
GLA_BLOCK = 256
GLA_CHUNKS_PER_ITER = 8
MLA_HEADS, MLA_NOPE, MLA_ROPE, MLA_V = 16, 128, 64, 128
Q_LORA = KV_LORA = 512
ROPE_THETA = 10000.0
NORM_EPS = 1e-6

N_CTX = BATCH * SEQ
N_LAT = DEC_BATCH * DEC_SEQ
N_TOK = N_CTX + N_LAT

LANES = 128
V7X_VMEM_BYTES = 64 * 1024 * 1024
V7X_VMEM_LIMIT_BYTES = 60000 * 1024

FF_TILE = 512
FF_PAD = -(-D_FF // FF_TILE) * FF_TILE
N_FF = FF_PAD // FF_TILE
FF_LAST = D_FF - (N_FF - 1) * FF_TILE
FFN_TM = 1024
FFN_ROWS = 512
PROJ_TM, PROJ_TN = 1024, 1024
MLA_TM = 1024
MERGE_TM = 256
ATT_TQ = 512
HEAD_W = 2 * LANES

COL_Q, COL_K, COL_V, COL_R = 0, 1024, 2048, 4096
COL_GA, COL_GB, COL_CQ, COL_CKV = 6144, 8192, 10240, 10752
PROJ_MAIN = 11264
PROJ_SMALL = 256


class _Stream(NamedTuple):
    name: str
    n_batch: int
    seq: int
    mod_base: int
    per_batch_mod: bool
    rope: bool

    @property
    def n_tok(self):
        return self.n_batch * self.seq

    def mod_row(self, tile, tm):
        return self.mod_base + (tile * tm // self.seq if self.per_batch_mod else 0 * tile)

    def rope_row(self, tile, tm):
        return 1 + tile % (self.seq // tm) if self.rope else 0 * tile


_CTX = _Stream("ctx", BATCH, SEQ, 0, False, False)
_LAT = _Stream("lat", DEC_BATCH, DEC_SEQ, 1, True, True)


def _rms(x, gain):
    ms = jnp.mean(x * x, axis=-1, keepdims=True)
    return x * lax.rsqrt(ms + NORM_EPS) * gain


def _silu(x):
    return x * jax.nn.sigmoid(x)


_NT = (((1,), (1,)), ((), ()))
_TN = (((0,), (0,)), ((), ()))


def _cparams(n_axes, vmem_limit_bytes=V7X_VMEM_LIMIT_BYTES):
    return pltpu.CompilerParams(dimension_semantics=("arbitrary",) * n_axes, vmem_limit_bytes=vmem_limit_bytes)


def _resident(shape):
    nd = len(shape)
    return pl.BlockSpec(shape, lambda *_: (0,) * nd, pipeline_mode=pl.Buffered(1))


def _mod_kernel(c_ref, w_ref, b_ref, o_ref):
    s = _silu(c_ref[...]).astype(BF16)
    o_ref[...] = jnp.dot(s, w_ref[...].astype(BF16), preferred_element_type=F32) + b_ref[...]


def _modulation(c_all, w_ada, b_ada):
    n = w_ada.shape[1]
    tn = 1024
    return pl.pallas_call(
        _mod_kernel,
        grid=(n // tn,),
        in_specs=[pl.BlockSpec((8, D_MODEL), lambda j: (0, 0)),
                  pl.BlockSpec((D_MODEL, tn), lambda j: (0, j)),
                  pl.BlockSpec((1, tn), lambda j: (0, j))],
        out_specs=pl.BlockSpec((8, tn), lambda j: (0, j)),
        out_shape=jax.ShapeDtypeStruct((8, n), F32),
        compiler_params=_cparams(1),
        name="adaln_mod",
    )(c_all, w_ada, b_ada.reshape(1, n))


def _row(ref, r):
    return ref[r:r + 1, :]


def _norm_mod(x_ref, h_ref, g_ref, mod_ref, sub):
    h = _rms(x_ref[...], _row(g_ref, 2 * sub)) * (1.0 + _row(mod_ref, 3 * sub + 1)) + _row(mod_ref, 3 * sub)
    h_ref[...] = h.astype(BF16)


def _residual(x_ref, o_ref, g_ref, mod_ref, sub, gate_scale):
    y = _rms(o_ref[...], _row(g_ref, 2 * sub + 1))
    o_ref[...] = x_ref[...] + (gate_scale * _row(mod_ref, 3 * sub + 2)) * y


def _ffn_kernel(x_ref, mod_ref, g_ref, wgu_ref, wo_ref, o_ref, h_ref, *, sub):
    j = pl.program_id(1)

    @pl.when(j == 0)
    def _():
        _norm_mod(x_ref, h_ref, g_ref, mod_ref, sub)
        o_ref[...] = jnp.zeros_like(o_ref)

    def chunk(width):
        for r in range(0, h_ref.shape[0], FFN_ROWS):
            rows = slice(r, r + FFN_ROWS)
            gu = jnp.dot(h_ref[rows, :], wgu_ref[:, :2 * width], preferred_element_type=F32)
            a = (_silu(gu[:, :width]) * gu[:, width:]).astype(BF16)
            o_ref[rows, :] += jnp.dot(a, wo_ref[:width, :], preferred_element_type=F32)

    last = pl.num_programs(1) - 1

    @pl.when(j < last)
    def _():
        chunk(FF_TILE)

    @pl.when(j == last)
    def _():
        chunk(FF_LAST)
        _residual(x_ref, o_ref, g_ref, mod_ref, sub, 0.5)


def _ffn(x, stream, mod, gains, w_gu, w_o, sub):
    tm = FFN_TM
    return pl.pallas_call(
        functools.partial(_ffn_kernel, sub=sub),
        grid=(stream.n_tok // tm, N_FF),
        in_specs=[pl.BlockSpec((tm, D_MODEL), lambda i, j: (i, 0)),
                  pl.BlockSpec((None, N_MOD, D_MODEL), lambda i, j: (stream.mod_row(i, tm), 0, 0)),
                  pl.BlockSpec((6, D_MODEL), lambda i, j: (0, 0)),
                  pl.BlockSpec((D_MODEL, 2 * FF_TILE), lambda i, j: (0, j)),
                  pl.BlockSpec((FF_TILE, D_MODEL), lambda i, j: (j, 0))],
        out_specs=pl.BlockSpec((tm, D_MODEL), lambda i, j: (i, 0)),
        out_shape=jax.ShapeDtypeStruct((stream.n_tok, D_MODEL), F32),
        scratch_shapes=[pltpu.VMEM((tm, D_MODEL), BF16)],
        compiler_params=_cparams(2, V7X_VMEM_BYTES - 2 * 1024 * 1024),
        name=f"ffn{sub}_{stream.name}",
    )(x, mod, gains, w_gu, w_o)


def _proj_kernel(x_ref, mod_ref, g_ref, wt_ref, wst_ref, o_ref, os_ref, h_ref):
    j = pl.program_id(1)

    @pl.when(j == 0)
    def _():
        _norm_mod(x_ref, h_ref, g_ref, mod_ref, 1)
        os_ref[...] = lax.dot_general(h_ref[...], wst_ref[...], _NT, preferred_element_type=F32)

    o_ref[...] = lax.dot_general(h_ref[...], wt_ref[...], _NT, preferred_element_type=F32)


def _mixer_proj(x, stream, mod, gains, w_main, w_small):
    tm, tn = PROJ_TM, PROJ_TN
    n_tok = stream.n_tok
    return pl.pallas_call(
        _proj_kernel,
        grid=(n_tok // tm, PROJ_MAIN // tn),
        in_specs=[pl.BlockSpec((tm, D_MODEL), lambda i, j: (i, 0)),
                  pl.BlockSpec((None, N_MOD, D_MODEL), lambda i, j: (stream.mod_row(i, tm), 0, 0)),
                  pl.BlockSpec((6, D_MODEL), lambda i, j: (0, 0)),
                  pl.BlockSpec((tn, D_MODEL), lambda i, j: (j, 0)),
                  pl.BlockSpec((PROJ_SMALL, D_MODEL), lambda i, j: (0, 0))],
        out_specs=[pl.BlockSpec((tm, tn), lambda i, j: (i, j)),
                   pl.BlockSpec((tm, PROJ_SMALL), lambda i, j: (i, 0))],
        out_shape=[jax.ShapeDtypeStruct((n_tok, PROJ_MAIN), F32),
                   jax.ShapeDtypeStruct((n_tok, PROJ_SMALL), F32)],
        scratch_shapes=[pltpu.VMEM((tm, D_MODEL), BF16)],
        compiler_params=_cparams(2),
        name=f"mixer_proj_{stream.name}",
    )(x, mod, gains, w_main, w_small)


def _log_sigmoid(x):
    return jnp.minimum(x, 0.0) - jnp.log(1.0 + jnp.exp(-jnp.abs(x)))


def _gla_kernel(*refs, seq, has_s0, emit_state):
    q_ref, k_ref, v_ref, r_ref, a_ref, wa_ref, ba_ref, gn_ref = refs[:8]
    pos = 8
    if has_s0:
        s0_refs = refs[pos:pos + 2]
        pos += 2
    o_ref = refs[pos]
    pos += 1
    if emit_state:
        s_out_refs = refs[pos:pos + 2]
        pos += 2
    qd_scr, ke_scr, vb_scr, dec_scr, o_scr, st_scr = refs[pos:pos + 6]

    c, blk_rows = GLA_CHUNK, GLA_BLOCK
    cpb = blk_rows // c
    nc = seq // c
    for d in range(2):
        st_scr[d] = s0_refs[d][...] if has_s0 else jnp.zeros((GLA_DK, GLA_DV), F32)

    row = lax.broadcasted_iota(jnp.int32, (blk_rows, blk_rows), 0)
    col = lax.broadcasted_iota(jnp.int32, (blk_rows, blk_rows), 1)
    same_chunk = (row // c) == (col // c)
    masks = (same_chunk & (col <= row), same_chunk & (col >= row))

    tri = [jnp.where(masks[d], 1.0, 0.0).astype(BF16) for d in range(2)]
    n_blocks = seq // blk_rows
    blocks_per_iter = 2 if n_blocks % 2 == 0 else 1

    def bulk(it, carry):
        blks = [it * blocks_per_iter + j for j in range(blocks_per_iter)]
        rows = [pl.ds(pl.multiple_of(blk * blk_rows, blk_rows), blk_rows) for blk in blks]
        nb = range(blocks_per_iter)
        a_in = [a_ref[rows[j], :].astype(BF16) for j in nb]
        q = [q_ref[rows[j], :] * (GLA_DK ** -0.5) for j in nb]
        k = [k_ref[rows[j], :] for j in nb]
        vb = [v_ref[rows[j], :].astype(BF16) for j in nb]
        units = [(j, d) for j in nb for d in range(2)]
        x = {(j, d): jnp.dot(a_in[j], wa_ref[d], preferred_element_type=F32) + ba_ref[d:d + 1, :] for j, d in units}
        la = {u: _log_sigmoid(x[u]) * (1.0 / GLA_TAU) for u in units}
        la1 = {u: la[u].astype(BF16) for u in units}
        rem = {u: la[u] - la1[u].astype(F32) for u in units}
        la2 = {u: rem[u].astype(BF16) for u in units}
        la3 = {u: (rem[u] - la2[u].astype(F32)).astype(BF16) for u in units}
        b = {(j, d): jnp.dot(tri[d], la1[j, d], preferred_element_type=F32)
             + jnp.dot(tri[d], la2[j, d], preferred_element_type=F32)
             + jnp.dot(tri[d], la3[j, d], preferred_element_type=F32) for j, d in units}
        b3 = {u: b[u].reshape(cpb, c, GLA_DK) for u in units}
        b_last = {(j, d): b3[j, d][:, c - 1:c, :] if d == 0 else b3[j, d][:, 0:1, :] for j, d in units}
        dec = {u: jnp.exp(b_last[u]) for u in units}
        dec_rows = {u: jnp.broadcast_to(dec[u], (cpb, c, GLA_DK)).reshape(blk_rows, GLA_DK) for u in units}
        q_dec = {(j, d): (q[j] * jnp.exp(b[j, d])).astype(BF16) for j, d in units}
        k_inv = {(j, d): k[j] * jnp.exp(-b[j, d]) for j, d in units}
        k_end = {u: (k_inv[u] * dec_rows[u]).astype(BF16) for u in units}
        a = {u: lax.dot_general(q_dec[u], k_inv[u].astype(BF16), _NT, preferred_element_type=F32) for u in units}
        a = {(j, d): jnp.where(masks[d], a[j, d], 0.0).astype(BF16) for j, d in units}
        o = {(j, d): jnp.dot(a[j, d], vb[j], preferred_element_type=F32) for j, d in units}
        for j in nb:
            vb_scr[rows[j], :] = vb[j]
        for j, d in units:
            o_scr[d, rows[j], :] = o[j, d]
            qd_scr[d, rows[j], :] = q_dec[j, d]
            ke_scr[d, rows[j], :] = k_end[j, d]
            dec8 = pl.ds(pl.multiple_of(blks[j] * (cpb * 8), cpb * 8), cpb * 8)
            dec_scr[d, dec8, :] = jnp.broadcast_to(dec[j, d], (cpb, 8, GLA_DK)).reshape(cpb * 8, GLA_DK)
        return carry

    lax.fori_loop(0, n_blocks // blocks_per_iter, bulk, 0)

    cpi = min(GLA_CHUNKS_PER_ITER, nc)

    def recur(i, carry):
        steps = [(u, d) for u in range(cpi) for d in range(2)]
        chunk = {(u, d): cpi * i + u if d == 0 else nc - 1 - (cpi * i + u) for u, d in steps}
        rows = {s: pl.ds(pl.multiple_of(chunk[s] * c, c), c) for s in steps}
        dec_rows = {s: pl.ds(pl.multiple_of(chunk[s] * 8, 8), 8) for s in steps}
        kv = {(u, d): lax.dot_general(ke_scr[d, rows[u, d], :], vb_scr[rows[u, d], :], _TN,
                                      preferred_element_type=F32) for u, d in steps}
        dec_col = {s: jnp.broadcast_to(dec_scr[s[1], dec_rows[s], :][0:1, :], (LANES, GLA_DK)).T for s in steps}
        dec_full = {s: jnp.concatenate([dec_col[s]] * (GLA_DV // LANES), axis=1) for s in steps}
        s_t = [st_scr[d] for d in range(2)]
        for u, d in steps:
            inter = jnp.dot(qd_scr[d, rows[u, d], :], s_t[d].astype(BF16), preferred_element_type=F32)
            o_scr[d, rows[u, d], :] += inter
            s_t[d] = s_t[d] * dec_full[u, d] + kv[u, d]
        for d in range(2):
            st_scr[d] = s_t[d]
        return carry

    lax.fori_loop(0, nc // cpi, recur, 0)

    def finish(i, carry):
        rows = pl.ds(pl.multiple_of(i * blk_rows, blk_rows), blk_rows)
        o = _rms(o_scr[0, rows, :] + o_scr[1, rows, :], gn_ref[...])
        o_ref[rows, :] = (o * _silu(r_ref[rows, :])).astype(BF16)
        return carry

    lax.fori_loop(0, seq // blk_rows, finish, 0)
    if emit_state:
        for d in range(2):
            s_out_refs[d][...] = st_scr[d]


def _gla(proj, small, wa, ba, gn, *, seq, n_batch, row0, s0=None, emit_state=False):
    rb = row0 // seq
    dk, dv = GLA_DK, GLA_DV
    in_specs = [pl.BlockSpec((seq, dk), lambda b, h: (rb + b, COL_Q // dk + h)),
                pl.BlockSpec((seq, dk), lambda b, h: (rb + b, COL_K // dk + h)),
                pl.BlockSpec((seq, dv), lambda b, h: (rb + b, COL_V // dv + h)),
                pl.BlockSpec((seq, dv), lambda b, h: (rb + b, COL_R // dv + h)),
                pl.BlockSpec((seq, LANES), lambda b, h: (rb + b, 1)),
                pl.BlockSpec((2, LANES, dk), lambda b, h: (0, 0, h)),
                pl.BlockSpec((2, dk), lambda b, h: (0, h)),
                pl.BlockSpec((1, dv), lambda b, h: (0, 0))]
    args = [proj, proj, proj, proj, small, wa, ba, gn]
    state_spec = pl.BlockSpec((None, None, dk, dv), lambda b, h: (b, h, 0, 0))
    if s0 is not None:
        in_specs += [state_spec, state_spec]
        args += list(s0)
    out_specs = [pl.BlockSpec((seq, dv), lambda b, h: (b, h))]
    out_shape = [jax.ShapeDtypeStruct((n_batch * seq, GLA_HEADS * dv), BF16)]
    if emit_state:
        out_specs += [state_spec, state_spec]
        out_shape += [jax.ShapeDtypeStruct((n_batch, GLA_HEADS, dk, dv), F32)] * 2
    return pl.pallas_call(
        functools.partial(_gla_kernel, seq=seq, has_s0=s0 is not None, emit_state=emit_state),
        grid=(n_batch, GLA_HEADS),
        in_specs=in_specs, out_specs=out_specs, out_shape=out_shape,
        scratch_shapes=[pltpu.VMEM((2, seq, dk), BF16),
                        pltpu.VMEM((2, seq, dk), BF16),
                        pltpu.VMEM((seq, dv), BF16),
                        pltpu.VMEM((2, seq // GLA_CHUNK * 8, dk), F32),
                        pltpu.VMEM((2, seq, dv), F32),
                        pltpu.VMEM((2, dk, dv), F32)],
        compiler_params=_cparams(2),
        name=f"gla_{seq}",
    )(*args)


def _rope_block(x, cs_ref):
    return x * cs_ref[:, :LANES] + pltpu.roll(x, MLA_ROPE, axis=1) * cs_ref[:, LANES:]


def _q_kernel(cq_ref, qn_ref, w_ref, cs_ref, o_ref):
    n = _rms(cq_ref[...], qn_ref[...]).astype(BF16)
    for h in range(MLA_HEADS):
        q = jnp.dot(n, w_ref[:, h * HEAD_W:(h + 1) * HEAD_W], preferred_element_type=F32)
        o_ref[:, h * HEAD_W:h * HEAD_W + LANES] = q[:, :LANES].astype(BF16)
        o_ref[:, h * HEAD_W + LANES:(h + 1) * HEAD_W] = _rope_block(q[:, LANES:], cs_ref).astype(BF16)


def _q_proj(proj, stream, q_norm, w_q, rope_tab):
    tm = MLA_TM
    return pl.pallas_call(
        _q_kernel,
        grid=(stream.n_tok // tm,),
        in_specs=[pl.BlockSpec((tm, Q_LORA), lambda i: (i, COL_CQ // Q_LORA)),
                  _resident((1, Q_LORA)),
                  _resident((Q_LORA, MLA_HEADS * HEAD_W)),
                  pl.BlockSpec((tm, 2 * LANES), lambda i: (stream.rope_row(i, tm), 0))],
        out_specs=pl.BlockSpec((tm, MLA_HEADS * HEAD_W), lambda i: (i, 0)),
        out_shape=jax.ShapeDtypeStruct((stream.n_tok, MLA_HEADS * HEAD_W), BF16),
        compiler_params=_cparams(1),
        name=f"mla_q_{stream.name}",
    )(proj, q_norm, w_q, rope_tab)


def _kv_kernel(*refs, norm, emit_ckv):
    ckv_ref, kn_ref, kr_ref, w_ref, cs_ref, k_ref, v_ref = refs[:7]
    c = ckv_ref[...]
    if norm:
        c = _rms(c, kn_ref[...])
    if emit_ckv:
        refs[7][...] = c
    cb = c.astype(BF16)
    nk = MLA_HEADS * MLA_NOPE
    kr = _rope_block(kr_ref[...], cs_ref).astype(BF16)
    v_ref[...] = jnp.dot(cb, w_ref[:, nk:], preferred_element_type=F32).astype(BF16)
    for h0 in range(0, MLA_HEADS, 2):
        kn = jnp.dot(cb, w_ref[:, h0 * MLA_NOPE:(h0 + 2) * MLA_NOPE], preferred_element_type=F32).astype(BF16)
        for h in (h0, h0 + 1):
            k_ref[:, h * HEAD_W:h * HEAD_W + LANES] = kn[:, (h - h0) * MLA_NOPE:(h - h0 + 1) * MLA_NOPE]
            k_ref[:, h * HEAD_W + LANES:(h + 1) * HEAD_W] = kr


def _kv_proj(ckv_src, ckv_col, kr_src, kv_norm, w_kv, rope_tab, *, n_rows, row0, rope_row, norm, emit_ckv):
    tm = MLA_TM
    r0 = row0 // tm
    out_specs = [pl.BlockSpec((tm, MLA_HEADS * HEAD_W), lambda i: (i, 0)),
                 pl.BlockSpec((tm, MLA_HEADS * MLA_V), lambda i: (i, 0))]
    out_shape = [jax.ShapeDtypeStruct((n_rows, MLA_HEADS * HEAD_W), BF16),
                 jax.ShapeDtypeStruct((n_rows, MLA_HEADS * MLA_V), BF16)]
    if emit_ckv:
        out_specs.append(pl.BlockSpec((tm, KV_LORA), lambda i: (i, 0)))
        out_shape.append(jax.ShapeDtypeStruct((n_rows, KV_LORA), F32))
    return pl.pallas_call(
        functools.partial(_kv_kernel, norm=norm, emit_ckv=emit_ckv),
        grid=(n_rows // tm,),
        in_specs=[pl.BlockSpec((tm, KV_LORA), lambda i: (r0 + i, ckv_col)),
                  _resident((1, KV_LORA)),
                  pl.BlockSpec((tm, LANES), lambda i: (r0 + i, 0)),
                  _resident((KV_LORA, MLA_HEADS * (MLA_NOPE + MLA_V))),
                  pl.BlockSpec((tm, 2 * LANES), lambda i: (rope_row(i), 0))],
        out_specs=out_specs, out_shape=out_shape,
        compiler_params=_cparams(1),
        name=f"mla_kv_{row0}_{n_rows}",
    )(ckv_src, kv_norm, kr_src, w_kv, rope_tab)


_EXP2_SCALE = float((MLA_NOPE + MLA_ROPE) ** -0.5 * np.log2(np.e))


def _attn_kernel(*refs, n_seg, heads, tq, lookahead):
    q_ref = refs[0]
    kv = refs[1:1 + 2 * n_seg]
    o_ref = refs[1 + 2 * n_seg]
    nt = q_ref.shape[0] // tq
    work = [(h, t) for h in range(heads) for t in range(nt)]

    def scores(h, t):
        hs = slice(h * HEAD_W, (h + 1) * HEAD_W)
        q = q_ref[t * tq:(t + 1) * tq, hs]
        return [lax.dot_general(q, kv[2 * g][:, hs], _NT, preferred_element_type=F32) for g in range(n_seg)]

    def row_max(s):
        m = s[0].max(axis=-1, keepdims=True)
        for g in range(1, n_seg):
            m = jnp.maximum(m, s[g].max(axis=-1, keepdims=True))
        return m

    def probs(s, m):
        return [jnp.exp2((s[g] - m) * _EXP2_SCALE).astype(BF16) for g in range(n_seg)]

    def values(g, h):
        v = kv[2 * g + 1][:, h * MLA_V:(h + 1) * MLA_V]
        one_col = jnp.where(lax.broadcasted_iota(jnp.int32, v.shape, 1) == 0, 1.0, 0.0).astype(BF16)
        return jnp.concatenate([v, one_col], axis=1)

    v_aug = {(g, h): values(g, h) for g in range(n_seg) for h in range(heads)}

    def weighted(p, h):
        acc = 0.0
        for g in range(n_seg):
            acc = acc + jnp.dot(p[g], v_aug[g, h], preferred_element_type=F32)
        return acc

    def store(acc, h, t):
        out = acc[:, :MLA_V] / acc[:, MLA_V:MLA_V + 1]
        o_ref[t * tq:(t + 1) * tq, h * MLA_V:(h + 1) * MLA_V] = out.astype(BF16)

    if lookahead >= len(work):
        s = [scores(*w) for w in work]
        m = [row_max(x) for x in s]
        p = [probs(x, y) for x, y in zip(s, m)]
        acc = [weighted(x, w[0]) for x, w in zip(p, work)]
        for x, w in zip(acc, work):
            store(x, *w)
    else:
        s_next = scores(*work[0])
        for idx, (h, t) in enumerate(work):
            s = s_next
            if idx + 1 < len(work):
                s_next = scores(*work[idx + 1])
            store(weighted(probs(s, row_max(s)), h), h, t)


def _attention(q, q_row0, segs, *, n_batch, seq, tq, heads):
    qb0 = q_row0 // seq
    n_hblk = MLA_HEADS // heads
    in_specs = [pl.BlockSpec((seq, heads * HEAD_W), lambda b, h: (qb0 + b, h))]
    args = [q]
    for keys, vals, n_keys in segs:
        in_specs += [pl.BlockSpec((n_keys, heads * HEAD_W), lambda b, h: (b, h)),
                     pl.BlockSpec((n_keys, heads * MLA_V), lambda b, h: (b, h))]
        args += [keys, vals]
    return pl.pallas_call(
        functools.partial(_attn_kernel, n_seg=len(segs), heads=heads, tq=tq,
                          lookahead=1 if seq > tq else heads),
        grid=(n_batch, n_hblk),
        in_specs=in_specs,
        out_specs=pl.BlockSpec((seq, heads * MLA_V), lambda b, h: (b, h)),
        out_shape=jax.ShapeDtypeStruct((n_batch * seq, MLA_HEADS * MLA_V), BF16),
        compiler_params=_cparams(2),
        name=f"mla_attn_{seq}",
    )(*args)


def _merge_kernel(og_ref, om_ref, ga_ref, gb_ref, x_ref, mod_ref, g_ref, wg_ref, wm_ref, wo_ref, o_ref):
    yg = jnp.dot(og_ref[...], wg_ref[...], preferred_element_type=F32)
    ym = jnp.dot(om_ref[...], wm_ref[...], preferred_element_type=F32)
    m = (jax.nn.sigmoid(ga_ref[...]) * yg + jax.nn.sigmoid(gb_ref[...]) * ym).astype(BF16)
    o_ref[...] = jnp.dot(m, wo_ref[...], preferred_element_type=F32)
    _residual(x_ref, o_ref, g_ref, mod_ref, 1, 1.0)


def _merge(o_gla, o_mla, proj, x, stream, mod, gains, w_gla_out, w_mla_out, w_out):
    tm = MERGE_TM
    tile = lambda c0: pl.BlockSpec((tm, D_MODEL), lambda i: (i, c0 // D_MODEL))
    return pl.pallas_call(
        _merge_kernel,
        grid=(stream.n_tok // tm,),
        in_specs=[tile(0), tile(0), tile(COL_GA), tile(COL_GB), tile(0),
                  pl.BlockSpec((None, N_MOD, D_MODEL), lambda i: (stream.mod_row(i, tm), 0, 0)),
                  _resident((6, D_MODEL)),
                  _resident((D_MODEL, D_MODEL)), _resident((D_MODEL, D_MODEL)), _resident((D_MODEL, D_MODEL))],
        out_specs=tile(0),
        out_shape=jax.ShapeDtypeStruct((stream.n_tok, D_MODEL), F32),
        compiler_params=_cparams(1),
        name=f"mixer_merge_{stream.name}",
    )(o_gla, o_mla, proj, proj, x, mod, gains, w_gla_out, w_mla_out, w_out)


_SWAP = np.arange(MLA_ROPE)
_SWAP = np.where(_SWAP % 32 < 16, _SWAP + 16, _SWAP - 16)


_FF_BLOCKS = D_FF // LANES
_U_PER_STEP = FF_TILE // LANES


def _prep_ffn_in_kernel(g_ref, *refs):
    u_refs, o_ref = refs[:-1], refs[-1]
    j = pl.program_id(0)

    @pl.when(j < N_FF - 1)
    def _():
        o_ref[:, :FF_TILE] = g_ref[...].astype(BF16)
        for q, u_ref in enumerate(u_refs):
            o_ref[:, FF_TILE + q * LANES:FF_TILE + (q + 1) * LANES] = u_ref[...].astype(BF16)

    @pl.when(j == N_FF - 1)
    def _():
        o_ref[:, :FF_LAST] = g_ref[:, :FF_LAST].astype(BF16)
        for q in range(FF_LAST // LANES):
            o_ref[:, FF_LAST + q * LANES:FF_LAST + (q + 1) * LANES] = u_refs[q][...].astype(BF16)
        o_ref[:, 2 * FF_LAST:] = jnp.zeros((D_MODEL, 2 * (FF_TILE - FF_LAST)), BF16)


def _prep_ffn_in(w, name):
    last = 2 * _FF_BLOCKS - 1
    u_spec = lambda q: pl.BlockSpec((D_MODEL, LANES),
                                    lambda j: (0, jnp.minimum(_FF_BLOCKS + _U_PER_STEP * j + q, last)))
    return pl.pallas_call(
        _prep_ffn_in_kernel,
        grid=(N_FF,),
        in_specs=[pl.BlockSpec((D_MODEL, FF_TILE), lambda j: (0, j))] + [u_spec(q) for q in range(_U_PER_STEP)],
        out_specs=pl.BlockSpec((D_MODEL, 2 * FF_TILE), lambda j: (0, j)),
        out_shape=jax.ShapeDtypeStruct((D_MODEL, N_FF * 2 * FF_TILE), BF16),
        compiler_params=_cparams(1),
        name=name,
    )(w, *([w] * _U_PER_STEP))


_FF_FULL = D_FF // FF_TILE
_FF_TAIL = (D_FF - _FF_FULL * FF_TILE) // LANES


def _prep_ffn_out_kernel(w_ref, *refs):
    tail_refs, o_ref = refs[:-1], refs[-1]
    j = pl.program_id(0)

    @pl.when(j < _FF_FULL)
    def _():
        o_ref[...] = w_ref[...].astype(BF16)

    @pl.when(j == _FF_FULL)
    def _():
        for t, t_ref in enumerate(tail_refs):
            o_ref[t * LANES:(t + 1) * LANES, :] = t_ref[...].astype(BF16)
        o_ref[_FF_TAIL * LANES:, :] = jnp.zeros((FF_TILE - _FF_TAIL * LANES, D_MODEL), BF16)


def _prep_ffn_out(w, name):
    tail = lambda t: pl.BlockSpec((LANES, D_MODEL), lambda j: (_FF_FULL * (FF_TILE // LANES) + t, 0))
    return pl.pallas_call(
        _prep_ffn_out_kernel,
        grid=(N_FF,),
        in_specs=[pl.BlockSpec((FF_TILE, D_MODEL), lambda j: (jnp.minimum(j, _FF_FULL - 1), 0))]
        + [tail(t) for t in range(_FF_TAIL)],
        out_specs=pl.BlockSpec((FF_TILE, D_MODEL), lambda j: (j, 0)),
        out_shape=jax.ShapeDtypeStruct((FF_PAD, D_MODEL), BF16),
        compiler_params=_cparams(1),
        name=name,
    )(w, *([w] * _FF_TAIL))


_W_IN_GROUPS = ((0, 6144), (7264, 2048), (9312, 2048), (6176, 512), (6688, 512))
_W_IN_GROUPS_T = ((0, 6144), (7264, 2048), (9312, 2048), (6176, 1024))
_W_IN_ROW_UNIT = 32


def _prep_w_in_kernel(start_ref, w_ref, o_ref):
    del start_ref
    o_ref[...] = w_ref[...].astype(BF16)


def _prep_w_in(w):
    wt = jnp.swapaxes(w, 0, 1)
    tn = PROJ_TN
    starts = np.array([(s + o) // _W_IN_ROW_UNIT for s, n in _W_IN_GROUPS_T for o in range(0, n, tn)], np.int32)
    main = pl.pallas_call(
        _prep_w_in_kernel,
        grid_spec=pltpu.PrefetchScalarGridSpec(
            num_scalar_prefetch=1,
            grid=(PROJ_MAIN // tn,),
            in_specs=[pl.BlockSpec((pl.Element(tn), pl.Element(D_MODEL)),
                                   lambda j, start: (start[j] * _W_IN_ROW_UNIT, 0))],
            out_specs=pl.BlockSpec((tn, D_MODEL), lambda j, start: (j, 0))),
        out_shape=jax.ShapeDtypeStruct((PROJ_MAIN, D_MODEL), BF16),
        compiler_params=_cparams(1),
        name="prep_w_in",
    )(jnp.asarray(starts), wt)
    af, ab, kr = wt[6144:6160], wt[6160:6176], wt[7200:7264]
    small = jnp.concatenate([kr, kr[_SWAP], af, ab,
                             jnp.zeros((PROJ_SMALL - 2 * MLA_ROPE - 2 * GLA_RANK, D_MODEL), F32)], axis=0).astype(BF16)
    return main, small


def _prep_w_uq(w):
    w = w.astype(BF16).reshape(Q_LORA, MLA_HEADS, MLA_NOPE + MLA_ROPE)
    rope = w[..., MLA_NOPE:]
    return jnp.concatenate([w[..., :MLA_NOPE], rope, rope[..., _SWAP]], axis=-1).reshape(Q_LORA, MLA_HEADS * HEAD_W)


def _prep_w_ukv(w):
    w = w.astype(BF16).reshape(KV_LORA, MLA_HEADS, MLA_NOPE + MLA_V)
    return jnp.concatenate([w[..., :MLA_NOPE].reshape(KV_LORA, -1), w[..., MLA_NOPE:].reshape(KV_LORA, -1)], axis=1)


def _prep_w_alpha(w):
    out = jnp.zeros((2, LANES, GLA_HEADS * GLA_DK), BF16)
    out = out.at[0, 0:GLA_RANK].set(w[0].astype(BF16))
    return out.at[1, GLA_RANK:2 * GLA_RANK].set(w[1].astype(BF16))


def _rope_table(tm):
    rows = DEC_SEQ // GRID_W
    row = np.repeat(np.arange(rows), GRID_W).astype(np.float64)
    col = np.tile(np.arange(GRID_W), rows).astype(np.float64)
    half = MLA_ROPE // 2
    inv_freq = ROPE_THETA ** (-np.arange(0, half, 2, dtype=np.float64) / half)
    ang_r, ang_c = row[:, None] * inv_freq, col[:, None] * inv_freq
    cos = np.concatenate([np.cos(ang_r), np.cos(ang_r), np.cos(ang_c), np.cos(ang_c)], axis=1)
    sin = np.concatenate([-np.sin(ang_r), np.sin(ang_r), -np.sin(ang_c), np.sin(ang_c)], axis=1)
    z = np.zeros((DEC_SEQ, LANES - MLA_ROPE))
    lat = np.concatenate([cos, z, sin, z], axis=1)
    ident = np.concatenate([np.ones((tm, MLA_ROPE)), np.zeros((tm, 2 * LANES - MLA_ROPE))], axis=1)
    return jnp.asarray(np.concatenate([ident, lat], axis=0), F32)


def kernel(x_prompt, x_sample, cache_ckv, cache_krope, state_gla_fwd, state_gla_bwd, c, c_ctx, w_ada, b_ada, norm_gains, w_ffn1_in, w_ffn1_out, w_ffn2_in, w_ffn2_out, w_in, w_gla_alpha, b_gla_alpha, gla_norm, w_gla_out, q_norm, kv_norm, w_uq, w_ukv, w_mla_out, w_out):
    c_all = jnp.concatenate([c_ctx[None, :], c, jnp.zeros((8 - 1 - DEC_BATCH, D_MODEL), F32)], axis=0)
    mod = _modulation(c_all, w_ada[0], b_ada[0]).reshape(8, N_MOD, D_MODEL)
    gains = norm_gains[0]

    streams = (_CTX, _LAT)
    x = {_CTX: x_prompt.reshape(N_CTX, D_MODEL), _LAT: x_sample.reshape(N_LAT, D_MODEL)}

    w_gu, w_o = _prep_ffn_in(w_ffn1_in[0], "prep_ffn0_in"), _prep_ffn_out(w_ffn1_out[0], "prep_ffn0_out")
    x = {s: _ffn(x[s], s, mod, gains, w_gu, w_o, sub=0) for s in streams}

    w_main, w_small = _prep_w_in(w_in[0])
    proj, small = {}, {}
    for s in streams:
        proj[s], small[s] = _mixer_proj(x[s], s, mod, gains, w_main, w_small)

    wa = _prep_w_alpha(w_gla_alpha[0])
    ba = b_gla_alpha[0]
    gn = gla_norm[0].reshape(1, GLA_DV)
    o_gla = {}
    o_gla[_CTX], s_f, s_b = _gla(proj[_CTX], small[_CTX], wa, ba, gn, seq=SEQ, n_batch=BATCH, row0=0,
                                 emit_state=True)
    (o_gla[_LAT],) = _gla(proj[_LAT], small[_LAT], wa, ba, gn, seq=DEC_SEQ, n_batch=DEC_BATCH, row0=0,
                          s0=(state_gla_fwd[:, 0], state_gla_bwd[:, 0]))

    tm = MLA_TM
    rope_tab = _rope_table(tm)
    w_q = _prep_w_uq(w_uq[0])
    q = {s: _q_proj(proj[s], s, q_norm[0].reshape(1, Q_LORA), w_q, rope_tab) for s in streams}
    w_kv = _prep_w_ukv(w_ukv[0])
    kvn = kv_norm[0].reshape(1, KV_LORA)
    ckv_col = COL_CKV // KV_LORA
    k_ctx, v_ctx, ckv_new = _kv_proj(proj[_CTX], ckv_col, small[_CTX], kvn, w_kv, rope_tab, n_rows=N_CTX, row0=0,
                                     rope_row=lambda i: _CTX.rope_row(i, tm), norm=True, emit_ckv=True)
    k_lat, v_lat = _kv_proj(proj[_LAT], ckv_col, small[_LAT], kvn, w_kv, rope_tab, n_rows=N_LAT, row0=0,
                            rope_row=lambda i: _LAT.rope_row(i, tm), norm=True, emit_ckv=False)
    n_past = DEC_BATCH * PAST_LEN
    kr_past = jnp.pad(cache_krope[:, 0].reshape(n_past, MLA_ROPE), ((0, 0), (0, LANES - MLA_ROPE)))
    k_past, v_past = _kv_proj(cache_ckv[:, 0].reshape(n_past, KV_LORA), 0, kr_past, kvn, w_kv, rope_tab,
                              n_rows=n_past, row0=0, rope_row=lambda i: 0 * i, norm=False, emit_ckv=False)
    o_mla = {
        _CTX: _attention(q[_CTX], 0, [(k_ctx, v_ctx, SEQ)], n_batch=BATCH, seq=SEQ, tq=SEQ, heads=MLA_HEADS),
        _LAT: _attention(q[_LAT], 0, [(k_past, v_past, PAST_LEN), (k_lat, v_lat, DEC_SEQ)],
                         n_batch=DEC_BATCH, seq=DEC_SEQ, tq=ATT_TQ, heads=1)}

    w_merge = (w_gla_out[0].astype(BF16), w_mla_out[0].astype(BF16), w_out[0].astype(BF16))
    x = {s: _merge(o_gla[s], o_mla[s], proj[s], x[s], s, mod, gains, *w_merge) for s in streams}
    w_gu, w_o = _prep_ffn_in(w_ffn2_in[0], "prep_ffn2_in"), _prep_ffn_out(w_ffn2_out[0], "prep_ffn2_out")
    y = {s: _ffn(x[s], s, mod, gains, w_gu, w_o, sub=2) for s in streams}

    y_prompt = y[_CTX].reshape(BATCH, SEQ, D_MODEL)
    y_sample = y[_LAT].reshape(DEC_BATCH, DEC_SEQ, D_MODEL)
    new_ckv = ckv_new.reshape(BATCH, 1, SEQ, KV_LORA)
    new_krope = small[_CTX][:, :MLA_ROPE].reshape(BATCH, 1, SEQ, MLA_ROPE)
    return (y_prompt, y_sample, new_ckv, new_krope,
            s_f.reshape(BATCH, 1, GLA_HEADS, GLA_DK, GLA_DV), s_b.reshape(BATCH, 1, GLA_HEADS, GLA_DK, GLA_DV))
```

```python
import functools
from typing import NamedTuple

import numpy as np
import jax
import jax.numpy as jnp
from jax import lax
from jax.experimental import pallas as pl
from jax.experimental.pallas import tpu as pltpu

F32 = jnp.float32
BF16 = jnp.bfloat16

D_MODEL = 2048
BATCH, SEQ = 16, 256
DEC_BATCH, DEC_SEQ = 4, 2048
PAST_LEN = 256
GRID_W = 64
D_FF = 5504
N_MOD = 9
GLA_HEADS, GLA_DK, GLA_DV = 4, 256, 512
GLA_RANK = 16
GLA_TAU = 16.0
GLA_CHUNK = 64
GLA_BLOCK = 256
GLA_CHUNKS_PER_ITER = 8
MLA_HEADS, MLA_NOPE, MLA_ROPE, MLA_V = 16, 128, 64, 128
Q_LORA = KV_LORA = 512
ROPE_THETA = 10000.0
NORM_EPS = 1e-6

N_CTX = BATCH * SEQ
N_LAT = DEC_BATCH * DEC_SEQ
N_TOK = N_CTX + N_LAT

LANES = 128
V7X_VMEM_BYTES = 64 * 1024 * 1024
V7X_VMEM_LIMIT_BYTES = 60000 * 1024

FF_TILE = 512
FF_PAD = -(-D_FF // FF_TILE) * FF_TILE
N_FF = FF_PAD // FF_TILE
FF_LAST = D_FF - (N_FF - 1) * FF_TILE
FFN_TM = 1024
FFN_ROWS = 512
PROJ_TM, PROJ_TN = 1024, 1024
MLA_TM = 1024
MERGE_TM = 256
ATT_TQ = 512
HEAD_W = 2 * LANES

COL_Q, COL_K, COL_V, COL_R = 0, 1024, 2048, 4096
COL_GA, COL_GB, COL_CQ, COL_CKV = 6144, 8192, 10240, 10752
PROJ_MAIN = 11264
PROJ_SMALL = 256


class _Stream(NamedTuple):
    name: str
    n_batch: int
    seq: int
    mod_base: int
    per_batch_mod: bool
    rope: bool

    @property
    def n_tok(self):
        return self.n_batch * self.seq

    def mod_row(self, tile, tm):
        return self.mod_base + (tile * tm // self.seq if self.per_batch_mod else 0 * tile)

    def rope_row(self, tile, tm):
        return 1 + tile % (self.seq // tm) if self.rope else 0 * tile


_CTX = _Stream("ctx", BATCH, SEQ, 0, False, False)
_LAT = _Stream("lat", DEC_BATCH, DEC_SEQ, 1, True, True)


def _rms(x, gain):
    ms = jnp.mean(x * x, axis=-1, keepdims=True)
    return x * lax.rsqrt(ms + NORM_EPS) * gain


def _silu(x):
    return x * jax.nn.sigmoid(x)


_NT = (((1,), (1,)), ((), ()))
_TN = (((0,), (0,)), ((), ()))


def _cparams(n_axes, vmem_limit_bytes=V7X_VMEM_LIMIT_BYTES):
    return pltpu.CompilerParams(dimension_semantics=("arbitrary",) * n_axes, vmem_limit_bytes=vmem_limit_bytes)


def _resident(shape):
    nd = len(shape)
    return pl.BlockSpec(shape, lambda *_: (0,) * nd, pipeline_mode=pl.Buffered(1))


def _mod_kernel(c_ref, w_ref, b_ref, o_ref):
    s = _silu(c_ref[...]).astype(BF16)
    o_ref[...] = jnp.dot(s, w_ref[...].astype(BF16), preferred_element_type=F32) + b_ref[...]


def _modulation(c_all, w_ada, b_ada):
    n = w_ada.shape[1]
    tn = 1024
    return pl.pallas_call(
        _mod_kernel,
        grid=(n // tn,),
        in_specs=[pl.BlockSpec((8, D_MODEL), lambda j: (0, 0)),
                  pl.BlockSpec((D_MODEL, tn), lambda j: (0, j)),
                  pl.BlockSpec((1, tn), lambda j: (0, j))],
        out_specs=pl.BlockSpec((8, tn), lambda j: (0, j)),
        out_shape=jax.ShapeDtypeStruct((8, n), F32),
        compiler_params=_cparams(1),
        name="adaln_mod",
    )(c_all, w_ada, b_ada.reshape(1, n))


def _row(ref, r):
    return ref[r:r + 1, :]


def _norm_mod(x_ref, h_ref, g_ref, mod_ref, sub):
    h = _rms(x_ref[...], _row(g_ref, 2 * sub)) * (1.0 + _row(mod_ref, 3 * sub + 1)) + _row(mod_ref, 3 * sub)
    h_ref[...] = h.astype(BF16)


def _residual(x_ref, o_ref, g_ref, mod_ref, sub, gate_scale):
    y = _rms(o_ref[...], _row(g_ref, 2 * sub + 1))
    o_ref[...] = x_ref[...] + (gate_scale * _row(mod_ref, 3 * sub + 2)) * y


def _ffn_kernel(x_ref, mod_ref, g_ref, wgu_ref, wo_ref, o_ref, h_ref, *, sub):
    j = pl.program_id(1)

    @pl.when(j == 0)
    def _():
        _norm_mod(x_ref, h_ref, g_ref, mod_ref, sub)
        o_ref[...] = jnp.zeros_like(o_ref)

    def chunk(width):
        for r in range(0, h_ref.shape[0], FFN_ROWS):
            rows = slice(r, r + FFN_ROWS)
            gu = jnp.dot(h_ref[rows, :], wgu_ref[:, :2 * width], preferred_element_type=F32)
            a = (_silu(gu[:, :width]) * gu[:, width:]).astype(BF16)
            o_ref[rows, :] += jnp.dot(a, wo_ref[:width, :], preferred_element_type=F32)

    last = pl.num_programs(1) - 1

    @pl.when(j < last)
    def _():
        chunk(FF_TILE)

    @pl.when(j == last)
    def _():
        chunk(FF_LAST)
        _residual(x_ref, o_ref, g_ref, mod_ref, sub, 0.5)


def _ffn(x, stream, mod, gains, w_gu, w_o, sub):
    tm = FFN_TM
    return pl.pallas_call(
        functools.partial(_ffn_kernel, sub=sub),
        grid=(stream.n_tok // tm, N_FF),
        in_specs=[pl.BlockSpec((tm, D_MODEL), lambda i, j: (i, 0)),
                  pl.BlockSpec((None, N_MOD, D_MODEL), lambda i, j: (stream.mod_row(i, tm), 0, 0)),
                  pl.BlockSpec((6, D_MODEL), lambda i, j: (0, 0)),
                  pl.BlockSpec((D_MODEL, 2 * FF_TILE), lambda i, j: (0, j)),
                  pl.BlockSpec((FF_TILE, D_MODEL), lambda i, j: (j, 0))],
        out_specs=pl.BlockSpec((tm, D_MODEL), lambda i, j: (i, 0)),
        out_shape=jax.ShapeDtypeStruct((stream.n_tok, D_MODEL), F32),
        scratch_shapes=[pltpu.VMEM((tm, D_MODEL), BF16)],
        compiler_params=_cparams(2, V7X_VMEM_BYTES - 2 * 1024 * 1024),
        name=f"ffn{sub}_{stream.name}",
    )(x, mod, gains, w_gu, w_o)


def _proj_kernel(x_ref, mod_ref, g_ref, wt_ref, wst_ref, side_ref, o_ref, os_ref, side_o_ref, h_ref, *, n_side):
    i, j = pl.program_id(0), pl.program_id(1)

    @pl.when(j == 0)
    def _():
        _norm_mod(x_ref, h_ref, g_ref, mod_ref, 1)
        os_ref[...] = lax.dot_general(h_ref[...], wst_ref[...], _NT, preferred_element_type=F32)

    o_ref[...] = lax.dot_general(h_ref[...], wt_ref[...], _NT, preferred_element_type=F32)
    step = i * pl.num_programs(1) + j
    side_o_ref[...] = jnp.where(step < n_side, side_ref[...], 0.0).astype(BF16)


def _ffn_in_block_dst(s):
    per_chunk = FF_TILE // LANES
    is_up = (s >= _FF_BLOCKS).astype(jnp.int32)
    b = s - is_up * _FF_BLOCKS
    c, q = b // per_chunk, b % per_chunk
    full = c * 2 * per_chunk + is_up * per_chunk + q
    ragged = (N_FF - 1) * 2 * per_chunk + is_up * (FF_LAST // LANES) + q
    return jnp.where(s >= 2 * _FF_BLOCKS, s, jnp.where(c < N_FF - 1, full, ragged))


def _mixer_proj(x, stream, mod, gains, w_main, w_small, side_w, side_kind):
    tm, tn = PROJ_TM, PROJ_TN
    n_tok = stream.n_tok
    n_j = PROJ_MAIN // tn
    n_steps = n_tok // tm * n_j
    step = lambda i, j: i * n_j + j
    if side_kind == "ffn_in":
        n_side, n_dst = 2 * _FF_BLOCKS, N_FF * 2 * FF_TILE // LANES
        side_in = pl.BlockSpec((D_MODEL, LANES), lambda i, j: (0, jnp.minimum(step(i, j), n_side - 1)))
        side_out = pl.BlockSpec((D_MODEL, LANES), lambda i, j: (0, _ffn_in_block_dst(step(i, j))))
        side_shape = jax.ShapeDtypeStruct((D_MODEL, N_FF * 2 * FF_TILE), BF16)
    else:
        n_side, n_dst = _FF_BLOCKS, FF_PAD // LANES
        side_in = pl.BlockSpec((LANES, D_MODEL), lambda i, j: (jnp.minimum(step(i, j), n_side - 1), 0))
        side_out = pl.BlockSpec((LANES, D_MODEL), lambda i, j: (step(i, j), 0))
        side_shape = jax.ShapeDtypeStruct((FF_PAD, D_MODEL), BF16)
    assert n_steps == n_dst, "the side cast needs exactly one destination block per grid step"
    return pl.pallas_call(
        functools.partial(_proj_kernel, n_side=n_side),
        grid=(n_tok // tm, n_j),
        in_specs=[pl.BlockSpec((tm, D_MODEL), lambda i, j: (i, 0)),
                  pl.BlockSpec((None, N_MOD, D_MODEL), lambda i, j: (stream.mod_row(i, tm), 0, 0)),
                  pl.BlockSpec((6, D_MODEL), lambda i, j: (0, 0)),
                  pl.BlockSpec((tn, D_MODEL), lambda i, j: (j, 0)),
                  pl.BlockSpec((PROJ_SMALL, D_MODEL), lambda i, j: (0, 0)),
                  side_in],
        out_specs=[pl.BlockSpec((tm, tn), lambda i, j: (i, j)),
                   pl.BlockSpec((tm, PROJ_SMALL), lambda i, j: (i, 0)),
                   side_out],
        out_shape=[jax.ShapeDtypeStruct((n_tok, PROJ_MAIN), F32),
                   jax.ShapeDtypeStruct((n_tok, PROJ_SMALL), F32),
                   side_shape],
        scratch_shapes=[pltpu.VMEM((tm, D_MODEL), BF16)],
        compiler_params=_cparams(2),
        name=f"mixer_proj_{stream.name}",
    )(x, mod, gains, w_main, w_small, side_w)


def _log_sigmoid(x):
    return jnp.minimum(x, 0.0) - jnp.log(1.0 + jnp.exp(-jnp.abs(x)))


def _gla_kernel(*refs, seq, has_s0, emit_state):
    q_ref, k_ref, v_ref, r_ref, a_ref, wa_ref, ba_ref, gn_ref = refs[:8]
    pos = 8
    if has_s0:
        s0_refs = refs[pos:pos + 2]
        pos += 2
    o_ref = refs[pos]
    pos += 1
    if emit_state:
        s_out_refs = refs[pos:pos + 2]
        pos += 2
    qd_scr, ke_scr, vb_scr, dec_scr, o_scr, st_scr = refs[pos:pos + 6]

    c, blk_rows = GLA_CHUNK, GLA_BLOCK
    cpb = blk_rows // c
    nc = seq // c
    for d in range(2):
        st_scr[d] = s0_refs[d][...] if has_s0 else jnp.zeros((GLA_DK, GLA_DV), F32)

    row = lax.broadcasted_iota(jnp.int32, (blk_rows, blk_rows), 0)
    col = lax.broadcasted_iota(jnp.int32, (blk_rows, blk_rows), 1)
    same_chunk = (row // c) == (col // c)
    masks = (same_chunk & (col <= row), same_chunk & (col >= row))

    tri = [jnp.where(masks[d], 1.0, 0.0).astype(BF16) for d in range(2)]
    n_blocks = seq // blk_rows
    blocks_per_iter = 2 if n_blocks % 2 == 0 else 1

    def bulk(it, carry):
        blks = [it * blocks_per_iter + j for j in range(blocks_per_iter)]
        rows = [pl.ds(pl.multiple_of(blk * blk_rows, blk_rows), blk_rows) for blk in blks]
        nb = range(blocks_per_iter)
        a_in = [a_ref[rows[j], :].astype(BF16) for j in nb]
        q = [q_ref[rows[j], :] * (GLA_DK ** -0.5) for j in nb]
        k = [k_ref[rows[j], :] for j in nb]
        vb = [v_ref[rows[j], :].astype(BF16) for j in nb]
        units = [(j, d) for j in nb for d in range(2)]
        x = {(j, d): jnp.dot(a_in[j], wa_ref[d], preferred_element_type=F32) + ba_ref[d:d + 1, :] for j, d in units}
        la = {u: _log_sigmoid(x[u]) * (1.0 / GLA_TAU) for u in units}
        la1 = {u: la[u].astype(BF16) for u in units}
        rem = {u: la[u] - la1[u].astype(F32) for u in units}
        la2 = {u: rem[u].astype(BF16) for u in units}
        la3 = {u: (rem[u] - la2[u].astype(F32)).astype(BF16) for u in units}
        b = {(j, d): jnp.dot(tri[d], la1[j, d], preferred_element_type=F32)
             + jnp.dot(tri[d], la2[j, d], preferred_element_type=F32)
             + jnp.dot(tri[d], la3[j, d], preferred_element_type=F32) for j, d in units}
        b3 = {u: b[u].reshape(cpb, c, GLA_DK) for u in units}
        b_last = {(j, d): b3[j, d][:, c - 1:c, :] if d == 0 else b3[j, d][:, 0:1, :] for j, d in units}
        dec = {u: jnp.exp(b_last[u]) for u in units}
        dec_rows = {u: jnp.broadcast_to(dec[u], (cpb, c, GLA_DK)).reshape(blk_rows, GLA_DK) for u in units}
        q_dec = {(j, d): (q[j] * jnp.exp(b[j, d])).astype(BF16) for j, d in units}
        k_inv = {(j, d): k[j] * jnp.exp(-b[j, d]) for j, d in units}
        k_end = {u: (k_inv[u] * dec_rows[u]).astype(BF16) for u in units}
        a = {u: lax.dot_general(q_dec[u], k_inv[u].astype(BF16), _NT, preferred_element_type=F32) for u in units}
        a = {(j, d): jnp.where(masks[d], a[j, d], 0.0).astype(BF16) for j, d in units}
        o = {(j, d): jnp.dot(a[j, d], vb[j], preferred_element_type=F32) for j, d in units}
        for j in nb:
            vb_scr[rows[j], :] = vb[j]
        for j, d in units:
            o_scr[d, rows[j], :] = o[j, d]
            qd_scr[d, rows[j], :] = q_dec[j, d]
            ke_scr[d, rows[j], :] = k_end[j, d]
            dec8 = pl.ds(pl.multiple_of(blks[j] * (cpb * 8), cpb * 8), cpb * 8)
            dec_scr[d, dec8, :] = jnp.broadcast_to(dec[j, d], (cpb, 8, GLA_DK)).reshape(cpb * 8, GLA_DK)
        return carry

    lax.fori_loop(0, n_blocks // blocks_per_iter, bulk, 0)

    cpi = min(GLA_CHUNKS_PER_ITER, nc)

    def recur(i, carry):
        steps = [(u, d) for u in range(cpi) for d in range(2)]
        chunk = {(u, d): cpi * i + u if d == 0 else nc - 1 - (cpi * i + u) for u, d in steps}
        rows = {s: pl.ds(pl.multiple_of(chunk[s] * c, c), c) for s in steps}
        dec_rows = {s: pl.ds(pl.multiple_of(chunk[s] * 8, 8), 8) for s in steps}
        kv = {(u, d): lax.dot_general(ke_scr[d, rows[u, d], :], vb_scr[rows[u, d], :], _TN,
                                      preferred_element_type=F32) for u, d in steps}
        dec_col = {s: jnp.broadcast_to(dec_scr[s[1], dec_rows[s], :][0:1, :], (LANES, GLA_DK)).T for s in steps}
        dec_full = {s: jnp.concatenate([dec_col[s]] * (GLA_DV // LANES), axis=1) for s in steps}
        s_t = [st_scr[d] for d in range(2)]
        for u, d in steps:
            inter = jnp.dot(qd_scr[d, rows[u, d], :], s_t[d].astype(BF16), preferred_element_type=F32)
            o_scr[d, rows[u, d], :] += inter
            s_t[d] = s_t[d] * dec_full[u, d] + kv[u, d]
        for d in range(2):
            st_scr[d] = s_t[d]
        return carry

    lax.fori_loop(0, nc // cpi, recur, 0)

    def finish(i, carry):
        rows = pl.ds(pl.multiple_of(i * blk_rows, blk_rows), blk_rows)
        o = _rms(o_scr[0, rows, :] + o_scr[1, rows, :], gn_ref[...])
        o_ref[rows, :] = (o * _silu(r_ref[rows, :])).astype(BF16)
        return carry

    lax.fori_loop(0, seq // blk_rows, finish, 0)
    if emit_state:
        for d in range(2):
            s_out_refs[d][...] = st_scr[d]


def _gla(proj, small, wa, ba, gn, *, seq, n_batch, row0, s0=None, emit_state=False):
    rb = row0 // seq
    dk, dv = GLA_DK, GLA_DV
    in_specs = [pl.BlockSpec((seq, dk), lambda b, h: (rb + b, COL_Q // dk + h)),
                pl.BlockSpec((seq, dk), lambda b, h: (rb + b, COL_K // dk + h)),
                pl.BlockSpec((seq, dv), lambda b, h: (rb + b, COL_V // dv + h)),
                pl.BlockSpec((seq, dv), lambda b, h: (rb + b, COL_R // dv + h)),
                pl.BlockSpec((seq, LANES), lambda b, h: (rb + b, 1)),
                pl.BlockSpec((2, LANES, dk), lambda b, h: (0, 0, h)),
                pl.BlockSpec((2, dk), lambda b, h: (0, h)),
                pl.BlockSpec((1, dv), lambda b, h: (0, 0))]
    args = [proj, proj, proj, proj, small, wa, ba, gn]
    state_spec = pl.BlockSpec((None, None, dk, dv), lambda b, h: (b, h, 0, 0))
    if s0 is not None:
        in_specs += [state_spec, state_spec]
        args += list(s0)
    out_specs = [pl.BlockSpec((seq, dv), lambda b, h: (b, h))]
    out_shape = [jax.ShapeDtypeStruct((n_batch * seq, GLA_HEADS * dv), BF16)]
    if emit_state:
        out_specs += [state_spec, state_spec]
        out_shape += [jax.ShapeDtypeStruct((n_batch, GLA_HEADS, dk, dv), F32)] * 2
    return pl.pallas_call(
        functools.partial(_gla_kernel, seq=seq, has_s0=s0 is not None, emit_state=emit_state),
        grid=(n_batch, GLA_HEADS),
        in_specs=in_specs, out_specs=out_specs, out_shape=out_shape,
        scratch_shapes=[pltpu.VMEM((2, seq, dk), BF16),
                        pltpu.VMEM((2, seq, dk), BF16),
                        pltpu.VMEM((seq, dv), BF16),
                        pltpu.VMEM((2, seq // GLA_CHUNK * 8, dk), F32),
                        pltpu.VMEM((2, seq, dv), F32),
                        pltpu.VMEM((2, dk, dv), F32)],
        compiler_params=_cparams(2),
        name=f"gla_{seq}",
    )(*args)


def _rope_block(x, cs_ref):
    return x * cs_ref[:, :LANES] + pltpu.roll(x, MLA_ROPE, axis=1) * cs_ref[:, LANES:]


def _q_kernel(cq_ref, qn_ref, w_ref, cs_ref, o_ref):
    n = _rms(cq_ref[...], qn_ref[...]).astype(BF16)
    for h in range(MLA_HEADS):
        q = jnp.dot(n, w_ref[:, h * HEAD_W:(h + 1) * HEAD_W], preferred_element_type=F32)
        o_ref[:, h * HEAD_W:h * HEAD_W + LANES] = q[:, :LANES].astype(BF16)
        o_ref[:, h * HEAD_W + LANES:(h + 1) * HEAD_W] = _rope_block(q[:, LANES:], cs_ref).astype(BF16)


def _q_proj(proj, stream, q_norm, w_q, rope_tab):
    tm = MLA_TM
    return pl.pallas_call(
        _q_kernel,
        grid=(stream.n_tok // tm,),
        in_specs=[pl.BlockSpec((tm, Q_LORA), lambda i: (i, COL_CQ // Q_LORA)),
                  _resident((1, Q_LORA)),
                  _resident((Q_LORA, MLA_HEADS * HEAD_W)),
                  pl.BlockSpec((tm, 2 * LANES), lambda i: (stream.rope_row(i, tm), 0))],
        out_specs=pl.BlockSpec((tm, MLA_HEADS * HEAD_W), lambda i: (i, 0)),
        out_shape=jax.ShapeDtypeStruct((stream.n_tok, MLA_HEADS * HEAD_W), BF16),
        compiler_params=_cparams(1),
        name=f"mla_q_{stream.name}",
    )(proj, q_norm, w_q, rope_tab)


def _kv_kernel(*refs, norm, emit_ckv):
    ckv_ref, kn_ref, kr_ref, w_ref, cs_ref, k_ref, v_ref = refs[:7]
    c = ckv_ref[...]
    if norm:
        c = _rms(c, kn_ref[...])
    if emit_ckv:
        refs[7][...] = c
    cb = c.astype(BF16)
    nk = MLA_HEADS * MLA_NOPE
    kr = _rope_block(kr_ref[...], cs_ref).astype(BF16)
    v_ref[...] = jnp.dot(cb, w_ref[:, nk:], preferred_element_type=F32).astype(BF16)
    for h0 in range(0, MLA_HEADS, 2):
        kn = jnp.dot(cb, w_ref[:, h0 * MLA_NOPE:(h0 + 2) * MLA_NOPE], preferred_element_type=F32).astype(BF16)
        for h in (h0, h0 + 1):
            k_ref[:, h * HEAD_W:h * HEAD_W + LANES] = kn[:, (h - h0) * MLA_NOPE:(h - h0 + 1) * MLA_NOPE]
            k_ref[:, h * HEAD_W + LANES:(h + 1) * HEAD_W] = kr


def _kv_proj(ckv_src, ckv_col, kr_src, kv_norm, w_kv, rope_tab, *, n_rows, row0, rope_row, norm, emit_ckv):
    tm = MLA_TM
    r0 = row0 // tm
    out_specs = [pl.BlockSpec((tm, MLA_HEADS * HEAD_W), lambda i: (i, 0)),
                 pl.BlockSpec((tm, MLA_HEADS * MLA_V), lambda i: (i, 0))]
    out_shape = [jax.ShapeDtypeStruct((n_rows, MLA_HEADS * HEAD_W), BF16),
                 jax.ShapeDtypeStruct((n_rows, MLA_HEADS * MLA_V), BF16)]
    if emit_ckv:
        out_specs.append(pl.BlockSpec((tm, KV_LORA), lambda i: (i, 0)))
        out_shape.append(jax.ShapeDtypeStruct((n_rows, KV_LORA), F32))
    return pl.pallas_call(
        functools.partial(_kv_kernel, norm=norm, emit_ckv=emit_ckv),
        grid=(n_rows // tm,),
        in_specs=[pl.BlockSpec((tm, KV_LORA), lambda i: (r0 + i, ckv_col)),
                  _resident((1, KV_LORA)),
                  pl.BlockSpec((tm, LANES), lambda i: (r0 + i, 0)),
                  _resident((KV_LORA, MLA_HEADS * (MLA_NOPE + MLA_V))),
                  pl.BlockSpec((tm, 2 * LANES), lambda i: (rope_row(i), 0))],
        out_specs=out_specs, out_shape=out_shape,
        compiler_params=_cparams(1),
        name=f"mla_kv_{row0}_{n_rows}",
    )(ckv_src, kv_norm, kr_src, w_kv, rope_tab)


_EXP2_SCALE = float((MLA_NOPE + MLA_ROPE) ** -0.5 * np.log2(np.e))


def _attn_kernel(*refs, n_seg, heads, tq, lookahead):
    q_ref = refs[0]
    kv = refs[1:1 + 2 * n_seg]
    o_ref = refs[1 + 2 * n_seg]
    nt = q_ref.shape[0] // tq
    work = [(h, t) for h in range(heads) for t in range(nt)]

    def scores(h, t):
        hs = slice(h * HEAD_W, (h + 1) * HEAD_W)
        q = q_ref[t * tq:(t + 1) * tq, hs]
        return [lax.dot_general(q, kv[2 * g][:, hs], _NT, preferred_element_type=F32) for g in range(n_seg)]

    def row_max(s):
        m = s[0].max(axis=-1, keepdims=True)
        for g in range(1, n_seg):
            m = jnp.maximum(m, s[g].max(axis=-1, keepdims=True))
        return m

    def probs(s, m):
        return [jnp.exp2((s[g] - m) * _EXP2_SCALE).astype(BF16) for g in range(n_seg)]

    def values(g, h):
        v = kv[2 * g + 1][:, h * MLA_V:(h + 1) * MLA_V]
        one_col = jnp.where(lax.broadcasted_iota(jnp.int32, v.shape, 1) == 0, 1.0, 0.0).astype(BF16)
        return jnp.concatenate([v, one_col], axis=1)

    v_aug = {(g, h): values(g, h) for g in range(n_seg) for h in range(heads)}

    def weighted(p, h):
        acc = 0.0
        for g in range(n_seg):
            acc = acc + jnp.dot(p[g], v_aug[g, h], preferred_element_type=F32)
        return acc

    def store(acc, h, t):
        out = acc[:, :MLA_V] / acc[:, MLA_V:MLA_V + 1]
        o_ref[t * tq:(t + 1) * tq, h * MLA_V:(h + 1) * MLA_V] = out.astype(BF16)

    if lookahead >= len(work):
        s = [scores(*w) for w in work]
        m = [row_max(x) for x in s]
        p = [probs(x, y) for x, y in zip(s, m)]
        acc = [weighted(x, w[0]) for x, w in zip(p, work)]
        for x, w in zip(acc, work):
            store(x, *w)
    else:
        s_next = scores(*work[0])
        for idx, (h, t) in enumerate(work):
            s = s_next
            if idx + 1 < len(work):
                s_next = scores(*work[idx + 1])
            store(weighted(probs(s, row_max(s)), h), h, t)


def _attention(q, q_row0, segs, *, n_batch, seq, tq, heads):
    qb0 = q_row0 // seq
    n_hblk = MLA_HEADS // heads
    in_specs = [pl.BlockSpec((seq, heads * HEAD_W), lambda b, h: (qb0 + b, h))]
    args = [q]
    for keys, vals, n_keys in segs:
        in_specs += [pl.BlockSpec((n_keys, heads * HEAD_W), lambda b, h: (b, h)),
                     pl.BlockSpec((n_keys, heads * MLA_V), lambda b, h: (b, h))]
        args += [keys, vals]
    return pl.pallas_call(
        functools.partial(_attn_kernel, n_seg=len(segs), heads=heads, tq=tq,
                          lookahead=1 if seq > tq else heads),
        grid=(n_batch, n_hblk),
        in_specs=in_specs,
        out_specs=pl.BlockSpec((seq, heads * MLA_V), lambda b, h: (b, h)),
        out_shape=jax.ShapeDtypeStruct((n_batch * seq, MLA_HEADS * MLA_V), BF16),
        compiler_params=_cparams(2),
        name=f"mla_attn_{seq}",
    )(*args)


def _merge_kernel(og_ref, om_ref, ga_ref, gb_ref, x_ref, mod_ref, g_ref, wg_ref, wm_ref, wo_ref, o_ref):
    yg = jnp.dot(og_ref[...], wg_ref[...], preferred_element_type=F32)
    ym = jnp.dot(om_ref[...], wm_ref[...], preferred_element_type=F32)
    m = (jax.nn.sigmoid(ga_ref[...]) * yg + jax.nn.sigmoid(gb_ref[...]) * ym).astype(BF16)
    o_ref[...] = jnp.dot(m, wo_ref[...], preferred_element_type=F32)
    _residual(x_ref, o_ref, g_ref, mod_ref, 1, 1.0)


def _merge(o_gla, o_mla, proj, x, stream, mod, gains, w_gla_out, w_mla_out, w_out):
    tm = MERGE_TM
    tile = lambda c0: pl.BlockSpec((tm, D_MODEL), lambda i: (i, c0 // D_MODEL))
    return pl.pallas_call(
        _merge_kernel,
        grid=(stream.n_tok // tm,),
        in_specs=[tile(0), tile(0), tile(COL_GA), tile(COL_GB), tile(0),
                  pl.BlockSpec((None, N_MOD, D_MODEL), lambda i: (stream.mod_row(i, tm), 0, 0)),
                  _resident((6, D_MODEL)),
                  _resident((D_MODEL, D_MODEL)), _resident((D_MODEL, D_MODEL)), _resident((D_MODEL, D_MODEL))],
        out_specs=tile(0),
        out_shape=jax.ShapeDtypeStruct((stream.n_tok, D_MODEL), F32),
        compiler_params=_cparams(1),
        name=f"mixer_merge_{stream.name}",
    )(o_gla, o_mla, proj, proj, x, mod, gains, w_gla_out, w_mla_out, w_out)


_SWAP = np.arange(MLA_ROPE)
_SWAP = np.where(_SWAP % 32 < 16, _SWAP + 16, _SWAP - 16)


_FF_BLOCKS = D_FF // LANES
_U_PER_STEP = FF_TILE // LANES


def _prep_ffn_in_kernel(g_ref, *refs):
    u_refs, o_ref = refs[:-1], refs[-1]
    j = pl.program_id(0)

    @pl.when(j < N_FF - 1)
    def _():
        o_ref[:, :FF_TILE] = g_ref[...].astype(BF16)
        for q, u_ref in enumerate(u_refs):
            o_ref[:, FF_TILE + q * LANES:FF_TILE + (q + 1) * LANES] = u_ref[...].astype(BF16)

    @pl.when(j == N_FF - 1)
    def _():
        o_ref[:, :FF_LAST] = g_ref[:, :FF_LAST].astype(BF16)
        for q in range(FF_LAST // LANES):
            o_ref[:, FF_LAST + q * LANES:FF_LAST + (q + 1) * LANES] = u_refs[q][...].astype(BF16)
        o_ref[:, 2 * FF_LAST:] = jnp.zeros((D_MODEL, 2 * (FF_TILE - FF_LAST)), BF16)


def _prep_ffn_in(w, name):
    last = 2 * _FF_BLOCKS - 1
    u_spec = lambda q: pl.BlockSpec((D_MODEL, LANES),
                                    lambda j: (0, jnp.minimum(_FF_BLOCKS + _U_PER_STEP * j + q, last)))
    return pl.pallas_call(
        _prep_ffn_in_kernel,
        grid=(N_FF,),
        in_specs=[pl.BlockSpec((D_MODEL, FF_TILE), lambda j: (0, j))] + [u_spec(q) for q in range(_U_PER_STEP)],
        out_specs=pl.BlockSpec((D_MODEL, 2 * FF_TILE), lambda j: (0, j)),
        out_shape=jax.ShapeDtypeStruct((D_MODEL, N_FF * 2 * FF_TILE), BF16),
        compiler_params=_cparams(1),
        name=name,
    )(w, *([w] * _U_PER_STEP))


_FF_FULL = D_FF // FF_TILE
_FF_TAIL = (D_FF - _FF_FULL * FF_TILE) // LANES


def _prep_ffn_out_kernel(w_ref, *refs):
    tail_refs, o_ref = refs[:-1], refs[-1]
    j = pl.program_id(0)

    @pl.when(j < _FF_FULL)
    def _():
        o_ref[...] = w_ref[...].astype(BF16)

    @pl.when(j == _FF_FULL)
    def _():
        for t, t_ref in enumerate(tail_refs):
            o_ref[t * LANES:(t + 1) * LANES, :] = t_ref[...].astype(BF16)
        o_ref[_FF_TAIL * LANES:, :] = jnp.zeros((FF_TILE - _FF_TAIL * LANES, D_MODEL), BF16)


def _prep_ffn_out(w, name):
    tail = lambda t: pl.BlockSpec((LANES, D_MODEL), lambda j: (_FF_FULL * (FF_TILE // LANES) + t, 0))
    return pl.pallas_call(
        _prep_ffn_out_kernel,
        grid=(N_FF,),
        in_specs=[pl.BlockSpec((FF_TILE, D_MODEL), lambda j: (jnp.minimum(j, _FF_FULL - 1), 0))]
        + [tail(t) for t in range(_FF_TAIL)],
        out_specs=pl.BlockSpec((FF_TILE, D_MODEL), lambda j: (j, 0)),
        out_shape=jax.ShapeDtypeStruct((FF_PAD, D_MODEL), BF16),
        compiler_params=_cparams(1),
        name=name,
    )(w, *([w] * _FF_TAIL))


_W_IN_GROUPS = ((0, 6144), (7264, 2048), (9312, 2048), (6176, 512), (6688, 512))
_W_IN_GROUPS_T = ((0, 6144), (7264, 2048), (9312, 2048), (6176, 1024))
_W_IN_ROW_UNIT = 32


def _prep_w_in_kernel(start_ref, w_ref, o_ref):
    del start_ref
    o_ref[...] = w_ref[...].astype(BF16)


def _prep_w_in(w):
    wt = jnp.swapaxes(w, 0, 1)
    tn = PROJ_TN
    starts = np.array([(s + o) // _W_IN_ROW_UNIT for s, n in _W_IN_GROUPS_T for o in range(0, n, tn)], np.int32)
    main = pl.pallas_call(
        _prep_w_in_kernel,
        grid_spec=pltpu.PrefetchScalarGridSpec(
            num_scalar_prefetch=1,
            grid=(PROJ_MAIN // tn,),
            in_specs=[pl.BlockSpec((pl.Element(tn), pl.Element(D_MODEL)),
                                   lambda j, start: (start[j] * _W_IN_ROW_UNIT, 0))],
            out_specs=pl.BlockSpec((tn, D_MODEL), lambda j, start: (j, 0))),
        out_shape=jax.ShapeDtypeStruct((PROJ_MAIN, D_MODEL), BF16),
        compiler_params=_cparams(1),
        name="prep_w_in",
    )(jnp.asarray(starts), wt)
    af, ab, kr = wt[6144:6160], wt[6160:6176], wt[7200:7264]
    small = jnp.concatenate([kr, kr[_SWAP], af, ab,
                             jnp.zeros((PROJ_SMALL - 2 * MLA_ROPE - 2 * GLA_RANK, D_MODEL), F32)], axis=0).astype(BF16)
    return main, small


def _prep_w_uq(w):
    w = w.astype(BF16).reshape(Q_LORA, MLA_HEADS, MLA_NOPE + MLA_ROPE)
    rope = w[..., MLA_NOPE:]
    return jnp.concatenate([w[..., :MLA_NOPE], rope, rope[..., _SWAP]], axis=-1).reshape(Q_LORA, MLA_HEADS * HEAD_W)


def _prep_w_ukv(w):
    w = w.astype(BF16).reshape(KV_LORA, MLA_HEADS, MLA_NOPE + MLA_V)
    return jnp.concatenate([w[..., :MLA_NOPE].reshape(KV_LORA, -1), w[..., MLA_NOPE:].reshape(KV_LORA, -1)], axis=1)


def _prep_w_alpha(w):
    out = jnp.zeros((2, LANES, GLA_HEADS * GLA_DK), BF16)
    out = out.at[0, 0:GLA_RANK].set(w[0].astype(BF16))
    return out.at[1, GLA_RANK:2 * GLA_RANK].set(w[1].astype(BF16))


def _rope_table(tm):
    rows = DEC_SEQ // GRID_W
    row = np.repeat(np.arange(rows), GRID_W).astype(np.float64)
    col = np.tile(np.arange(GRID_W), rows).astype(np.float64)
    half = MLA_ROPE // 2
    inv_freq = ROPE_THETA ** (-np.arange(0, half, 2, dtype=np.float64) / half)
    ang_r, ang_c = row[:, None] * inv_freq, col[:, None] * inv_freq
    cos = np.concatenate([np.cos(ang_r), np.cos(ang_r), np.cos(ang_c), np.cos(ang_c)], axis=1)
    sin = np.concatenate([-np.sin(ang_r), np.sin(ang_r), -np.sin(ang_c), np.sin(ang_c)], axis=1)
    z = np.zeros((DEC_SEQ, LANES - MLA_ROPE))
    lat = np.concatenate([cos, z, sin, z], axis=1)
    ident = np.concatenate([np.ones((tm, MLA_ROPE)), np.zeros((tm, 2 * LANES - MLA_ROPE))], axis=1)
    return jnp.asarray(np.concatenate([ident, lat], axis=0), F32)


def kernel(x_prompt, x_sample, cache_ckv, cache_krope, state_gla_fwd, state_gla_bwd, c, c_ctx, w_ada, b_ada, norm_gains, w_ffn1_in, w_ffn1_out, w_ffn2_in, w_ffn2_out, w_in, w_gla_alpha, b_gla_alpha, gla_norm, w_gla_out, q_norm, kv_norm, w_uq, w_ukv, w_mla_out, w_out):
    c_all = jnp.concatenate([c_ctx[None, :], c, jnp.zeros((8 - 1 - DEC_BATCH, D_MODEL), F32)], axis=0)
    mod = _modulation(c_all, w_ada[0], b_ada[0]).reshape(8, N_MOD, D_MODEL)
    gains = norm_gains[0]

    streams = (_CTX, _LAT)
    x = {_CTX: x_prompt.reshape(N_CTX, D_MODEL), _LAT: x_sample.reshape(N_LAT, D_MODEL)}

    w_gu, w_o = _prep_ffn_in(w_ffn1_in[0], "prep_ffn0_in"), _prep_ffn_out(w_ffn1_out[0], "prep_ffn0_out")
    x = {s: _ffn(x[s], s, mod, gains, w_gu, w_o, sub=0) for s in streams}

    w_main, w_small = _prep_w_in(w_in[0])
    proj, small = {}, {}
    proj[_CTX], small[_CTX], w_o2 = _mixer_proj(x[_CTX], _CTX, mod, gains, w_main, w_small, w_ffn2_out[0], "ffn_out")
    proj[_LAT], small[_LAT], w_gu2 = _mixer_proj(x[_LAT], _LAT, mod, gains, w_main, w_small, w_ffn2_in[0], "ffn_in")

    wa = _prep_w_alpha(w_gla_alpha[0])
    ba = b_gla_alpha[0]
    gn = gla_norm[0].reshape(1, GLA_DV)
    o_gla = {}
    o_gla[_CTX], s_f, s_b = _gla(proj[_CTX], small[_CTX], wa, ba, gn, seq=SEQ, n_batch=BATCH, row0=0,
                                 emit_state=True)
    (o_gla[_LAT],) = _gla(proj[_LAT], small[_LAT], wa, ba, gn, seq=DEC_SEQ, n_batch=DEC_BATCH, row0=0,
                          s0=(state_gla_fwd[:, 0], state_gla_bwd[:, 0]))

    tm = MLA_TM
    rope_tab = _rope_table(tm)
    w_q = _prep_w_uq(w_uq[0])
    q = {s: _q_proj(proj[s], s, q_norm[0].reshape(1, Q_LORA), w_q, rope_tab) for s in streams}
    w_kv = _prep_w_ukv(w_ukv[0])
    kvn = kv_norm[0].reshape(1, KV_LORA)
    ckv_col = COL_CKV // KV_LORA
    k_ctx, v_ctx, ckv_new = _kv_proj(proj[_CTX], ckv_col, small[_CTX], kvn, w_kv, rope_tab, n_rows=N_CTX, row0=0,
                                     rope_row=lambda i: _CTX.rope_row(i, tm), norm=True, emit_ckv=True)
    k_lat, v_lat = _kv_proj(proj[_LAT], ckv_col, small[_LAT], kvn, w_kv, rope_tab, n_rows=N_LAT, row0=0,
                            rope_row=lambda i: _LAT.rope_row(i, tm), norm=True, emit_ckv=False)
    n_past = DEC_BATCH * PAST_LEN
    kr_past = jnp.pad(cache_krope[:, 0].reshape(n_past, MLA_ROPE), ((0, 0), (0, LANES - MLA_ROPE)))
    k_past, v_past = _kv_proj(cache_ckv[:, 0].reshape(n_past, KV_LORA), 0, kr_past, kvn, w_kv, rope_tab,
                              n_rows=n_past, row0=0, rope_row=lambda i: 0 * i, norm=False, emit_ckv=False)
    o_mla = {
        _CTX: _attention(q[_CTX], 0, [(k_ctx, v_ctx, SEQ)], n_batch=BATCH, seq=SEQ, tq=SEQ, heads=MLA_HEADS),
        _LAT: _attention(q[_LAT], 0, [(k_past, v_past, PAST_LEN), (k_lat, v_lat, DEC_SEQ)],
                         n_batch=DEC_BATCH, seq=DEC_SEQ, tq=ATT_TQ, heads=1)}

    w_merge = (w_gla_out[0].astype(BF16), w_mla_out[0].astype(BF16), w_out[0].astype(BF16))
    x = {s: _merge(o_gla[s], o_mla[s], proj[s], x[s], s, mod, gains, *w_merge) for s in streams}
    y = {s: _ffn(x[s], s, mod, gains, w_gu2, w_o2, sub=2) for s in streams}

    y_prompt = y[_CTX].reshape(BATCH, SEQ, D_MODEL)
    y_sample = y[_LAT].reshape(DEC_BATCH, DEC_SEQ, D_MODEL)
    new_ckv = ckv_new.reshape(BATCH, 1, SEQ, KV_LORA)
    new_krope = small[_CTX][:, :MLA_ROPE].reshape(BATCH, 1, SEQ, MLA_ROPE)
    return (y_prompt, y_sample, new_ckv, new_krope,
            s_f.reshape(BATCH, 1, GLA_HEADS, GLA_DK, GLA_DV), s_b.reshape(BATCH, 1, GLA_HEADS, GLA_DK, GLA_DV))
```

```python
import functools
from typing import NamedTuple

import numpy as np
import jax
import jax.numpy as jnp
from jax import lax
from jax.experimental import pallas as pl
from jax.experimental.pallas import tpu as pltpu

F32 = jnp.float32
BF16 = jnp.bfloat16

D_MODEL = 2048
BATCH, SEQ = 16, 256
DEC_BATCH, DEC_SEQ = 4, 2048
PAST_LEN = 256
GRID_W = 64
D_FF = 5504
N_MOD = 9
GLA_HEADS, GLA_DK, GLA_DV = 4, 256, 512
GLA_RANK = 16
GLA_TAU = 16.0
GLA_CHUNK = 64
GLA_BLOCK = 256
GLA_BLOCKS_PER_ITER = 4
GLA_CHUNKS_PER_ITER = 8
MLA_HEADS, MLA_NOPE, MLA_ROPE, MLA_V = 16, 128, 64, 128
Q_LORA = KV_LORA = 512
ROPE_THETA = 10000.0
NORM_EPS = 1e-6

N_CTX = BATCH * SEQ
N_LAT = DEC_BATCH * DEC_SEQ
N_TOK = N_CTX + N_LAT

LANES = 128
V7X_VMEM_BYTES = 64 * 1024 * 1024
V7X_VMEM_LIMIT_BYTES = 60000 * 1024

FF_TILE = 512
FF_PAD = -(-D_FF // FF_TILE) * FF_TILE
N_FF = FF_PAD // FF_TILE
FF_LAST = D_FF - (N_FF - 1) * FF_TILE
FFN_TM = 1024
FFN_ROWS = 512
PROJ_TM, PROJ_TN = 1024, 1024
MLA_TM = 1024
MERGE_TM = 256
ATT_TQ = 1024
ATT_HEADS = 1
HEAD_W = 2 * LANES

COL_Q, COL_K, COL_V, COL_R = 0, 1024, 2048, 4096
COL_GA, COL_GB, COL_CQ, COL_CKV = 6144, 8192, 10240, 10752
PROJ_MAIN = 11264
PROJ_SMALL = 256


class _Stream(NamedTuple):
    name: str
    n_batch: int
    seq: int
    mod_base: int
    per_batch_mod: bool
    rope: bool

    @property
    def n_tok(self):
        return self.n_batch * self.seq

    def mod_row(self, tile, tm):
        return self.mod_base + (tile * tm // self.seq if self.per_batch_mod else 0 * tile)

    def rope_row(self, tile, tm):
        return 1 + tile % (self.seq // tm) if self.rope else 0 * tile


_CTX = _Stream("ctx", BATCH, SEQ, 0, False, False)
_LAT = _Stream("lat", DEC_BATCH, DEC_SEQ, 1, True, True)


def _rms(x, gain):
    ms = jnp.mean(x * x, axis=-1, keepdims=True)
    return x * lax.rsqrt(ms + NORM_EPS) * gain


def _silu(x):
    return x * jax.nn.sigmoid(x)


_NT = (((1,), (1,)), ((), ()))
_TN = (((0,), (0,)), ((), ()))


def _cparams(n_axes, vmem_limit_bytes=V7X_VMEM_LIMIT_BYTES):
    return pltpu.CompilerParams(dimension_semantics=("arbitrary",) * n_axes, vmem_limit_bytes=vmem_limit_bytes)


def _resident(shape):
    nd = len(shape)
    return pl.BlockSpec(shape, lambda *_: (0,) * nd, pipeline_mode=pl.Buffered(1))


def _mod_kernel(c_ref, w_ref, b_ref, o_ref):
    s = _silu(c_ref[...]).astype(BF16)
    o_ref[...] = jnp.dot(s, w_ref[...].astype(BF16), preferred_element_type=F32) + b_ref[...]


def _modulation(c_all, w_ada, b_ada):
    n = w_ada.shape[1]
    tn = 1024
    return pl.pallas_call(
        _mod_kernel,
        grid=(n // tn,),
        in_specs=[pl.BlockSpec((8, D_MODEL), lambda j: (0, 0)),
                  pl.BlockSpec((D_MODEL, tn), lambda j: (0, j)),
                  pl.BlockSpec((1, tn), lambda j: (0, j))],
        out_specs=pl.BlockSpec((8, tn), lambda j: (0, j)),
        out_shape=jax.ShapeDtypeStruct((8, n), F32),
        compiler_params=_cparams(1),
        name="adaln_mod",
    )(c_all, w_ada, b_ada.reshape(1, n))


def _row(ref, r):
    return ref[r:r + 1, :]


def _norm_mod(x_ref, h_ref, g_ref, mod_ref, sub):
    h = _rms(x_ref[...], _row(g_ref, 2 * sub)) * (1.0 + _row(mod_ref, 3 * sub + 1)) + _row(mod_ref, 3 * sub)
    h_ref[...] = h.astype(BF16)


def _residual(x_ref, o_ref, g_ref, mod_ref, sub, gate_scale):
    y = _rms(o_ref[...], _row(g_ref, 2 * sub + 1))
    o_ref[...] = x_ref[...] + (gate_scale * _row(mod_ref, 3 * sub + 2)) * y


def _ffn_kernel(x_ref, mod_ref, g_ref, wgu_ref, wo_ref, o_ref, h_ref, *, sub):
    j = pl.program_id(1)

    @pl.when(j == 0)
    def _():
        _norm_mod(x_ref, h_ref, g_ref, mod_ref, sub)
        o_ref[...] = jnp.zeros_like(o_ref)

    def chunk(width):
        for r in range(0, h_ref.shape[0], FFN_ROWS):
            rows = slice(r, r + FFN_ROWS)
            gu = jnp.dot(h_ref[rows, :], wgu_ref[:, :2 * width], preferred_element_type=F32)
            a = (_silu(gu[:, :width]) * gu[:, width:]).astype(BF16)
            o_ref[rows, :] += jnp.dot(a, wo_ref[:width, :], preferred_element_type=F32)

    last = pl.num_programs(1) - 1

    @pl.when(j < last)
    def _():
        chunk(FF_TILE)

    @pl.when(j == last)
    def _():
        chunk(FF_LAST)
        _residual(x_ref, o_ref, g_ref, mod_ref, sub, 0.5)


def _ffn(x, stream, mod, gains, w_gu, w_o, sub):
    tm = FFN_TM
    return pl.pallas_call(
        functools.partial(_ffn_kernel, sub=sub),
        grid=(stream.n_tok // tm, N_FF),
        in_specs=[pl.BlockSpec((tm, D_MODEL), lambda i, j: (i, 0)),
                  pl.BlockSpec((None, N_MOD, D_MODEL), lambda i, j: (stream.mod_row(i, tm), 0, 0)),
                  pl.BlockSpec((6, D_MODEL), lambda i, j: (0, 0)),
                  pl.BlockSpec((D_MODEL, 2 * FF_TILE), lambda i, j: (0, j)),
                  pl.BlockSpec((FF_TILE, D_MODEL), lambda i, j: (j, 0))],
        out_specs=pl.BlockSpec((tm, D_MODEL), lambda i, j: (i, 0)),
        out_shape=jax.ShapeDtypeStruct((stream.n_tok, D_MODEL), F32),
        scratch_shapes=[pltpu.VMEM((tm, D_MODEL), BF16)],
        compiler_params=_cparams(2, V7X_VMEM_BYTES - 2 * 1024 * 1024),
        name=f"ffn{sub}_{stream.name}",
    )(x, mod, gains, w_gu, w_o)


def _proj_kernel(x_ref, mod_ref, g_ref, wt_ref, wst_ref, side_ref, o_ref, os_ref, side_o_ref, h_ref, *, n_side):
    i, j = pl.program_id(0), pl.program_id(1)

    @pl.when(j == 0)
    def _():
        _norm_mod(x_ref, h_ref, g_ref, mod_ref, 1)
        os_ref[...] = lax.dot_general(h_ref[...], wst_ref[...], _NT, preferred_element_type=F32)

    o_ref[...] = lax.dot_general(h_ref[...], wt_ref[...], _NT, preferred_element_type=F32)
    step = i * pl.num_programs(1) + j
    side_o_ref[...] = jnp.where(step < n_side, side_ref[...], 0.0).astype(BF16)


def _ffn_in_block_dst(s):
    per_chunk = FF_TILE // LANES
    is_up = (s >= _FF_BLOCKS).astype(jnp.int32)
    b = s - is_up * _FF_BLOCKS
    c, q = b // per_chunk, b % per_chunk
    full = c * 2 * per_chunk + is_up * per_chunk + q
    ragged = (N_FF - 1) * 2 * per_chunk + is_up * (FF_LAST // LANES) + q
    return jnp.where(s >= 2 * _FF_BLOCKS, s, jnp.where(c < N_FF - 1, full, ragged))


def _mixer_proj(x, stream, mod, gains, w_main, w_small, side_w, side_kind):
    tm, tn = PROJ_TM, PROJ_TN
    n_tok = stream.n_tok
    n_j = PROJ_MAIN // tn
    n_steps = n_tok // tm * n_j
    step = lambda i, j: i * n_j + j
    if side_kind == "ffn_in":
        n_side, n_dst = 2 * _FF_BLOCKS, N_FF * 2 * FF_TILE // LANES
        side_in = pl.BlockSpec((D_MODEL, LANES), lambda i, j: (0, jnp.minimum(step(i, j), n_side - 1)))
        side_out = pl.BlockSpec((D_MODEL, LANES), lambda i, j: (0, _ffn_in_block_dst(step(i, j))))
        side_shape = jax.ShapeDtypeStruct((D_MODEL, N_FF * 2 * FF_TILE), BF16)
    else:
        n_side, n_dst = _FF_BLOCKS, FF_PAD // LANES
        side_in = pl.BlockSpec((LANES, D_MODEL), lambda i, j: (jnp.minimum(step(i, j), n_side - 1), 0))
        side_out = pl.BlockSpec((LANES, D_MODEL), lambda i, j: (step(i, j), 0))
        side_shape = jax.ShapeDtypeStruct((FF_PAD, D_MODEL), BF16)
    assert n_steps == n_dst, "the side cast needs exactly one destination block per grid step"
    return pl.pallas_call(
        functools.partial(_proj_kernel, n_side=n_side),
        grid=(n_tok // tm, n_j),
        in_specs=[pl.BlockSpec((tm, D_MODEL), lambda i, j: (i, 0)),
                  pl.BlockSpec((None, N_MOD, D_MODEL), lambda i, j: (stream.mod_row(i, tm), 0, 0)),
                  pl.BlockSpec((6, D_MODEL), lambda i, j: (0, 0)),
                  pl.BlockSpec((tn, D_MODEL), lambda i, j: (j, 0)),
                  pl.BlockSpec((PROJ_SMALL, D_MODEL), lambda i, j: (0, 0)),
                  side_in],
        out_specs=[pl.BlockSpec((tm, tn), lambda i, j: (i, j)),
                   pl.BlockSpec((tm, PROJ_SMALL), lambda i, j: (i, 0)),
                   side_out],
        out_shape=[jax.ShapeDtypeStruct((n_tok, PROJ_MAIN), F32),
                   jax.ShapeDtypeStruct((n_tok, PROJ_SMALL), F32),
                   side_shape],
        scratch_shapes=[pltpu.VMEM((tm, D_MODEL), BF16)],
        compiler_params=_cparams(2),
        name=f"mixer_proj_{stream.name}",
    )(x, mod, gains, w_main, w_small, side_w)


def _log_sigmoid(x):
    return jnp.minimum(x, 0.0) - jnp.log(1.0 + jnp.exp(-jnp.abs(x)))


def _gla_kernel(*refs, seq, has_s0, emit_state):
    q_ref, k_ref, v_ref, r_ref, a_ref, wa_ref, ba_ref, gn_ref = refs[:8]
    pos = 8
    if has_s0:
        s0_refs = refs[pos:pos + 2]
        pos += 2
    o_ref = refs[pos]
    pos += 1
    if emit_state:
        s_out_refs = refs[pos:pos + 2]
        pos += 2
    qd_scr, ke_scr, vb_scr, dec_scr, o_scr, st_scr = refs[pos:pos + 6]

    c, blk_rows = GLA_CHUNK, GLA_BLOCK
    cpb = blk_rows // c
    nc = seq // c
    for d in range(2):
        st_scr[d] = s0_refs[d][...] if has_s0 else jnp.zeros((GLA_DK, GLA_DV), F32)

    row = lax.broadcasted_iota(jnp.int32, (blk_rows, blk_rows), 0)
    col = lax.broadcasted_iota(jnp.int32, (blk_rows, blk_rows), 1)
    same_chunk = (row // c) == (col // c)
    masks = (same_chunk & (col <= row), same_chunk & (col >= row))

    tri = [jnp.where(masks[d], 1.0, 0.0).astype(BF16) for d in range(2)]
    n_blocks = seq // blk_rows
    blocks_per_iter = min(GLA_BLOCKS_PER_ITER, n_blocks)

    def bulk(it, carry):
        blks = [it * blocks_per_iter + j for j in range(blocks_per_iter)]
        rows = [pl.ds(pl.multiple_of(blk * blk_rows, blk_rows), blk_rows) for blk in blks]
        nb = range(blocks_per_iter)
        a_in = [a_ref[rows[j], :].astype(BF16) for j in nb]
        q = [q_ref[rows[j], :] * (GLA_DK ** -0.5) for j in nb]
        k = [k_ref[rows[j], :] for j in nb]
        vb = [v_ref[rows[j], :].astype(BF16) for j in nb]
        units = [(j, d) for j in nb for d in range(2)]
        x = {(j, d): jnp.dot(a_in[j], wa_ref[d], preferred_element_type=F32) + ba_ref[d:d + 1, :] for j, d in units}
        la = {u: _log_sigmoid(x[u]) * (1.0 / GLA_TAU) for u in units}
        la1 = {u: la[u].astype(BF16) for u in units}
        rem = {u: la[u] - la1[u].astype(F32) for u in units}
        la2 = {u: rem[u].astype(BF16) for u in units}
        la3 = {u: (rem[u] - la2[u].astype(F32)).astype(BF16) for u in units}
        b = {(j, d): jnp.dot(tri[d], la1[j, d], preferred_element_type=F32)
             + jnp.dot(tri[d], la2[j, d], preferred_element_type=F32)
             + jnp.dot(tri[d], la3[j, d], preferred_element_type=F32) for j, d in units}
        b3 = {u: b[u].reshape(cpb, c, GLA_DK) for u in units}
        b_last = {(j, d): b3[j, d][:, c - 1:c, :] if d == 0 else b3[j, d][:, 0:1, :] for j, d in units}
        dec = {u: jnp.exp(b_last[u]) for u in units}
        dec_rows = {u: jnp.broadcast_to(dec[u], (cpb, c, GLA_DK)).reshape(blk_rows, GLA_DK) for u in units}
        q_dec = {(j, d): (q[j] * jnp.exp(b[j, d])).astype(BF16) for j, d in units}
        k_inv = {(j, d): k[j] * jnp.exp(-b[j, d]) for j, d in units}
        k_end = {u: (k_inv[u] * dec_rows[u]).astype(BF16) for u in units}
        a = {u: lax.dot_general(q_dec[u], k_inv[u].astype(BF16), _NT, preferred_element_type=F32) for u in units}
        a = {(j, d): jnp.where(masks[d], a[j, d], 0.0).astype(BF16) for j, d in units}
        o = {(j, d): jnp.dot(a[j, d], vb[j], preferred_element_type=F32) for j, d in units}
        for j in nb:
            vb_scr[rows[j], :] = vb[j]
        for j, d in units:
            o_scr[d, rows[j], :] = o[j, d]
            qd_scr[d, rows[j], :] = q_dec[j, d]
            ke_scr[d, rows[j], :] = k_end[j, d]
            dec8 = pl.ds(pl.multiple_of(blks[j] * (cpb * 8), cpb * 8), cpb * 8)
            dec_scr[d, dec8, :] = jnp.broadcast_to(dec[j, d], (cpb, 8, GLA_DK)).reshape(cpb * 8, GLA_DK)
        return carry

    lax.fori_loop(0, n_blocks // blocks_per_iter, bulk, 0)

    cpi = min(GLA_CHUNKS_PER_ITER, nc)

    def recur(i, carry):
        steps = [(u, d) for u in range(cpi) for d in range(2)]
        chunk = {(u, d): cpi * i + u if d == 0 else nc - 1 - (cpi * i + u) for u, d in steps}
        rows = {s: pl.ds(pl.multiple_of(chunk[s] * c, c), c) for s in steps}
        dec_rows = {s: pl.ds(pl.multiple_of(chunk[s] * 8, 8), 8) for s in steps}
        kv = {(u, d): lax.dot_general(ke_scr[d, rows[u, d], :], vb_scr[rows[u, d], :], _TN,
                                      preferred_element_type=F32) for u, d in steps}
        dec_col = {s: jnp.broadcast_to(dec_scr[s[1], dec_rows[s], :][0:1, :], (LANES, GLA_DK)).T for s in steps}
        dec_full = {s: jnp.concatenate([dec_col[s]] * (GLA_DV // LANES), axis=1) for s in steps}
        s_t = [st_scr[d] for d in range(2)]
        for u, d in steps:
            inter = jnp.dot(qd_scr[d, rows[u, d], :], s_t[d].astype(BF16), preferred_element_type=F32)
            o_scr[d, rows[u, d], :] += inter
            s_t[d] = s_t[d] * dec_full[u, d] + kv[u, d]
        for d in range(2):
            st_scr[d] = s_t[d]
        return carry

    lax.fori_loop(0, nc // cpi, recur, 0)

    def finish(i, carry):
        rows = pl.ds(pl.multiple_of(i * blk_rows, blk_rows), blk_rows)
        o = _rms(o_scr[0, rows, :] + o_scr[1, rows, :], gn_ref[...])
        o_ref[rows, :] = (o * _silu(r_ref[rows, :])).astype(BF16)
        return carry

    lax.fori_loop(0, seq // blk_rows, finish, 0)
    if emit_state:
        for d in range(2):
            s_out_refs[d][...] = st_scr[d]


def _gla(proj, small, wa, ba, gn, *, seq, n_batch, row0, s0=None, emit_state=False):
    rb = row0 // seq
    dk, dv = GLA_DK, GLA_DV
    in_specs = [pl.BlockSpec((seq, dk), lambda b, h: (rb + b, COL_Q // dk + h)),
                pl.BlockSpec((seq, dk), lambda b, h: (rb + b, COL_K // dk + h)),
                pl.BlockSpec((seq, dv), lambda b, h: (rb + b, COL_V // dv + h)),
                pl.BlockSpec((seq, dv), lambda b, h: (rb + b, COL_R // dv + h)),
                pl.BlockSpec((seq, LANES), lambda b, h: (rb + b, 1)),
                pl.BlockSpec((2, LANES, dk), lambda b, h: (0, 0, h)),
                pl.BlockSpec((2, dk), lambda b, h: (0, h)),
                pl.BlockSpec((1, dv), lambda b, h: (0, 0))]
    args = [proj, proj, proj, proj, small, wa, ba, gn]
    state_spec = pl.BlockSpec((None, None, dk, dv), lambda b, h: (b, h, 0, 0))
    if s0 is not None:
        in_specs += [state_spec, state_spec]
        args += list(s0)
    out_specs = [pl.BlockSpec((seq, dv), lambda b, h: (b, h))]
    out_shape = [jax.ShapeDtypeStruct((n_batch * seq, GLA_HEADS * dv), BF16)]
    if emit_state:
        out_specs += [state_spec, state_spec]
        out_shape += [jax.ShapeDtypeStruct((n_batch, GLA_HEADS, dk, dv), F32)] * 2
    return pl.pallas_call(
        functools.partial(_gla_kernel, seq=seq, has_s0=s0 is not None, emit_state=emit_state),
        grid=(n_batch, GLA_HEADS),
        in_specs=in_specs, out_specs=out_specs, out_shape=out_shape,
        scratch_shapes=[pltpu.VMEM((2, seq, dk), BF16),
                        pltpu.VMEM((2, seq, dk), BF16),
                        pltpu.VMEM((seq, dv), BF16),
                        pltpu.VMEM((2, seq // GLA_CHUNK * 8, dk), F32),
                        pltpu.VMEM((2, seq, dv), F32),
                        pltpu.VMEM((2, dk, dv), F32)],
        compiler_params=_cparams(2),
        name=f"gla_{seq}",
    )(*args)


def _rope_block(x, cs_ref):
    return x * cs_ref[:, :LANES] + pltpu.roll(x, MLA_ROPE, axis=1) * cs_ref[:, LANES:]


def _q_kernel(cq_ref, qn_ref, w_ref, cs_ref, o_ref):
    n = _rms(cq_ref[...], qn_ref[...]).astype(BF16)
    for h in range(MLA_HEADS):
        q = jnp.dot(n, w_ref[:, h * HEAD_W:(h + 1) * HEAD_W], preferred_element_type=F32)
        o_ref[:, h * HEAD_W:h * HEAD_W + LANES] = q[:, :LANES].astype(BF16)
        o_ref[:, h * HEAD_W + LANES:(h + 1) * HEAD_W] = _rope_block(q[:, LANES:], cs_ref).astype(BF16)


def _q_proj(proj, stream, q_norm, w_q, rope_tab):
    tm = MLA_TM
    return pl.pallas_call(
        _q_kernel,
        grid=(stream.n_tok // tm,),
        in_specs=[pl.BlockSpec((tm, Q_LORA), lambda i: (i, COL_CQ // Q_LORA)),
                  _resident((1, Q_LORA)),
                  _resident((Q_LORA, MLA_HEADS * HEAD_W)),
                  pl.BlockSpec((tm, 2 * LANES), lambda i: (stream.rope_row(i, tm), 0))],
        out_specs=pl.BlockSpec((tm, MLA_HEADS * HEAD_W), lambda i: (i, 0)),
        out_shape=jax.ShapeDtypeStruct((stream.n_tok, MLA_HEADS * HEAD_W), BF16),
        compiler_params=_cparams(1),
        name=f"mla_q_{stream.name}",
    )(proj, q_norm, w_q, rope_tab)


def _kv_kernel(*refs, norm, emit_ckv):
    ckv_ref, kn_ref, kr_ref, w_ref, cs_ref, k_ref, v_ref = refs[:7]
    c = ckv_ref[...]
    if norm:
        c = _rms(c, kn_ref[...])
    if emit_ckv:
        refs[7][...] = c
    cb = c.astype(BF16)
    nk = MLA_HEADS * MLA_NOPE
    kr = _rope_block(kr_ref[...], cs_ref).astype(BF16)
    v_ref[...] = jnp.dot(cb, w_ref[:, nk:], preferred_element_type=F32).astype(BF16)
    for h0 in range(0, MLA_HEADS, 2):
        kn = jnp.dot(cb, w_ref[:, h0 * MLA_NOPE:(h0 + 2) * MLA_NOPE], preferred_element_type=F32).astype(BF16)
        for h in (h0, h0 + 1):
            k_ref[:, h * HEAD_W:h * HEAD_W + LANES] = kn[:, (h - h0) * MLA_NOPE:(h - h0 + 1) * MLA_NOPE]
            k_ref[:, h * HEAD_W + LANES:(h + 1) * HEAD_W] = kr


def _kv_proj(ckv_src, ckv_col, kr_src, kv_norm, w_kv, rope_tab, *, n_rows, row0, rope_row, norm, emit_ckv):
    tm = MLA_TM
    r0 = row0 // tm
    out_specs = [pl.BlockSpec((tm, MLA_HEADS * HEAD_W), lambda i: (i, 0)),
                 pl.BlockSpec((tm, MLA_HEADS * MLA_V), lambda i: (i, 0))]
    out_shape = [jax.ShapeDtypeStruct((n_rows, MLA_HEADS * HEAD_W), BF16),
                 jax.ShapeDtypeStruct((n_rows, MLA_HEADS * MLA_V), BF16)]
    if emit_ckv:
        out_specs.append(pl.BlockSpec((tm, KV_LORA), lambda i: (i, 0)))
        out_shape.append(jax.ShapeDtypeStruct((n_rows, KV_LORA), F32))
    return pl.pallas_call(
        functools.partial(_kv_kernel, norm=norm, emit_ckv=emit_ckv),
        grid=(n_rows // tm,),
        in_specs=[pl.BlockSpec((tm, KV_LORA), lambda i: (r0 + i, ckv_col)),
                  _resident((1, KV_LORA)),
                  pl.BlockSpec((tm, LANES), lambda i: (r0 + i, 0)),
                  _resident((KV_LORA, MLA_HEADS * (MLA_NOPE + MLA_V))),
                  pl.BlockSpec((tm, 2 * LANES), lambda i: (rope_row(i), 0))],
        out_specs=out_specs, out_shape=out_shape,
        compiler_params=_cparams(1),
        name=f"mla_kv_{row0}_{n_rows}",
    )(ckv_src, kv_norm, kr_src, w_kv, rope_tab)


_EXP2_SCALE = float((MLA_NOPE + MLA_ROPE) ** -0.5 * np.log2(np.e))


def _attn_kernel(*refs, n_seg, heads, tq, lookahead):
    q_ref = refs[0]
    kv = refs[1:1 + 2 * n_seg]
    o_ref = refs[1 + 2 * n_seg]
    nt = q_ref.shape[0] // tq
    work = [(h, t) for h in range(heads) for t in range(nt)]

    def scores(h, t):
        hs = slice(h * HEAD_W, (h + 1) * HEAD_W)
        q = q_ref[t * tq:(t + 1) * tq, hs]
        return [lax.dot_general(q, kv[2 * g][:, hs], _NT, preferred_element_type=F32) for g in range(n_seg)]

    def row_max(s):
        m = s[0].max(axis=-1, keepdims=True)
        for g in range(1, n_seg):
            m = jnp.maximum(m, s[g].max(axis=-1, keepdims=True))
        return m

    def probs(s, m):
        return [jnp.exp2((s[g] - m) * _EXP2_SCALE).astype(BF16) for g in range(n_seg)]

    def values(g, h):
        v = kv[2 * g + 1][:, h * MLA_V:(h + 1) * MLA_V]
        one_col = jnp.where(lax.broadcasted_iota(jnp.int32, v.shape, 1) == 0, 1.0, 0.0).astype(BF16)
        return jnp.concatenate([v, one_col], axis=1)

    v_aug = {(g, h): values(g, h) for g in range(n_seg) for h in range(heads)}

    def weighted(p, h):
        acc = 0.0
        for g in range(n_seg):
            acc = acc + jnp.dot(p[g], v_aug[g, h], preferred_element_type=F32)
        return acc

    def store(acc, h, t):
        out = acc[:, :MLA_V] / acc[:, MLA_V:MLA_V + 1]
        o_ref[t * tq:(t + 1) * tq, h * MLA_V:(h + 1) * MLA_V] = out.astype(BF16)

    if lookahead >= len(work):
        s = [scores(*w) for w in work]
        m = [row_max(x) for x in s]
        p = [probs(x, y) for x, y in zip(s, m)]
        acc = [weighted(x, w[0]) for x, w in zip(p, work)]
        for x, w in zip(acc, work):
            store(x, *w)
    else:
        s_next = scores(*work[0])
        for idx, (h, t) in enumerate(work):
            s = s_next
            if idx + 1 < len(work):
                s_next = scores(*work[idx + 1])
            store(weighted(probs(s, row_max(s)), h), h, t)


def _attention(q, q_row0, segs, *, n_batch, seq, tq, heads):
    qb0 = q_row0 // seq
    n_hblk = MLA_HEADS // heads
    in_specs = [pl.BlockSpec((seq, heads * HEAD_W), lambda b, h: (qb0 + b, h))]
    args = [q]
    for keys, vals, n_keys in segs:
        in_specs += [pl.BlockSpec((n_keys, heads * HEAD_W), lambda b, h: (b, h)),
                     pl.BlockSpec((n_keys, heads * MLA_V), lambda b, h: (b, h))]
        args += [keys, vals]
    return pl.pallas_call(
        functools.partial(_attn_kernel, n_seg=len(segs), heads=heads, tq=tq,
                          lookahead=1 if seq > tq else heads),
        grid=(n_batch, n_hblk),
        in_specs=in_specs,
        out_specs=pl.BlockSpec((seq, heads * MLA_V), lambda b, h: (b, h)),
        out_shape=jax.ShapeDtypeStruct((n_batch * seq, MLA_HEADS * MLA_V), BF16),
        compiler_params=_cparams(2),
        name=f"mla_attn_{seq}",
    )(*args)


def _merge_kernel(og_ref, om_ref, ga_ref, gb_ref, x_ref, mod_ref, g_ref, wg_ref, wm_ref, wo_ref, o_ref):
    yg = jnp.dot(og_ref[...], wg_ref[...], preferred_element_type=F32)
    ym = jnp.dot(om_ref[...], wm_ref[...], preferred_element_type=F32)
    m = (jax.nn.sigmoid(ga_ref[...]) * yg + jax.nn.sigmoid(gb_ref[...]) * ym).astype(BF16)
    o_ref[...] = jnp.dot(m, wo_ref[...], preferred_element_type=F32)
    _residual(x_ref, o_ref, g_ref, mod_ref, 1, 1.0)


def _merge(o_gla, o_mla, proj, x, stream, mod, gains, w_gla_out, w_mla_out, w_out):
    tm = MERGE_TM
    tile = lambda c0: pl.BlockSpec((tm, D_MODEL), lambda i: (i, c0 // D_MODEL))
    return pl.pallas_call(
        _merge_kernel,
        grid=(stream.n_tok // tm,),
        in_specs=[tile(0), tile(0), tile(COL_GA), tile(COL_GB), tile(0),
                  pl.BlockSpec((None, N_MOD, D_MODEL), lambda i: (stream.mod_row(i, tm), 0, 0)),
                  _resident((6, D_MODEL)),
                  _resident((D_MODEL, D_MODEL)), _resident((D_MODEL, D_MODEL)), _resident((D_MODEL, D_MODEL))],
        out_specs=tile(0),
        out_shape=jax.ShapeDtypeStruct((stream.n_tok, D_MODEL), F32),
        compiler_params=_cparams(1),
        name=f"mixer_merge_{stream.name}",
    )(o_gla, o_mla, proj, proj, x, mod, gains, w_gla_out, w_mla_out, w_out)


_SWAP = np.arange(MLA_ROPE)
_SWAP = np.where(_SWAP % 32 < 16, _SWAP + 16, _SWAP - 16)


_FF_BLOCKS = D_FF // LANES
_U_PER_STEP = FF_TILE // LANES


def _prep_ffn_in_kernel(g_ref, *refs):
    u_refs, o_ref = refs[:-1], refs[-1]
    j = pl.program_id(0)

    @pl.when(j < N_FF - 1)
    def _():
        o_ref[:, :FF_TILE] = g_ref[...].astype(BF16)
        for q, u_ref in enumerate(u_refs):
            o_ref[:, FF_TILE + q * LANES:FF_TILE + (q + 1) * LANES] = u_ref[...].astype(BF16)

    @pl.when(j == N_FF - 1)
    def _():
        o_ref[:, :FF_LAST] = g_ref[:, :FF_LAST].astype(BF16)
        for q in range(FF_LAST // LANES):
            o_ref[:, FF_LAST + q * LANES:FF_LAST + (q + 1) * LANES] = u_refs[q][...].astype(BF16)
        o_ref[:, 2 * FF_LAST:] = jnp.zeros((D_MODEL, 2 * (FF_TILE - FF_LAST)), BF16)


def _prep_ffn_in(w, name):
    last = 2 * _FF_BLOCKS - 1
    u_spec = lambda q: pl.BlockSpec((D_MODEL, LANES),
                                    lambda j: (0, jnp.minimum(_FF_BLOCKS + _U_PER_STEP * j + q, last)))
    return pl.pallas_call(
        _prep_ffn_in_kernel,
        grid=(N_FF,),
        in_specs=[pl.BlockSpec((D_MODEL, FF_TILE), lambda j: (0, j))] + [u_spec(q) for q in range(_U_PER_STEP)],
        out_specs=pl.BlockSpec((D_MODEL, 2 * FF_TILE), lambda j: (0, j)),
        out_shape=jax.ShapeDtypeStruct((D_MODEL, N_FF * 2 * FF_TILE), BF16),
        compiler_params=_cparams(1),
        name=name,
    )(w, *([w] * _U_PER_STEP))


_FF_FULL = D_FF // FF_TILE
_FF_TAIL = (D_FF - _FF_FULL * FF_TILE) // LANES


def _prep_ffn_out_kernel(w_ref, *refs):
    tail_refs, o_ref = refs[:-1], refs[-1]
    j = pl.program_id(0)

    @pl.when(j < _FF_FULL)
    def _():
        o_ref[...] = w_ref[...].astype(BF16)

    @pl.when(j == _FF_FULL)
    def _():
        for t, t_ref in enumerate(tail_refs):
            o_ref[t * LANES:(t + 1) * LANES, :] = t_ref[...].astype(BF16)
        o_ref[_FF_TAIL * LANES:, :] = jnp.zeros((FF_TILE - _FF_TAIL * LANES, D_MODEL), BF16)


def _prep_ffn_out(w, name):
    tail = lambda t: pl.BlockSpec((LANES, D_MODEL), lambda j: (_FF_FULL * (FF_TILE // LANES) + t, 0))
    return pl.pallas_call(
        _prep_ffn_out_kernel,
        grid=(N_FF,),
        in_specs=[pl.BlockSpec((FF_TILE, D_MODEL), lambda j: (jnp.minimum(j, _FF_FULL - 1), 0))]
        + [tail(t) for t in range(_FF_TAIL)],
        out_specs=pl.BlockSpec((FF_TILE, D_MODEL), lambda j: (j, 0)),
        out_shape=jax.ShapeDtypeStruct((FF_PAD, D_MODEL), BF16),
        compiler_params=_cparams(1),
        name=name,
    )(w, *([w] * _FF_TAIL))


_W_IN_GROUPS = ((0, 6144), (7264, 2048), (9312, 2048), (6176, 512), (6688, 512))
_W_IN_GROUPS_T = ((0, 6144), (7264, 2048), (9312, 2048), (6176, 1024))
_W_IN_ROW_UNIT = 32


def _prep_w_in_kernel(start_ref, w_ref, o_ref):
    del start_ref
    o_ref[...] = w_ref[...].astype(BF16)


def _prep_w_in(w):
    wt = jnp.swapaxes(w, 0, 1)
    tn = PROJ_TN
    starts = np.array([(s + o) // _W_IN_ROW_UNIT for s, n in _W_IN_GROUPS_T for o in range(0, n, tn)], np.int32)
    main = pl.pallas_call(
        _prep_w_in_kernel,
        grid_spec=pltpu.PrefetchScalarGridSpec(
            num_scalar_prefetch=1,
            grid=(PROJ_MAIN // tn,),
            in_specs=[pl.BlockSpec((pl.Element(tn), pl.Element(D_MODEL)),
                                   lambda j, start: (start[j] * _W_IN_ROW_UNIT, 0))],
            out_specs=pl.BlockSpec((tn, D_MODEL), lambda j, start: (j, 0))),
        out_shape=jax.ShapeDtypeStruct((PROJ_MAIN, D_MODEL), BF16),
        compiler_params=_cparams(1),
        name="prep_w_in",
    )(jnp.asarray(starts), wt)
    af, ab, kr = wt[6144:6160], wt[6160:6176], wt[7200:7264]
    small = jnp.concatenate([kr, kr[_SWAP], af, ab,
                             jnp.zeros((PROJ_SMALL - 2 * MLA_ROPE - 2 * GLA_RANK, D_MODEL), F32)], axis=0).astype(BF16)
    return main, small


def _prep_w_uq(w):
    w = w.astype(BF16).reshape(Q_LORA, MLA_HEADS, MLA_NOPE + MLA_ROPE)
    rope = w[..., MLA_NOPE:]
    return jnp.concatenate([w[..., :MLA_NOPE], rope, rope[..., _SWAP]], axis=-1).reshape(Q_LORA, MLA_HEADS * HEAD_W)


def _prep_w_ukv(w):
    w = w.astype(BF16).reshape(KV_LORA, MLA_HEADS, MLA_NOPE + MLA_V)
    return jnp.concatenate([w[..., :MLA_NOPE].reshape(KV_LORA, -1), w[..., MLA_NOPE:].reshape(KV_LORA, -1)], axis=1)


def _prep_w_alpha(w):
    out = jnp.zeros((2, LANES, GLA_HEADS * GLA_DK), BF16)
    out = out.at[0, 0:GLA_RANK].set(w[0].astype(BF16))
    return out.at[1, GLA_RANK:2 * GLA_RANK].set(w[1].astype(BF16))


def _rope_table(tm):
    rows = DEC_SEQ // GRID_W
    row = np.repeat(np.arange(rows), GRID_W).astype(np.float64)
    col = np.tile(np.arange(GRID_W), rows).astype(np.float64)
    half = MLA_ROPE // 2
    inv_freq = ROPE_THETA ** (-np.arange(0, half, 2, dtype=np.float64) / half)
    ang_r, ang_c = row[:, None] * inv_freq, col[:, None] * inv_freq
    cos = np.concatenate([np.cos(ang_r), np.cos(ang_r), np.cos(ang_c), np.cos(ang_c)], axis=1)
    sin = np.concatenate([-np.sin(ang_r), np.sin(ang_r), -np.sin(ang_c), np.sin(ang_c)], axis=1)
    z = np.zeros((DEC_SEQ, LANES - MLA_ROPE))
    lat = np.concatenate([cos, z, sin, z], axis=1)
    ident = np.concatenate([np.ones((tm, MLA_ROPE)), np.zeros((tm, 2 * LANES - MLA_ROPE))], axis=1)
    return jnp.asarray(np.concatenate([ident, lat], axis=0), F32)


def kernel(x_prompt, x_sample, cache_ckv, cache_krope, state_gla_fwd, state_gla_bwd, c, c_ctx, w_ada, b_ada, norm_gains, w_ffn1_in, w_ffn1_out, w_ffn2_in, w_ffn2_out, w_in, w_gla_alpha, b_gla_alpha, gla_norm, w_gla_out, q_norm, kv_norm, w_uq, w_ukv, w_mla_out, w_out):
    c_all = jnp.concatenate([c_ctx[None, :], c, jnp.zeros((8 - 1 - DEC_BATCH, D_MODEL), F32)], axis=0)
    mod = _modulation(c_all, w_ada[0], b_ada[0]).reshape(8, N_MOD, D_MODEL)
    gains = norm_gains[0]

    streams = (_CTX, _LAT)
    x = {_CTX: x_prompt.reshape(N_CTX, D_MODEL), _LAT: x_sample.reshape(N_LAT, D_MODEL)}

    w_gu, w_o = _prep_ffn_in(w_ffn1_in[0], "prep_ffn0_in"), _prep_ffn_out(w_ffn1_out[0], "prep_ffn0_out")
    x = {s: _ffn(x[s], s, mod, gains, w_gu, w_o, sub=0) for s in streams}

    w_main, w_small = _prep_w_in(w_in[0])
    proj, small = {}, {}
    proj[_CTX], small[_CTX], w_o2 = _mixer_proj(x[_CTX], _CTX, mod, gains, w_main, w_small, w_ffn2_out[0], "ffn_out")
    proj[_LAT], small[_LAT], w_gu2 = _mixer_proj(x[_LAT], _LAT, mod, gains, w_main, w_small, w_ffn2_in[0], "ffn_in")

    wa = _prep_w_alpha(w_gla_alpha[0])
    ba = b_gla_alpha[0]
    gn = gla_norm[0].reshape(1, GLA_DV)
    o_gla = {}
    o_gla[_CTX], s_f, s_b = _gla(proj[_CTX], small[_CTX], wa, ba, gn, seq=SEQ, n_batch=BATCH, row0=0,
                                 emit_state=True)
    (o_gla[_LAT],) = _gla(proj[_LAT], small[_LAT], wa, ba, gn, seq=DEC_SEQ, n_batch=DEC_BATCH, row0=0,
                          s0=(state_gla_fwd[:, 0], state_gla_bwd[:, 0]))

    tm = MLA_TM
    rope_tab = _rope_table(tm)
    w_q = _prep_w_uq(w_uq[0])
    q = {s: _q_proj(proj[s], s, q_norm[0].reshape(1, Q_LORA), w_q, rope_tab) for s in streams}
    w_kv = _prep_w_ukv(w_ukv[0])
    kvn = kv_norm[0].reshape(1, KV_LORA)
    ckv_col = COL_CKV // KV_LORA
    k_ctx, v_ctx, ckv_new = _kv_proj(proj[_CTX], ckv_col, small[_CTX], kvn, w_kv, rope_tab, n_rows=N_CTX, row0=0,
                                     rope_row=lambda i: _CTX.rope_row(i, tm), norm=True, emit_ckv=True)
    k_lat, v_lat = _kv_proj(proj[_LAT], ckv_col, small[_LAT], kvn, w_kv, rope_tab, n_rows=N_LAT, row0=0,
                            rope_row=lambda i: _LAT.rope_row(i, tm), norm=True, emit_ckv=False)
    n_past = DEC_BATCH * PAST_LEN
    kr_past = jnp.pad(cache_krope[:, 0].reshape(n_past, MLA_ROPE), ((0, 0), (0, LANES - MLA_ROPE)))
    k_past, v_past = _kv_proj(cache_ckv[:, 0].reshape(n_past, KV_LORA), 0, kr_past, kvn, w_kv, rope_tab,
                              n_rows=n_past, row0=0, rope_row=lambda i: 0 * i, norm=False, emit_ckv=False)
    o_mla = {
        _CTX: _attention(q[_CTX], 0, [(k_ctx, v_ctx, SEQ)], n_batch=BATCH, seq=SEQ, tq=SEQ, heads=MLA_HEADS),
        _LAT: _attention(q[_LAT], 0, [(k_past, v_past, PAST_LEN), (k_lat, v_lat, DEC_SEQ)],
                         n_batch=DEC_BATCH, seq=DEC_SEQ, tq=ATT_TQ, heads=ATT_HEADS)}

    w_merge = (w_gla_out[0].astype(BF16), w_mla_out[0].astype(BF16), w_out[0].astype(BF16))
    x = {s: _merge(o_gla[s], o_mla[s], proj[s], x[s], s, mod, gains, *w_merge) for s in streams}
    y = {s: _ffn(x[s], s, mod, gains, w_gu2, w_o2, sub=2) for s in streams}

    y_prompt = y[_CTX].reshape(BATCH, SEQ, D_MODEL)
    y_sample = y[_LAT].reshape(DEC_BATCH, DEC_SEQ, D_MODEL)
    new_ckv = ckv_new.reshape(BATCH, 1, SEQ, KV_LORA)
    new_krope = small[_CTX][:, :MLA_ROPE].reshape(BATCH, 1, SEQ, MLA_ROPE)
    return (y_prompt, y_sample, new_ckv, new_krope,
            s_f.reshape(BATCH, 1, GLA_HEADS, GLA_DK, GLA_DV), s_b.reshape(BATCH, 1, GLA_HEADS, GLA_DK, GLA_DV))
```

```python
import functools
from typing import NamedTuple

import numpy as np
import jax
import jax.numpy as jnp
from jax import lax
from jax.experimental import pallas as pl
from jax.experimental.pallas import tpu as pltpu

F32 = jnp.float32
BF16 = jnp.bfloat16

D_MODEL = 2048
BATCH, SEQ = 16, 256
DEC_BATCH, DEC_SEQ = 4, 2048
PAST_LEN = 256
GRID_W = 64
D_FF = 5504
N_MOD = 9
GLA_HEADS, GLA_DK, GLA_DV = 4, 256, 512
GLA_RANK = 16
GLA_TAU = 16.0
GLA_CHUNK = 64
GLA_BLOCK = 256
GLA_BLOCKS_PER_ITER = 4
GLA_CHUNKS_PER_ITER = 8
MLA_HEADS, MLA_NOPE, MLA_ROPE, MLA_V = 16, 128, 64, 128
Q_LORA = KV_LORA = 512
ROPE_THETA = 10000.0
NORM_EPS = 1e-6

N_CTX = BATCH * SEQ
N_LAT = DEC_BATCH * DEC_SEQ
N_TOK = N_CTX + N_LAT

LANES = 128
V7X_VMEM_BYTES = 64 * 1024 * 1024
V7X_VMEM_LIMIT_BYTES = 60000 * 1024

FF_TILE = 512
FF_PAD = -(-D_FF // FF_TILE) * FF_TILE
N_FF = FF_PAD // FF_TILE
FF_LAST = D_FF - (N_FF - 1) * FF_TILE
FFN_TM = 1024
FFN_ROWS = 512
PROJ_TM, PROJ_TN = 1024, 1024
MLA_TM = 1024
MERGE_TM = 256
ATT_TQ = 1024
ATT_HEADS = 1
HEAD_W = 2 * LANES

COL_Q, COL_K, COL_V, COL_R = 0, 1024, 2048, 4096
COL_GA, COL_GB, COL_CQ, COL_CKV = 6144, 8192, 10240, 10752
PROJ_MAIN = 11264
PROJ_SMALL = 256


class _Stream(NamedTuple):
    name: str
    n_batch: int
    seq: int
    mod_base: int
    per_batch_mod: bool
    rope: bool

    @property
    def n_tok(self):
        return self.n_batch * self.seq

    def mod_row(self, tile, tm):
        return self.mod_base + (tile * tm // self.seq if self.per_batch_mod else 0 * tile)

    def rope_row(self, tile, tm):
        return 1 + tile % (self.seq // tm) if self.rope else 0 * tile


_CTX = _Stream("ctx", BATCH, SEQ, 0, False, False)
_LAT = _Stream("lat", DEC_BATCH, DEC_SEQ, 1, True, True)


def _rms(x, gain):
    ms = jnp.mean(x * x, axis=-1, keepdims=True)
    return x * lax.rsqrt(ms + NORM_EPS) * gain


def _silu(x):
    return x * jax.nn.sigmoid(x)


_NT = (((1,), (1,)), ((), ()))
_TN = (((0,), (0,)), ((), ()))


def _cparams(n_axes, vmem_limit_bytes=V7X_VMEM_LIMIT_BYTES):
    return pltpu.CompilerParams(dimension_semantics=("arbitrary",) * n_axes, vmem_limit_bytes=vmem_limit_bytes)


def _resident(shape):
    nd = len(shape)
    return pl.BlockSpec(shape, lambda *_: (0,) * nd, pipeline_mode=pl.Buffered(1))


def _mod_kernel(c_ref, w_ref, b_ref, o_ref):
    s = _silu(c_ref[...]).astype(BF16)
    o_ref[...] = jnp.dot(s, w_ref[...].astype(BF16), preferred_element_type=F32) + b_ref[...]


def _modulation(c_all, w_ada, b_ada):
    n = w_ada.shape[1]
    tn = 1024
    return pl.pallas_call(
        _mod_kernel,
        grid=(n // tn,),
        in_specs=[pl.BlockSpec((8, D_MODEL), lambda j: (0, 0)),
                  pl.BlockSpec((D_MODEL, tn), lambda j: (0, j)),
                  pl.BlockSpec((1, tn), lambda j: (0, j))],
        out_specs=pl.BlockSpec((8, tn), lambda j: (0, j)),
        out_shape=jax.ShapeDtypeStruct((8, n), F32),
        compiler_params=_cparams(1),
        name="adaln_mod",
    )(c_all, w_ada, b_ada.reshape(1, n))


def _row(ref, r):
    return ref[r:r + 1, :]


def _norm_mod(x_ref, h_ref, g_ref, mod_ref, sub):
    h = _rms(x_ref[...], _row(g_ref, 2 * sub)) * (1.0 + _row(mod_ref, 3 * sub + 1)) + _row(mod_ref, 3 * sub)
    h_ref[...] = h.astype(BF16)


def _residual(x_ref, o_ref, g_ref, mod_ref, sub, gate_scale):
    y = _rms(o_ref[...], _row(g_ref, 2 * sub + 1))
    o_ref[...] = x_ref[...] + (gate_scale * _row(mod_ref, 3 * sub + 2)) * y


def _ffn_kernel(x_ref, mod_ref, g_ref, wgu_ref, wo_ref, o_ref, h_ref, *, sub):
    j = pl.program_id(1)

    @pl.when(j == 0)
    def _():
        _norm_mod(x_ref, h_ref, g_ref, mod_ref, sub)
        o_ref[...] = jnp.zeros_like(o_ref)

    def chunk(width):
        for r in range(0, h_ref.shape[0], FFN_ROWS):
            rows = slice(r, r + FFN_ROWS)
            gu = jnp.dot(h_ref[rows, :], wgu_ref[:, :2 * width], preferred_element_type=F32)
            a = (_silu(gu[:, :width]) * gu[:, width:]).astype(BF16)
            o_ref[rows, :] += jnp.dot(a, wo_ref[:width, :], preferred_element_type=F32)

    last = pl.num_programs(1) - 1

    @pl.when(j < last)
    def _():
        chunk(FF_TILE)

    @pl.when(j == last)
    def _():
        chunk(FF_LAST)
        _residual(x_ref, o_ref, g_ref, mod_ref, sub, 0.5)


def _ffn(x, stream, mod, gains, w_gu, w_o, sub):
    tm = FFN_TM
    return pl.pallas_call(
        functools.partial(_ffn_kernel, sub=sub),
        grid=(stream.n_tok // tm, N_FF),
        in_specs=[pl.BlockSpec((tm, D_MODEL), lambda i, j: (i, 0)),
                  pl.BlockSpec((None, N_MOD, D_MODEL), lambda i, j: (stream.mod_row(i, tm), 0, 0)),
                  pl.BlockSpec((6, D_MODEL), lambda i, j: (0, 0)),
                  pl.BlockSpec((D_MODEL, 2 * FF_TILE), lambda i, j: (0, j)),
                  pl.BlockSpec((FF_TILE, D_MODEL), lambda i, j: (j, 0))],
        out_specs=pl.BlockSpec((tm, D_MODEL), lambda i, j: (i, 0)),
        out_shape=jax.ShapeDtypeStruct((stream.n_tok, D_MODEL), F32),
        scratch_shapes=[pltpu.VMEM((tm, D_MODEL), BF16)],
        compiler_params=_cparams(2, V7X_VMEM_BYTES - 2 * 1024 * 1024),
        name=f"ffn{sub}_{stream.name}",
    )(x, mod, gains, w_gu, w_o)


def _proj_kernel(x_ref, mod_ref, g_ref, wt_ref, wst_ref, side_ref, o_ref, os_ref, side_o_ref, h_ref, *, n_side):
    i, j = pl.program_id(0), pl.program_id(1)

    @pl.when(j == 0)
    def _():
        _norm_mod(x_ref, h_ref, g_ref, mod_ref, 1)
        os_ref[...] = lax.dot_general(h_ref[...], wst_ref[...], _NT, preferred_element_type=F32)

    o_ref[...] = lax.dot_general(h_ref[...], wt_ref[...], _NT, preferred_element_type=F32)
    step = i * pl.num_programs(1) + j
    side_o_ref[...] = jnp.where(step < n_side, side_ref[...], 0.0).astype(BF16)


def _ffn_in_block_dst(s):
    per_chunk = FF_TILE // LANES
    is_up = (s >= _FF_BLOCKS).astype(jnp.int32)
    b = s - is_up * _FF_BLOCKS
    c, q = b // per_chunk, b % per_chunk
    full = c * 2 * per_chunk + is_up * per_chunk + q
    ragged = (N_FF - 1) * 2 * per_chunk + is_up * (FF_LAST // LANES) + q
    return jnp.where(s >= 2 * _FF_BLOCKS, s, jnp.where(c < N_FF - 1, full, ragged))


def _mixer_proj(x, stream, mod, gains, w_main, w_small, side_w, side_kind):
    tm, tn = PROJ_TM, PROJ_TN
    n_tok = stream.n_tok
    n_j = PROJ_MAIN // tn
    n_steps = n_tok // tm * n_j
    step = lambda i, j: i * n_j + j
    if side_kind == "ffn_in":
        n_side, n_dst = 2 * _FF_BLOCKS, N_FF * 2 * FF_TILE // LANES
        side_in = pl.BlockSpec((D_MODEL, LANES), lambda i, j: (0, jnp.minimum(step(i, j), n_side - 1)))
        side_out = pl.BlockSpec((D_MODEL, LANES), lambda i, j: (0, _ffn_in_block_dst(step(i, j))))
        side_shape = jax.ShapeDtypeStruct((D_MODEL, N_FF * 2 * FF_TILE), BF16)
    else:
        n_side, n_dst = _FF_BLOCKS, FF_PAD // LANES
        side_in = pl.BlockSpec((LANES, D_MODEL), lambda i, j: (jnp.minimum(step(i, j), n_side - 1), 0))
        side_out = pl.BlockSpec((LANES, D_MODEL), lambda i, j: (step(i, j), 0))
        side_shape = jax.ShapeDtypeStruct((FF_PAD, D_MODEL), BF16)
    assert n_steps == n_dst, "the side cast needs exactly one destination block per grid step"
    return pl.pallas_call(
        functools.partial(_proj_kernel, n_side=n_side),
        grid=(n_tok // tm, n_j),
        in_specs=[pl.BlockSpec((tm, D_MODEL), lambda i, j: (i, 0)),
                  pl.BlockSpec((None, N_MOD, D_MODEL), lambda i, j: (stream.mod_row(i, tm), 0, 0)),
                  pl.BlockSpec((6, D_MODEL), lambda i, j: (0, 0)),
                  pl.BlockSpec((tn, D_MODEL), lambda i, j: (j, 0)),
                  pl.BlockSpec((PROJ_SMALL, D_MODEL), lambda i, j: (0, 0)),
                  side_in],
        out_specs=[pl.BlockSpec((tm, tn), lambda i, j: (i, j)),
                   pl.BlockSpec((tm, PROJ_SMALL), lambda i, j: (i, 0)),
                   side_out],
        out_shape=[jax.ShapeDtypeStruct((n_tok, PROJ_MAIN), F32),
                   jax.ShapeDtypeStruct((n_tok, PROJ_SMALL), F32),
                   side_shape],
        scratch_shapes=[pltpu.VMEM((tm, D_MODEL), BF16)],
        compiler_params=_cparams(2),
        name=f"mixer_proj_{stream.name}",
    )(x, mod, gains, w_main, w_small, side_w)


def _log_sigmoid(x):
    return jnp.minimum(x, 0.0) - jnp.log(1.0 + jnp.exp(-jnp.abs(x)))


def _gla_kernel(*refs, seq, has_s0, emit_state):
    q_ref, k_ref, v_ref, r_ref, a_ref, wa_ref, ba_ref, gn_ref = refs[:8]
    pos = 8
    if has_s0:
        s0_refs = refs[pos:pos + 2]
        pos += 2
    o_ref = refs[pos]
    pos += 1
    if emit_state:
        s_out_refs = refs[pos:pos + 2]
        pos += 2
    qd_scr, ke_scr, vb_scr, dec_scr, o_scr, st_scr = refs[pos:pos + 6]

    c, blk_rows = GLA_CHUNK, GLA_BLOCK
    cpb = blk_rows // c
    nc = seq // c
    for d in range(2):
        st_scr[d] = s0_refs[d][...] if has_s0 else jnp.zeros((GLA_DK, GLA_DV), F32)

    row = lax.broadcasted_iota(jnp.int32, (blk_rows, blk_rows), 0)
    col = lax.broadcasted_iota(jnp.int32, (blk_rows, blk_rows), 1)
    same_chunk = (row // c) == (col // c)
    masks = (same_chunk & (col <= row), same_chunk & (col >= row))

    tri = [jnp.where(masks[d], 1.0, 0.0).astype(BF16) for d in range(2)]
    n_blocks = seq // blk_rows
    blocks_per_iter = min(GLA_BLOCKS_PER_ITER, n_blocks)

    def bulk(it, carry):
        blks = [it * blocks_per_iter + j for j in range(blocks_per_iter)]
        rows = [pl.ds(pl.multiple_of(blk * blk_rows, blk_rows), blk_rows) for blk in blks]
        nb = range(blocks_per_iter)
        a_in = [a_ref[rows[j], :].astype(BF16) for j in nb]
        q = [q_ref[rows[j], :] * (GLA_DK ** -0.5) for j in nb]
        k = [k_ref[rows[j], :] for j in nb]
        vb = [v_ref[rows[j], :].astype(BF16) for j in nb]
        units = [(j, d) for j in nb for d in range(2)]
        x = {(j, d): jnp.dot(a_in[j], wa_ref[d], preferred_element_type=F32) + ba_ref[d:d + 1, :] for j, d in units}
        la = {u: _log_sigmoid(x[u]) * (1.0 / GLA_TAU) for u in units}
        la1 = {u: la[u].astype(BF16) for u in units}
        rem = {u: la[u] - la1[u].astype(F32) for u in units}
        la2 = {u: rem[u].astype(BF16) for u in units}
        la3 = {u: (rem[u] - la2[u].astype(F32)).astype(BF16) for u in units}
        b = {(j, d): jnp.dot(tri[d], la1[j, d], preferred_element_type=F32)
             + jnp.dot(tri[d], la2[j, d], preferred_element_type=F32)
             + jnp.dot(tri[d], la3[j, d], preferred_element_type=F32) for j, d in units}
        b3 = {u: b[u].reshape(cpb, c, GLA_DK) for u in units}
        b_last = {(j, d): b3[j, d][:, c - 1:c, :] if d == 0 else b3[j, d][:, 0:1, :] for j, d in units}
        dec = {u: jnp.exp(b_last[u]) for u in units}
        dec_rows = {u: jnp.broadcast_to(dec[u], (cpb, c, GLA_DK)).reshape(blk_rows, GLA_DK) for u in units}
        q_dec = {(j, d): (q[j] * jnp.exp(b[j, d])).astype(BF16) for j, d in units}
        k_inv = {(j, d): k[j] * jnp.exp(-b[j, d]) for j, d in units}
        k_end = {u: (k_inv[u] * dec_rows[u]).astype(BF16) for u in units}
        a = {u: lax.dot_general(q_dec[u], k_inv[u].astype(BF16), _NT, preferred_element_type=F32) for u in units}
        a = {(j, d): jnp.where(masks[d], a[j, d], 0.0).astype(BF16) for j, d in units}
        o = {(j, d): jnp.dot(a[j, d], vb[j], preferred_element_type=F32) for j, d in units}
        for j in nb:
            vb_scr[rows[j], :] = vb[j]
        for j, d in units:
            o_scr[d, rows[j], :] = o[j, d]
            qd_scr[d, rows[j], :] = q_dec[j, d]
            ke_scr[d, rows[j], :] = k_end[j, d]
            dec8 = pl.ds(pl.multiple_of(blks[j] * (cpb * 8), cpb * 8), cpb * 8)
            dec_scr[d, dec8, :] = jnp.broadcast_to(dec[j, d], (cpb, 8, GLA_DK)).reshape(cpb * 8, GLA_DK)
        return carry

    lax.fori_loop(0, n_blocks // blocks_per_iter, bulk, 0)

    cpi = min(GLA_CHUNKS_PER_ITER, nc)

    def recur(i, carry):
        steps = [(u, d) for u in range(cpi) for d in range(2)]
        chunk = {(u, d): cpi * i + u if d == 0 else nc - 1 - (cpi * i + u) for u, d in steps}
        rows = {s: pl.ds(pl.multiple_of(chunk[s] * c, c), c) for s in steps}
        dec_rows = {s: pl.ds(pl.multiple_of(chunk[s] * 8, 8), 8) for s in steps}
        kv = {(u, d): lax.dot_general(ke_scr[d, rows[u, d], :], vb_scr[rows[u, d], :], _TN,
                                      preferred_element_type=F32) for u, d in steps}
        dec_col = {s: jnp.broadcast_to(dec_scr[s[1], dec_rows[s], :][0:1, :], (LANES, GLA_DK)).T for s in steps}
        dec_full = {s: jnp.concatenate([dec_col[s]] * (GLA_DV // LANES), axis=1) for s in steps}
        s_t = [st_scr[d] for d in range(2)]
        for u, d in steps:
            inter = jnp.dot(qd_scr[d, rows[u, d], :], s_t[d].astype(BF16), preferred_element_type=F32)
            o_scr[d, rows[u, d], :] += inter
            s_t[d] = s_t[d] * dec_full[u, d] + kv[u, d]
        for d in range(2):
            st_scr[d] = s_t[d]
        return carry

    lax.fori_loop(0, nc // cpi, recur, 0)

    def finish(i, carry):
        rows = pl.ds(pl.multiple_of(i * blk_rows, blk_rows), blk_rows)
        o = _rms(o_scr[0, rows, :] + o_scr[1, rows, :], gn_ref[...])
        o_ref[rows, :] = (o * _silu(r_ref[rows, :])).astype(BF16)
        return carry

    lax.fori_loop(0, seq // blk_rows, finish, 0)
    if emit_state:
        for d in range(2):
            s_out_refs[d][...] = st_scr[d]


def _gla(proj, small, wa, ba, gn, *, seq, n_batch, row0, s0=None, emit_state=False):
    rb = row0 // seq
    dk, dv = GLA_DK, GLA_DV
    in_specs = [pl.BlockSpec((seq, dk), lambda b, h: (rb + b, COL_Q // dk + h)),
                pl.BlockSpec((seq, dk), lambda b, h: (rb + b, COL_K // dk + h)),
                pl.BlockSpec((seq, dv), lambda b, h: (rb + b, COL_V // dv + h)),
                pl.BlockSpec((seq, dv), lambda b, h: (rb + b, COL_R // dv + h)),
                pl.BlockSpec((seq, LANES), lambda b, h: (rb + b, 1)),
                pl.BlockSpec((2, LANES, dk), lambda b, h: (0, 0, h)),
                pl.BlockSpec((2, dk), lambda b, h: (0, h)),
                pl.BlockSpec((1, dv), lambda b, h: (0, 0))]
    args = [proj, proj, proj, proj, small, wa, ba, gn]
    state_spec = pl.BlockSpec((None, None, dk, dv), lambda b, h: (b, h, 0, 0))
    if s0 is not None:
        in_specs += [state_spec, state_spec]
        args += list(s0)
    out_specs = [pl.BlockSpec((seq, dv), lambda b, h: (b, h))]
    out_shape = [jax.ShapeDtypeStruct((n_batch * seq, GLA_HEADS * dv), BF16)]
    if emit_state:
        out_specs += [state_spec, state_spec]
        out_shape += [jax.ShapeDtypeStruct((n_batch, GLA_HEADS, dk, dv), F32)] * 2
    return pl.pallas_call(
        functools.partial(_gla_kernel, seq=seq, has_s0=s0 is not None, emit_state=emit_state),
        grid=(n_batch, GLA_HEADS),
        in_specs=in_specs, out_specs=out_specs, out_shape=out_shape,
        scratch_shapes=[pltpu.VMEM((2, seq, dk), BF16),
                        pltpu.VMEM((2, seq, dk), BF16),
                        pltpu.VMEM((seq, dv), BF16),
                        pltpu.VMEM((2, seq // GLA_CHUNK * 8, dk), F32),
                        pltpu.VMEM((2, seq, dv), F32),
                        pltpu.VMEM((2, dk, dv), F32)],
        compiler_params=_cparams(2),
        name=f"gla_{seq}",
    )(*args)


def _rope_block(x, cs_ref):
    return x * cs_ref[:, :LANES] + pltpu.roll(x, MLA_ROPE, axis=1) * cs_ref[:, LANES:]


def _q_kernel(cq_ref, qn_ref, w_ref, cs_ref, o_ref):
    n = _rms(cq_ref[...], qn_ref[...]).astype(BF16)
    for h in range(MLA_HEADS):
        q = jnp.dot(n, w_ref[:, h * HEAD_W:(h + 1) * HEAD_W], preferred_element_type=F32)
        o_ref[:, h * HEAD_W:h * HEAD_W + LANES] = q[:, :LANES].astype(BF16)
        o_ref[:, h * HEAD_W + LANES:(h + 1) * HEAD_W] = _rope_block(q[:, LANES:], cs_ref).astype(BF16)


def _q_proj(proj, stream, q_norm, w_q, rope_tab):
    tm = MLA_TM
    return pl.pallas_call(
        _q_kernel,
        grid=(stream.n_tok // tm,),
        in_specs=[pl.BlockSpec((tm, Q_LORA), lambda i: (i, COL_CQ // Q_LORA)),
                  _resident((1, Q_LORA)),
                  _resident((Q_LORA, MLA_HEADS * HEAD_W)),
                  pl.BlockSpec((tm, 2 * LANES), lambda i: (stream.rope_row(i, tm), 0))],
        out_specs=pl.BlockSpec((tm, MLA_HEADS * HEAD_W), lambda i: (i, 0)),
        out_shape=jax.ShapeDtypeStruct((stream.n_tok, MLA_HEADS * HEAD_W), BF16),
        compiler_params=_cparams(1),
        name=f"mla_q_{stream.name}",
    )(proj, q_norm, w_q, rope_tab)


def _kv_kernel(*refs, norm, emit_ckv):
    ckv_ref, kn_ref, kr_ref, w_ref, cs_ref, k_ref, v_ref = refs[:7]
    c = ckv_ref[...]
    if norm:
        c = _rms(c, kn_ref[...])
    if emit_ckv:
        refs[7][...] = c
    cb = c.astype(BF16)
    nk = MLA_HEADS * MLA_NOPE
    kr = _rope_block(kr_ref[...], cs_ref).astype(BF16)
    v_ref[...] = jnp.dot(cb, w_ref[:, nk:], preferred_element_type=F32).astype(BF16)
    for h0 in range(0, MLA_HEADS, 2):
        kn = jnp.dot(cb, w_ref[:, h0 * MLA_NOPE:(h0 + 2) * MLA_NOPE], preferred_element_type=F32).astype(BF16)
        for h in (h0, h0 + 1):
            k_ref[:, h * HEAD_W:h * HEAD_W + LANES] = kn[:, (h - h0) * MLA_NOPE:(h - h0 + 1) * MLA_NOPE]
            k_ref[:, h * HEAD_W + LANES:(h + 1) * HEAD_W] = kr


def _kv_proj(ckv_src, ckv_col, kr_src, kv_norm, w_kv, rope_tab, *, n_rows, row0, rope_row, norm, emit_ckv):
    tm = MLA_TM
    r0 = row0 // tm
    out_specs = [pl.BlockSpec((tm, MLA_HEADS * HEAD_W), lambda i: (i, 0)),
                 pl.BlockSpec((tm, MLA_HEADS * MLA_V), lambda i: (i, 0))]
    out_shape = [jax.ShapeDtypeStruct((n_rows, MLA_HEADS * HEAD_W), BF16),
                 jax.ShapeDtypeStruct((n_rows, MLA_HEADS * MLA_V), BF16)]
    if emit_ckv:
        out_specs.append(pl.BlockSpec((tm, KV_LORA), lambda i: (i, 0)))
        out_shape.append(jax.ShapeDtypeStruct((n_rows, KV_LORA), F32))
    return pl.pallas_call(
        functools.partial(_kv_kernel, norm=norm, emit_ckv=emit_ckv),
        grid=(n_rows // tm,),
        in_specs=[pl.BlockSpec((tm, KV_LORA), lambda i: (r0 + i, ckv_col)),
                  _resident((1, KV_LORA)),
                  pl.BlockSpec((tm, LANES), lambda i: (r0 + i, 0)),
                  _resident((KV_LORA, MLA_HEADS * (MLA_NOPE + MLA_V))),
                  pl.BlockSpec((tm, 2 * LANES), lambda i: (rope_row(i), 0))],
        out_specs=out_specs, out_shape=out_shape,
        compiler_params=_cparams(1),
        name=f"mla_kv_{row0}_{n_rows}",
    )(ckv_src, kv_norm, kr_src, w_kv, rope_tab)


_EXP2_SCALE = float((MLA_NOPE + MLA_ROPE) ** -0.5 * np.log2(np.e))


def _attn_kernel(*refs, n_seg, n_side, heads, tq, lookahead):
    q_ref = refs[0]
    kv = refs[1:1 + 2 * n_seg]
    side_in = refs[1 + 2 * n_seg:1 + 2 * n_seg + n_side]
    o_ref = refs[1 + 2 * n_seg + n_side]
    side_out = refs[2 + 2 * n_seg + n_side:]
    for src, dst in zip(side_in, side_out):
        dst[...] = src[...].astype(BF16)
    nt = q_ref.shape[0] // tq
    work = [(h, t) for h in range(heads) for t in range(nt)]

    def scores(h, t):
        hs = slice(h * HEAD_W, (h + 1) * HEAD_W)
        q = q_ref[t * tq:(t + 1) * tq, hs]
        return [lax.dot_general(q, kv[2 * g][:, hs], _NT, preferred_element_type=F32) for g in range(n_seg)]

    def row_max(s):
        m = s[0].max(axis=-1, keepdims=True)
        for g in range(1, n_seg):
            m = jnp.maximum(m, s[g].max(axis=-1, keepdims=True))
        return m

    def probs(s, m):
        return [jnp.exp2((s[g] - m) * _EXP2_SCALE).astype(BF16) for g in range(n_seg)]

    def values(g, h):
        v = kv[2 * g + 1][:, h * MLA_V:(h + 1) * MLA_V]
        one_col = jnp.where(lax.broadcasted_iota(jnp.int32, v.shape, 1) == 0, 1.0, 0.0).astype(BF16)
        return jnp.concatenate([v, one_col], axis=1)

    v_aug = {(g, h): values(g, h) for g in range(n_seg) for h in range(heads)}

    def weighted(p, h):
        acc = 0.0
        for g in range(n_seg):
            acc = acc + jnp.dot(p[g], v_aug[g, h], preferred_element_type=F32)
        return acc

    def store(acc, h, t):
        out = acc[:, :MLA_V] / acc[:, MLA_V:MLA_V + 1]
        o_ref[t * tq:(t + 1) * tq, h * MLA_V:(h + 1) * MLA_V] = out.astype(BF16)

    if lookahead >= len(work):
        s = [scores(*w) for w in work]
        m = [row_max(x) for x in s]
        p = [probs(x, y) for x, y in zip(s, m)]
        acc = [weighted(x, w[0]) for x, w in zip(p, work)]
        for x, w in zip(acc, work):
            store(x, *w)
    else:
        s_next = scores(*work[0])
        for idx, (h, t) in enumerate(work):
            s = s_next
            if idx + 1 < len(work):
                s_next = scores(*work[idx + 1])
            store(weighted(probs(s, row_max(s)), h), h, t)


def _attention(q, q_row0, segs, *, n_batch, seq, tq, heads, side=()):
    qb0 = q_row0 // seq
    n_hblk = MLA_HEADS // heads
    n_steps = n_batch * n_hblk
    in_specs = [pl.BlockSpec((seq, heads * HEAD_W), lambda b, h: (qb0 + b, h))]
    args = [q]
    for keys, vals, n_keys in segs:
        in_specs += [pl.BlockSpec((n_keys, heads * HEAD_W), lambda b, h: (b, h)),
                     pl.BlockSpec((n_keys, heads * MLA_V), lambda b, h: (b, h))]
        args += [keys, vals]
    out_specs = [pl.BlockSpec((seq, heads * MLA_V), lambda b, h: (b, h))]
    out_shape = [jax.ShapeDtypeStruct((n_batch * seq, MLA_HEADS * MLA_V), BF16)]
    for w in side:
        rows, cols = w.shape[0] // n_steps, w.shape[1]
        assert rows * n_steps == w.shape[0] and rows % 16 == 0, "side cast needs whole bf16 row tiles per step"
        spec = lambda: pl.BlockSpec((rows, cols), lambda b, h: (b * n_hblk + h, 0))
        in_specs.append(spec())
        out_specs.append(spec())
        out_shape.append(jax.ShapeDtypeStruct(w.shape, BF16))
        args.append(w)
    outs = pl.pallas_call(
        functools.partial(_attn_kernel, n_seg=len(segs), n_side=len(side), heads=heads, tq=tq,
                          lookahead=1 if seq > tq else heads),
        grid=(n_batch, n_hblk),
        in_specs=in_specs, out_specs=out_specs, out_shape=out_shape,
        compiler_params=_cparams(2),
        name=f"mla_attn_{seq}",
    )(*args)
    return outs if side else outs[0]


def _merge_kernel(og_ref, om_ref, ga_ref, gb_ref, x_ref, mod_ref, g_ref, wg_ref, wm_ref, wo_ref, o_ref):
    yg = jnp.dot(og_ref[...], wg_ref[...], preferred_element_type=F32)
    ym = jnp.dot(om_ref[...], wm_ref[...], preferred_element_type=F32)
    m = (jax.nn.sigmoid(ga_ref[...]) * yg + jax.nn.sigmoid(gb_ref[...]) * ym).astype(BF16)
    o_ref[...] = jnp.dot(m, wo_ref[...], preferred_element_type=F32)
    _residual(x_ref, o_ref, g_ref, mod_ref, 1, 1.0)


def _merge(o_gla, o_mla, proj, x, stream, mod, gains, w_gla_out, w_mla_out, w_out):
    tm = MERGE_TM
    tile = lambda c0: pl.BlockSpec((tm, D_MODEL), lambda i: (i, c0 // D_MODEL))
    return pl.pallas_call(
        _merge_kernel,
        grid=(stream.n_tok // tm,),
        in_specs=[tile(0), tile(0), tile(COL_GA), tile(COL_GB), tile(0),
                  pl.BlockSpec((None, N_MOD, D_MODEL), lambda i: (stream.mod_row(i, tm), 0, 0)),
                  _resident((6, D_MODEL)),
                  _resident((D_MODEL, D_MODEL)), _resident((D_MODEL, D_MODEL)), _resident((D_MODEL, D_MODEL))],
        out_specs=tile(0),
        out_shape=jax.ShapeDtypeStruct((stream.n_tok, D_MODEL), F32),
        compiler_params=_cparams(1),
        name=f"mixer_merge_{stream.name}",
    )(o_gla, o_mla, proj, proj, x, mod, gains, w_gla_out, w_mla_out, w_out)


_SWAP = np.arange(MLA_ROPE)
_SWAP = np.where(_SWAP % 32 < 16, _SWAP + 16, _SWAP - 16)


_FF_BLOCKS = D_FF // LANES
_U_PER_STEP = FF_TILE // LANES


def _prep_ffn_in_kernel(g_ref, *refs):
    u_refs, o_ref = refs[:-1], refs[-1]
    j = pl.program_id(0)

    @pl.when(j < N_FF - 1)
    def _():
        o_ref[:, :FF_TILE] = g_ref[...].astype(BF16)
        for q, u_ref in enumerate(u_refs):
            o_ref[:, FF_TILE + q * LANES:FF_TILE + (q + 1) * LANES] = u_ref[...].astype(BF16)

    @pl.when(j == N_FF - 1)
    def _():
        o_ref[:, :FF_LAST] = g_ref[:, :FF_LAST].astype(BF16)
        for q in range(FF_LAST // LANES):
            o_ref[:, FF_LAST + q * LANES:FF_LAST + (q + 1) * LANES] = u_refs[q][...].astype(BF16)
        o_ref[:, 2 * FF_LAST:] = jnp.zeros((D_MODEL, 2 * (FF_TILE - FF_LAST)), BF16)


def _prep_ffn_in(w, name):
    last = 2 * _FF_BLOCKS - 1
    u_spec = lambda q: pl.BlockSpec((D_MODEL, LANES),
                                    lambda j: (0, jnp.minimum(_FF_BLOCKS + _U_PER_STEP * j + q, last)))
    return pl.pallas_call(
        _prep_ffn_in_kernel,
        grid=(N_FF,),
        in_specs=[pl.BlockSpec((D_MODEL, FF_TILE), lambda j: (0, j))] + [u_spec(q) for q in range(_U_PER_STEP)],
        out_specs=pl.BlockSpec((D_MODEL, 2 * FF_TILE), lambda j: (0, j)),
        out_shape=jax.ShapeDtypeStruct((D_MODEL, N_FF * 2 * FF_TILE), BF16),
        compiler_params=_cparams(1),
        name=name,
    )(w, *([w] * _U_PER_STEP))


_FF_FULL = D_FF // FF_TILE
_FF_TAIL = (D_FF - _FF_FULL * FF_TILE) // LANES


def _prep_ffn_out_kernel(w_ref, *refs):
    tail_refs, o_ref = refs[:-1], refs[-1]
    j = pl.program_id(0)

    @pl.when(j < _FF_FULL)
    def _():
        o_ref[...] = w_ref[...].astype(BF16)

    @pl.when(j == _FF_FULL)
    def _():
        for t, t_ref in enumerate(tail_refs):
            o_ref[t * LANES:(t + 1) * LANES, :] = t_ref[...].astype(BF16)
        o_ref[_FF_TAIL * LANES:, :] = jnp.zeros((FF_TILE - _FF_TAIL * LANES, D_MODEL), BF16)


def _prep_ffn_out(w, name):
    tail = lambda t: pl.BlockSpec((LANES, D_MODEL), lambda j: (_FF_FULL * (FF_TILE // LANES) + t, 0))
    return pl.pallas_call(
        _prep_ffn_out_kernel,
        grid=(N_FF,),
        in_specs=[pl.BlockSpec((FF_TILE, D_MODEL), lambda j: (jnp.minimum(j, _FF_FULL - 1), 0))]
        + [tail(t) for t in range(_FF_TAIL)],
        out_specs=pl.BlockSpec((FF_TILE, D_MODEL), lambda j: (j, 0)),
        out_shape=jax.ShapeDtypeStruct((FF_PAD, D_MODEL), BF16),
        compiler_params=_cparams(1),
        name=name,
    )(w, *([w] * _FF_TAIL))


_W_IN_GROUPS = ((0, 6144), (7264, 2048), (9312, 2048), (6176, 512), (6688, 512))
_W_IN_GROUPS_T = ((0, 6144), (7264, 2048), (9312, 2048), (6176, 1024))
_W_IN_ROW_UNIT = 32


def _prep_w_in_kernel(start_ref, w_ref, o_ref):
    del start_ref
    o_ref[...] = w_ref[...].astype(BF16)


def _prep_w_in(w):
    wt = jnp.swapaxes(w, 0, 1)
    tn = PROJ_TN
    starts = np.array([(s + o) // _W_IN_ROW_UNIT for s, n in _W_IN_GROUPS_T for o in range(0, n, tn)], np.int32)
    main = pl.pallas_call(
        _prep_w_in_kernel,
        grid_spec=pltpu.PrefetchScalarGridSpec(
            num_scalar_prefetch=1,
            grid=(PROJ_MAIN // tn,),
            in_specs=[pl.BlockSpec((pl.Element(tn), pl.Element(D_MODEL)),
                                   lambda j, start: (start[j] * _W_IN_ROW_UNIT, 0))],
            out_specs=pl.BlockSpec((tn, D_MODEL), lambda j, start: (j, 0))),
        out_shape=jax.ShapeDtypeStruct((PROJ_MAIN, D_MODEL), BF16),
        compiler_params=_cparams(1),
        name="prep_w_in",
    )(jnp.asarray(starts), wt)
    af, ab, kr = wt[6144:6160], wt[6160:6176], wt[7200:7264]
    small = jnp.concatenate([kr, kr[_SWAP], af, ab,
                             jnp.zeros((PROJ_SMALL - 2 * MLA_ROPE - 2 * GLA_RANK, D_MODEL), F32)], axis=0).astype(BF16)
    return main, small


def _prep_w_uq(w):
    w = w.astype(BF16).reshape(Q_LORA, MLA_HEADS, MLA_NOPE + MLA_ROPE)
    rope = w[..., MLA_NOPE:]
    return jnp.concatenate([w[..., :MLA_NOPE], rope, rope[..., _SWAP]], axis=-1).reshape(Q_LORA, MLA_HEADS * HEAD_W)


def _prep_w_ukv(w):
    w = w.astype(BF16).reshape(KV_LORA, MLA_HEADS, MLA_NOPE + MLA_V)
    return jnp.concatenate([w[..., :MLA_NOPE].reshape(KV_LORA, -1), w[..., MLA_NOPE:].reshape(KV_LORA, -1)], axis=1)


def _prep_w_alpha(w):
    out = jnp.zeros((2, LANES, GLA_HEADS * GLA_DK), BF16)
    out = out.at[0, 0:GLA_RANK].set(w[0].astype(BF16))
    return out.at[1, GLA_RANK:2 * GLA_RANK].set(w[1].astype(BF16))


def _rope_table(tm):
    rows = DEC_SEQ // GRID_W
    row = np.repeat(np.arange(rows), GRID_W).astype(np.float64)
    col = np.tile(np.arange(GRID_W), rows).astype(np.float64)
    half = MLA_ROPE // 2
    inv_freq = ROPE_THETA ** (-np.arange(0, half, 2, dtype=np.float64) / half)
    ang_r, ang_c = row[:, None] * inv_freq, col[:, None] * inv_freq
    cos = np.concatenate([np.cos(ang_r), np.cos(ang_r), np.cos(ang_c), np.cos(ang_c)], axis=1)
    sin = np.concatenate([-np.sin(ang_r), np.sin(ang_r), -np.sin(ang_c), np.sin(ang_c)], axis=1)
    z = np.zeros((DEC_SEQ, LANES - MLA_ROPE))
    lat = np.concatenate([cos, z, sin, z], axis=1)
    ident = np.concatenate([np.ones((tm, MLA_ROPE)), np.zeros((tm, 2 * LANES - MLA_ROPE))], axis=1)
    return jnp.asarray(np.concatenate([ident, lat], axis=0), F32)


def kernel(x_prompt, x_sample, cache_ckv, cache_krope, state_gla_fwd, state_gla_bwd, c, c_ctx, w_ada, b_ada, norm_gains, w_ffn1_in, w_ffn1_out, w_ffn2_in, w_ffn2_out, w_in, w_gla_alpha, b_gla_alpha, gla_norm, w_gla_out, q_norm, kv_norm, w_uq, w_ukv, w_mla_out, w_out):
    c_all = jnp.concatenate([c_ctx[None, :], c, jnp.zeros((8 - 1 - DEC_BATCH, D_MODEL), F32)], axis=0)
    mod = _modulation(c_all, w_ada[0], b_ada[0]).reshape(8, N_MOD, D_MODEL)
    gains = norm_gains[0]

    streams = (_CTX, _LAT)
    x = {_CTX: x_prompt.reshape(N_CTX, D_MODEL), _LAT: x_sample.reshape(N_LAT, D_MODEL)}

    w_gu, w_o = _prep_ffn_in(w_ffn1_in[0], "prep_ffn0_in"), _prep_ffn_out(w_ffn1_out[0], "prep_ffn0_out")
    x = {s: _ffn(x[s], s, mod, gains, w_gu, w_o, sub=0) for s in streams}

    w_main, w_small = _prep_w_in(w_in[0])
    proj, small = {}, {}
    proj[_CTX], small[_CTX], w_o2 = _mixer_proj(x[_CTX], _CTX, mod, gains, w_main, w_small, w_ffn2_out[0], "ffn_out")
    proj[_LAT], small[_LAT], w_gu2 = _mixer_proj(x[_LAT], _LAT, mod, gains, w_main, w_small, w_ffn2_in[0], "ffn_in")

    wa = _prep_w_alpha(w_gla_alpha[0])
    ba = b_gla_alpha[0]
    gn = gla_norm[0].reshape(1, GLA_DV)
    o_gla = {}
    o_gla[_CTX], s_f, s_b = _gla(proj[_CTX], small[_CTX], wa, ba, gn, seq=SEQ, n_batch=BATCH, row0=0,
                                 emit_state=True)
    (o_gla[_LAT],) = _gla(proj[_LAT], small[_LAT], wa, ba, gn, seq=DEC_SEQ, n_batch=DEC_BATCH, row0=0,
                          s0=(state_gla_fwd[:, 0], state_gla_bwd[:, 0]))

    tm = MLA_TM
    rope_tab = _rope_table(tm)
    w_q = _prep_w_uq(w_uq[0])
    q = {s: _q_proj(proj[s], s, q_norm[0].reshape(1, Q_LORA), w_q, rope_tab) for s in streams}
    w_kv = _prep_w_ukv(w_ukv[0])
    kvn = kv_norm[0].reshape(1, KV_LORA)
    ckv_col = COL_CKV // KV_LORA
    k_ctx, v_ctx, ckv_new = _kv_proj(proj[_CTX], ckv_col, small[_CTX], kvn, w_kv, rope_tab, n_rows=N_CTX, row0=0,
                                     rope_row=lambda i: _CTX.rope_row(i, tm), norm=True, emit_ckv=True)
    k_lat, v_lat = _kv_proj(proj[_LAT], ckv_col, small[_LAT], kvn, w_kv, rope_tab, n_rows=N_LAT, row0=0,
                            rope_row=lambda i: _LAT.rope_row(i, tm), norm=True, emit_ckv=False)
    n_past = DEC_BATCH * PAST_LEN
    kr_past = jnp.pad(cache_krope[:, 0].reshape(n_past, MLA_ROPE), ((0, 0), (0, LANES - MLA_ROPE)))
    k_past, v_past = _kv_proj(cache_ckv[:, 0].reshape(n_past, KV_LORA), 0, kr_past, kvn, w_kv, rope_tab,
                              n_rows=n_past, row0=0, rope_row=lambda i: 0 * i, norm=False, emit_ckv=False)
    o_mla = {_CTX: _attention(q[_CTX], 0, [(k_ctx, v_ctx, SEQ)], n_batch=BATCH, seq=SEQ, tq=SEQ, heads=MLA_HEADS)}
    o_mla[_LAT], *w_merge = _attention(q[_LAT], 0, [(k_past, v_past, PAST_LEN), (k_lat, v_lat, DEC_SEQ)],
                                       n_batch=DEC_BATCH, seq=DEC_SEQ, tq=ATT_TQ, heads=ATT_HEADS,
                                       side=(w_gla_out[0], w_mla_out[0], w_out[0]))
    x = {s: _merge(o_gla[s], o_mla[s], proj[s], x[s], s, mod, gains, *w_merge) for s in streams}
    y = {s: _ffn(x[s], s, mod, gains, w_gu2, w_o2, sub=2) for s in streams}

    y_prompt = y[_CTX].reshape(BATCH, SEQ, D_MODEL)
    y_sample = y[_LAT].reshape(DEC_BATCH, DEC_SEQ, D_MODEL)
    new_ckv = ckv_new.reshape(BATCH, 1, SEQ, KV_LORA)
    new_krope = small[_CTX][:, :MLA_ROPE].reshape(BATCH, 1, SEQ, MLA_ROPE)
    return (y_prompt, y_sample, new_ckv, new_krope,
            s_f.reshape(BATCH, 1, GLA_HEADS, GLA_DK, GLA_DV), s_b.reshape(BATCH, 1, GLA_HEADS, GLA_DK, GLA_DV))
```

```python
import functools
from typing import NamedTuple

import numpy as np
import jax
import jax.numpy as jnp
from jax import lax
from jax.experimental import pallas as pl
from jax.experimental.pallas import tpu as pltpu

F32 = jnp.float32
BF16 = jnp.bfloat16

D_MODEL = 2048
BATCH, SEQ = 16, 256
DEC_BATCH, DEC_SEQ = 4, 2048
PAST_LEN = 256
GRID_W = 64
D_FF = 5504
N_MOD = 9
GLA_HEADS, GLA_DK, GLA_DV = 4, 256, 512
GLA_RANK = 16
GLA_TAU = 16.0
GLA_CHUNK = 64
GLA_BLOCK = 256
GLA_BLOCKS_PER_ITER = 4
GLA_CHUNKS_PER_ITER = 8
MLA_HEADS, MLA_NOPE, MLA_ROPE, MLA_V = 16, 128, 64, 128
Q_LORA = KV_LORA = 512
ROPE_THETA = 10000.0
NORM_EPS = 1e-6

N_CTX = BATCH * SEQ
N_LAT = DEC_BATCH * DEC_SEQ
N_TOK = N_CTX + N_LAT

LANES = 128
V7X_VMEM_BYTES = 64 * 1024 * 1024
V7X_VMEM_LIMIT_BYTES = 60000 * 1024

FF_TILE = 512
FF_PAD = -(-D_FF // FF_TILE) * FF_TILE
N_FF = FF_PAD // FF_TILE
FF_LAST = D_FF - (N_FF - 1) * FF_TILE
FFN_TM = 1024
FFN_ROWS = 512
PROJ_TM, PROJ_TN = 1024, 1024
MLA_TM = 1024
MERGE_TM = 256
ATT_TQ = 1024
ATT_HEADS = 1
HEAD_W = 2 * LANES

COL_Q, COL_K, COL_V, COL_R = 0, 1024, 2048, 4096
COL_GA, COL_GB, COL_CQ, COL_CKV = 6144, 8192, 10240, 10752
PROJ_MAIN = 11264
PROJ_SMALL = 256


class _Stream(NamedTuple):
    name: str
    n_batch: int
    seq: int
    mod_base: int
    per_batch_mod: bool
    rope: bool

    @property
    def n_tok(self):
        return self.n_batch * self.seq

    def mod_row(self, tile, tm):
        return self.mod_base + (tile * tm // self.seq if self.per_batch_mod else 0 * tile)

    def rope_row(self, tile, tm):
        return 1 + tile % (self.seq // tm) if self.rope else 0 * tile


_CTX = _Stream("ctx", BATCH, SEQ, 0, False, False)
_LAT = _Stream("lat", DEC_BATCH, DEC_SEQ, 1, True, True)


def _rms(x, gain):
    ms = jnp.mean(x * x, axis=-1, keepdims=True)
    return x * lax.rsqrt(ms + NORM_EPS) * gain


def _silu(x):
    return x * jax.nn.sigmoid(x)


_NT = (((1,), (1,)), ((), ()))
_TN = (((0,), (0,)), ((), ()))


def _cparams(n_axes, vmem_limit_bytes=V7X_VMEM_LIMIT_BYTES):
    return pltpu.CompilerParams(dimension_semantics=("arbitrary",) * n_axes, vmem_limit_bytes=vmem_limit_bytes)


def _resident(shape):
    nd = len(shape)
    return pl.BlockSpec(shape, lambda *_: (0,) * nd, pipeline_mode=pl.Buffered(1))


def _mod_kernel(c_ref, w_ref, b_ref, o_ref):
    s = _silu(c_ref[...]).astype(BF16)
    o_ref[...] = jnp.dot(s, w_ref[...].astype(BF16), preferred_element_type=F32) + b_ref[...]


def _modulation(c_all, w_ada, b_ada):
    n = w_ada.shape[1]
    tn = 1024
    return pl.pallas_call(
        _mod_kernel,
        grid=(n // tn,),
        in_specs=[pl.BlockSpec((8, D_MODEL), lambda j: (0, 0)),
                  pl.BlockSpec((D_MODEL, tn), lambda j: (0, j)),
                  pl.BlockSpec((1, tn), lambda j: (0, j))],
        out_specs=pl.BlockSpec((8, tn), lambda j: (0, j)),
        out_shape=jax.ShapeDtypeStruct((8, n), F32),
        compiler_params=_cparams(1),
        name="adaln_mod",
    )(c_all, w_ada, b_ada.reshape(1, n))


def _row(ref, r):
    return ref[r:r + 1, :]


def _norm_mod(x_ref, h_ref, g_ref, mod_ref, sub):
    h = _rms(x_ref[...], _row(g_ref, 2 * sub)) * (1.0 + _row(mod_ref, 3 * sub + 1)) + _row(mod_ref, 3 * sub)
    h_ref[...] = h.astype(BF16)


def _residual(x_ref, o_ref, g_ref, mod_ref, sub, gate_scale):
    y = _rms(o_ref[...], _row(g_ref, 2 * sub + 1))
    o_ref[...] = x_ref[...] + (gate_scale * _row(mod_ref, 3 * sub + 2)) * y


def _ffn_kernel(*refs, sub, has_side):
    if has_side:
        x_ref, mod_ref, g_ref, wgu_ref, wo_ref, side_ref, o_ref, side_o_ref, h_ref = refs
        side_o_ref[...] = side_ref[...].astype(BF16)
    else:
        x_ref, mod_ref, g_ref, wgu_ref, wo_ref, o_ref, h_ref = refs
    j = pl.program_id(1)

    @pl.when(j == 0)
    def _():
        _norm_mod(x_ref, h_ref, g_ref, mod_ref, sub)
        o_ref[...] = jnp.zeros_like(o_ref)

    def chunk(width):
        for r in range(0, h_ref.shape[0], FFN_ROWS):
            rows = slice(r, r + FFN_ROWS)
            gu = jnp.dot(h_ref[rows, :], wgu_ref[:, :2 * width], preferred_element_type=F32)
            a = (_silu(gu[:, :width]) * gu[:, width:]).astype(BF16)
            o_ref[rows, :] += jnp.dot(a, wo_ref[:width, :], preferred_element_type=F32)

    last = pl.num_programs(1) - 1

    @pl.when(j < last)
    def _():
        chunk(FF_TILE)

    @pl.when(j == last)
    def _():
        chunk(FF_LAST)
        _residual(x_ref, o_ref, g_ref, mod_ref, sub, 0.5)


def _w_in_src_row(s, rows):
    start, first = 0, 0
    for src, n in _W_IN_GROUPS_T:
        here = (src // _W_IN_ROW_UNIT) + (s - first) * (rows // _W_IN_ROW_UNIT)
        start = jnp.where(s >= first, here, start)
        first += n // rows
    return start * _W_IN_ROW_UNIT


def _ffn(x, stream, mod, gains, w_gu, w_o, sub, *, tm=FFN_TM, w_in_t=None):
    n_j = N_FF
    n_steps = stream.n_tok // tm * n_j
    in_specs = [pl.BlockSpec((tm, D_MODEL), lambda i, j: (i, 0)),
                pl.BlockSpec((None, N_MOD, D_MODEL), lambda i, j: (stream.mod_row(i, tm), 0, 0)),
                pl.BlockSpec((6, D_MODEL), lambda i, j: (0, 0)),
                pl.BlockSpec((D_MODEL, 2 * FF_TILE), lambda i, j: (0, j)),
                pl.BlockSpec((FF_TILE, D_MODEL), lambda i, j: (j, 0))]
    out_specs = [pl.BlockSpec((tm, D_MODEL), lambda i, j: (i, 0))]
    out_shape = [jax.ShapeDtypeStruct((stream.n_tok, D_MODEL), F32)]
    args = [x, mod, gains, w_gu, w_o]
    if w_in_t is not None:
        rows = PROJ_MAIN // n_steps
        assert rows * n_steps == PROJ_MAIN and rows % _W_IN_ROW_UNIT == 0
        in_specs.append(pl.BlockSpec((pl.Element(rows), pl.Element(D_MODEL)),
                                     lambda i, j: (_w_in_src_row(i * n_j + j, rows), 0)))
        out_specs.append(pl.BlockSpec((rows, D_MODEL), lambda i, j: (i * n_j + j, 0)))
        out_shape.append(jax.ShapeDtypeStruct((PROJ_MAIN, D_MODEL), BF16))
        args.append(w_in_t)
    outs = pl.pallas_call(
        functools.partial(_ffn_kernel, sub=sub, has_side=w_in_t is not None),
        grid=(stream.n_tok // tm, n_j),
        in_specs=in_specs, out_specs=out_specs, out_shape=out_shape,
        scratch_shapes=[pltpu.VMEM((tm, D_MODEL), BF16)],
        compiler_params=_cparams(2, V7X_VMEM_BYTES - 2 * 1024 * 1024),
        name=f"ffn{sub}_{stream.name}",
    )(*args)
    return outs if w_in_t is not None else outs[0]


def _proj_kernel(x_ref, mod_ref, g_ref, wt_ref, wst_ref, side_ref, o_ref, os_ref, side_o_ref, h_ref, *, n_side):
    i, j = pl.program_id(0), pl.program_id(1)

    @pl.when(j == 0)
    def _():
        _norm_mod(x_ref, h_ref, g_ref, mod_ref, 1)
        os_ref[...] = lax.dot_general(h_ref[...], wst_ref[...], _NT, preferred_element_type=F32)

    o_ref[...] = lax.dot_general(h_ref[...], wt_ref[...], _NT, preferred_element_type=F32)
    step = i * pl.num_programs(1) + j
    side_o_ref[...] = jnp.where(step < n_side, side_ref[...], 0.0).astype(BF16)


def _ffn_in_block_dst(s):
    per_chunk = FF_TILE // LANES
    is_up = (s >= _FF_BLOCKS).astype(jnp.int32)
    b = s - is_up * _FF_BLOCKS
    c, q = b // per_chunk, b % per_chunk
    full = c * 2 * per_chunk + is_up * per_chunk + q
    ragged = (N_FF - 1) * 2 * per_chunk + is_up * (FF_LAST // LANES) + q
    return jnp.where(s >= 2 * _FF_BLOCKS, s, jnp.where(c < N_FF - 1, full, ragged))


def _mixer_proj(x, stream, mod, gains, w_main, w_small, side_w, side_kind):
    tm, tn = PROJ_TM, PROJ_TN
    n_tok = stream.n_tok
    n_j = PROJ_MAIN // tn
    n_steps = n_tok // tm * n_j
    step = lambda i, j: i * n_j + j
    if side_kind == "ffn_in":
        n_side, n_dst = 2 * _FF_BLOCKS, N_FF * 2 * FF_TILE // LANES
        side_in = pl.BlockSpec((D_MODEL, LANES), lambda i, j: (0, jnp.minimum(step(i, j), n_side - 1)))
        side_out = pl.BlockSpec((D_MODEL, LANES), lambda i, j: (0, _ffn_in_block_dst(step(i, j))))
        side_shape = jax.ShapeDtypeStruct((D_MODEL, N_FF * 2 * FF_TILE), BF16)
    else:
        n_side, n_dst = _FF_BLOCKS, FF_PAD // LANES
        side_in = pl.BlockSpec((LANES, D_MODEL), lambda i, j: (jnp.minimum(step(i, j), n_side - 1), 0))
        side_out = pl.BlockSpec((LANES, D_MODEL), lambda i, j: (step(i, j), 0))
        side_shape = jax.ShapeDtypeStruct((FF_PAD, D_MODEL), BF16)
    assert n_steps == n_dst, "the side cast needs exactly one destination block per grid step"
    return pl.pallas_call(
        functools.partial(_proj_kernel, n_side=n_side),
        grid=(n_tok // tm, n_j),
        in_specs=[pl.BlockSpec((tm, D_MODEL), lambda i, j: (i, 0)),
                  pl.BlockSpec((None, N_MOD, D_MODEL), lambda i, j: (stream.mod_row(i, tm), 0, 0)),
                  pl.BlockSpec((6, D_MODEL), lambda i, j: (0, 0)),
                  pl.BlockSpec((tn, D_MODEL), lambda i, j: (j, 0)),
                  pl.BlockSpec((PROJ_SMALL, D_MODEL), lambda i, j: (0, 0)),
                  side_in],
        out_specs=[pl.BlockSpec((tm, tn), lambda i, j: (i, j)),
                   pl.BlockSpec((tm, PROJ_SMALL), lambda i, j: (i, 0)),
                   side_out],
        out_shape=[jax.ShapeDtypeStruct((n_tok, PROJ_MAIN), F32),
                   jax.ShapeDtypeStruct((n_tok, PROJ_SMALL), F32),
                   side_shape],
        scratch_shapes=[pltpu.VMEM((tm, D_MODEL), BF16)],
        compiler_params=_cparams(2),
        name=f"mixer_proj_{stream.name}",
    )(x, mod, gains, w_main, w_small, side_w)


def _log_sigmoid(x):
    return jnp.minimum(x, 0.0) - jnp.log(1.0 + jnp.exp(-jnp.abs(x)))


def _gla_kernel(*refs, seq, has_s0, emit_state):
    q_ref, k_ref, v_ref, r_ref, a_ref, wa_ref, ba_ref, gn_ref = refs[:8]
    pos = 8
    if has_s0:
        s0_refs = refs[pos:pos + 2]
        pos += 2
    o_ref = refs[pos]
    pos += 1
    if emit_state:
        s_out_refs = refs[pos:pos + 2]
        pos += 2
    qd_scr, ke_scr, vb_scr, dec_scr, o_scr, st_scr = refs[pos:pos + 6]

    c, blk_rows = GLA_CHUNK, GLA_BLOCK
    cpb = blk_rows // c
    nc = seq // c
    for d in range(2):
        st_scr[d] = s0_refs[d][...] if has_s0 else jnp.zeros((GLA_DK, GLA_DV), F32)

    row = lax.broadcasted_iota(jnp.int32, (blk_rows, blk_rows), 0)
    col = lax.broadcasted_iota(jnp.int32, (blk_rows, blk_rows), 1)
    same_chunk = (row // c) == (col // c)
    masks = (same_chunk & (col <= row), same_chunk & (col >= row))

    tri = [jnp.where(masks[d], 1.0, 0.0).astype(BF16) for d in range(2)]
    n_blocks = seq // blk_rows
    blocks_per_iter = min(GLA_BLOCKS_PER_ITER, n_blocks)

    def bulk(it, carry):
        blks = [it * blocks_per_iter + j for j in range(blocks_per_iter)]
        rows = [pl.ds(pl.multiple_of(blk * blk_rows, blk_rows), blk_rows) for blk in blks]
        nb = range(blocks_per_iter)
        a_in = [a_ref[rows[j], :].astype(BF16) for j in nb]
        q = [q_ref[rows[j], :] * (GLA_DK ** -0.5) for j in nb]
        k = [k_ref[rows[j], :] for j in nb]
        vb = [v_ref[rows[j], :].astype(BF16) for j in nb]
        units = [(j, d) for j in nb for d in range(2)]
        x = {(j, d): jnp.dot(a_in[j], wa_ref[d], preferred_element_type=F32) + ba_ref[d:d + 1, :] for j, d in units}
        la = {u: _log_sigmoid(x[u]) * (1.0 / GLA_TAU) for u in units}
        la1 = {u: la[u].astype(BF16) for u in units}
        rem = {u: la[u] - la1[u].astype(F32) for u in units}
        la2 = {u: rem[u].astype(BF16) for u in units}
        la3 = {u: (rem[u] - la2[u].astype(F32)).astype(BF16) for u in units}
        b = {(j, d): jnp.dot(tri[d], la1[j, d], preferred_element_type=F32)
             + jnp.dot(tri[d], la2[j, d], preferred_element_type=F32)
             + jnp.dot(tri[d], la3[j, d], preferred_element_type=F32) for j, d in units}
        b3 = {u: b[u].reshape(cpb, c, GLA_DK) for u in units}
        b_last = {(j, d): b3[j, d][:, c - 1:c, :] if d == 0 else b3[j, d][:, 0:1, :] for j, d in units}
        dec = {u: jnp.exp(b_last[u]) for u in units}
        dec_rows = {u: jnp.broadcast_to(dec[u], (cpb, c, GLA_DK)).reshape(blk_rows, GLA_DK) for u in units}
        q_dec = {(j, d): (q[j] * jnp.exp(b[j, d])).astype(BF16) for j, d in units}
        k_inv = {(j, d): k[j] * jnp.exp(-b[j, d]) for j, d in units}
        k_end = {u: (k_inv[u] * dec_rows[u]).astype(BF16) for u in units}
        a = {u: lax.dot_general(q_dec[u], k_inv[u].astype(BF16), _NT, preferred_element_type=F32) for u in units}
        a = {(j, d): jnp.where(masks[d], a[j, d], 0.0).astype(BF16) for j, d in units}
        o = {(j, d): jnp.dot(a[j, d], vb[j], preferred_element_type=F32) for j, d in units}
        for j in nb:
            vb_scr[rows[j], :] = vb[j]
        for j, d in units:
            o_scr[d, rows[j], :] = o[j, d]
            qd_scr[d, rows[j], :] = q_dec[j, d]
            ke_scr[d, rows[j], :] = k_end[j, d]
            dec8 = pl.ds(pl.multiple_of(blks[j] * (cpb * 8), cpb * 8), cpb * 8)
            dec_scr[d, dec8, :] = jnp.broadcast_to(dec[j, d], (cpb, 8, GLA_DK)).reshape(cpb * 8, GLA_DK)
        return carry

    lax.fori_loop(0, n_blocks // blocks_per_iter, bulk, 0)

    cpi = min(GLA_CHUNKS_PER_ITER, nc)

    def recur(i, carry):
        steps = [(u, d) for u in range(cpi) for d in range(2)]
        chunk = {(u, d): cpi * i + u if d == 0 else nc - 1 - (cpi * i + u) for u, d in steps}
        rows = {s: pl.ds(pl.multiple_of(chunk[s] * c, c), c) for s in steps}
        dec_rows = {s: pl.ds(pl.multiple_of(chunk[s] * 8, 8), 8) for s in steps}
        kv = {(u, d): lax.dot_general(ke_scr[d, rows[u, d], :], vb_scr[rows[u, d], :], _TN,
                                      preferred_element_type=F32) for u, d in steps}
        dec_col = {s: jnp.broadcast_to(dec_scr[s[1], dec_rows[s], :][0:1, :], (LANES, GLA_DK)).T for s in steps}
        dec_full = {s: jnp.concatenate([dec_col[s]] * (GLA_DV // LANES), axis=1) for s in steps}
        s_t = [st_scr[d] for d in range(2)]
        for u, d in steps:
            inter = jnp.dot(qd_scr[d, rows[u, d], :], s_t[d].astype(BF16), preferred_element_type=F32)
            o_scr[d, rows[u, d], :] += inter
            s_t[d] = s_t[d] * dec_full[u, d] + kv[u, d]
        for d in range(2):
            st_scr[d] = s_t[d]
        return carry

    lax.fori_loop(0, nc // cpi, recur, 0)

    def finish(i, carry):
        rows = pl.ds(pl.multiple_of(i * blk_rows, blk_rows), blk_rows)
        o = _rms(o_scr[0, rows, :] + o_scr[1, rows, :], gn_ref[...])
        o_ref[rows, :] = (o * _silu(r_ref[rows, :])).astype(BF16)
        return carry

    lax.fori_loop(0, seq // blk_rows, finish, 0)
    if emit_state:
        for d in range(2):
            s_out_refs[d][...] = st_scr[d]


def _gla(proj, small, wa, ba, gn, *, seq, n_batch, row0, s0=None, emit_state=False):
    rb = row0 // seq
    dk, dv = GLA_DK, GLA_DV
    in_specs = [pl.BlockSpec((seq, dk), lambda b, h: (rb + b, COL_Q // dk + h)),
                pl.BlockSpec((seq, dk), lambda b, h: (rb + b, COL_K // dk + h)),
                pl.BlockSpec((seq, dv), lambda b, h: (rb + b, COL_V // dv + h)),
                pl.BlockSpec((seq, dv), lambda b, h: (rb + b, COL_R // dv + h)),
                pl.BlockSpec((seq, LANES), lambda b, h: (rb + b, 1)),
                pl.BlockSpec((2, LANES, dk), lambda b, h: (0, 0, h)),
                pl.BlockSpec((2, dk), lambda b, h: (0, h)),
                pl.BlockSpec((1, dv), lambda b, h: (0, 0))]
    args = [proj, proj, proj, proj, small, wa, ba, gn]
    state_spec = pl.BlockSpec((None, None, dk, dv), lambda b, h: (b, h, 0, 0))
    if s0 is not None:
        in_specs += [state_spec, state_spec]
        args += list(s0)
    out_specs = [pl.BlockSpec((seq, dv), lambda b, h: (b, h))]
    out_shape = [jax.ShapeDtypeStruct((n_batch * seq, GLA_HEADS * dv), BF16)]
    if emit_state:
        out_specs += [state_spec, state_spec]
        out_shape += [jax.ShapeDtypeStruct((n_batch, GLA_HEADS, dk, dv), F32)] * 2
    return pl.pallas_call(
        functools.partial(_gla_kernel, seq=seq, has_s0=s0 is not None, emit_state=emit_state),
        grid=(n_batch, GLA_HEADS),
        in_specs=in_specs, out_specs=out_specs, out_shape=out_shape,
        scratch_shapes=[pltpu.VMEM((2, seq, dk), BF16),
                        pltpu.VMEM((2, seq, dk), BF16),
                        pltpu.VMEM((seq, dv), BF16),
                        pltpu.VMEM((2, seq // GLA_CHUNK * 8, dk), F32),
                        pltpu.VMEM((2, seq, dv), F32),
                        pltpu.VMEM((2, dk, dv), F32)],
        compiler_params=_cparams(2),
        name=f"gla_{seq}",
    )(*args)


def _rope_block(x, cs_ref):
    return x * cs_ref[:, :LANES] + pltpu.roll(x, MLA_ROPE, axis=1) * cs_ref[:, LANES:]


def _q_kernel(cq_ref, qn_ref, w_ref, cs_ref, o_ref):
    n = _rms(cq_ref[...], qn_ref[...]).astype(BF16)
    for h in range(MLA_HEADS):
        q = jnp.dot(n, w_ref[:, h * HEAD_W:(h + 1) * HEAD_W], preferred_element_type=F32)
        o_ref[:, h * HEAD_W:h * HEAD_W + LANES] = q[:, :LANES].astype(BF16)
        o_ref[:, h * HEAD_W + LANES:(h + 1) * HEAD_W] = _rope_block(q[:, LANES:], cs_ref).astype(BF16)


def _q_proj(proj, stream, q_norm, w_q, rope_tab):
    tm = MLA_TM
    return pl.pallas_call(
        _q_kernel,
        grid=(stream.n_tok // tm,),
        in_specs=[pl.BlockSpec((tm, Q_LORA), lambda i: (i, COL_CQ // Q_LORA)),
                  _resident((1, Q_LORA)),
                  _resident((Q_LORA, MLA_HEADS * HEAD_W)),
                  pl.BlockSpec((tm, 2 * LANES), lambda i: (stream.rope_row(i, tm), 0))],
        out_specs=pl.BlockSpec((tm, MLA_HEADS * HEAD_W), lambda i: (i, 0)),
        out_shape=jax.ShapeDtypeStruct((stream.n_tok, MLA_HEADS * HEAD_W), BF16),
        compiler_params=_cparams(1),
        name=f"mla_q_{stream.name}",
    )(proj, q_norm, w_q, rope_tab)


def _kv_kernel(*refs, norm, emit_ckv):
    ckv_ref, kn_ref, kr_ref, w_ref, cs_ref, k_ref, v_ref = refs[:7]
    c = ckv_ref[...]
    if norm:
        c = _rms(c, kn_ref[...])
    if emit_ckv:
        refs[7][...] = c
    cb = c.astype(BF16)
    nk = MLA_HEADS * MLA_NOPE
    kr = _rope_block(kr_ref[...], cs_ref).astype(BF16)
    v_ref[...] = jnp.dot(cb, w_ref[:, nk:], preferred_element_type=F32).astype(BF16)
    for h0 in range(0, MLA_HEADS, 2):
        kn = jnp.dot(cb, w_ref[:, h0 * MLA_NOPE:(h0 + 2) * MLA_NOPE], preferred_element_type=F32).astype(BF16)
        for h in (h0, h0 + 1):
            k_ref[:, h * HEAD_W:h * HEAD_W + LANES] = kn[:, (h - h0) * MLA_NOPE:(h - h0 + 1) * MLA_NOPE]
            k_ref[:, h * HEAD_W + LANES:(h + 1) * HEAD_W] = kr


def _kv_proj(ckv_src, ckv_col, kr_src, kv_norm, w_kv, rope_tab, *, n_rows, row0, rope_row, norm, emit_ckv):
    tm = MLA_TM
    r0 = row0 // tm
    out_specs = [pl.BlockSpec((tm, MLA_HEADS * HEAD_W), lambda i: (i, 0)),
                 pl.BlockSpec((tm, MLA_HEADS * MLA_V), lambda i: (i, 0))]
    out_shape = [jax.ShapeDtypeStruct((n_rows, MLA_HEADS * HEAD_W), BF16),
                 jax.ShapeDtypeStruct((n_rows, MLA_HEADS * MLA_V), BF16)]
    if emit_ckv:
        out_specs.append(pl.BlockSpec((tm, KV_LORA), lambda i: (i, 0)))
        out_shape.append(jax.ShapeDtypeStruct((n_rows, KV_LORA), F32))
    return pl.pallas_call(
        functools.partial(_kv_kernel, norm=norm, emit_ckv=emit_ckv),
        grid=(n_rows // tm,),
        in_specs=[pl.BlockSpec((tm, KV_LORA), lambda i: (r0 + i, ckv_col)),
                  _resident((1, KV_LORA)),
                  pl.BlockSpec((tm, LANES), lambda i: (r0 + i, 0)),
                  _resident((KV_LORA, MLA_HEADS * (MLA_NOPE + MLA_V))),
                  pl.BlockSpec((tm, 2 * LANES), lambda i: (rope_row(i), 0))],
        out_specs=out_specs, out_shape=out_shape,
        compiler_params=_cparams(1),
        name=f"mla_kv_{row0}_{n_rows}",
    )(ckv_src, kv_norm, kr_src, w_kv, rope_tab)


_EXP2_SCALE = float((MLA_NOPE + MLA_ROPE) ** -0.5 * np.log2(np.e))


def _attn_kernel(*refs, n_seg, n_side, heads, tq, lookahead):
    q_ref = refs[0]
    kv = refs[1:1 + 2 * n_seg]
    side_in = refs[1 + 2 * n_seg:1 + 2 * n_seg + n_side]
    o_ref = refs[1 + 2 * n_seg + n_side]
    side_out = refs[2 + 2 * n_seg + n_side:]
    for src, dst in zip(side_in, side_out):
        dst[...] = src[...].astype(BF16)
    nt = q_ref.shape[0] // tq
    work = [(h, t) for h in range(heads) for t in range(nt)]

    def scores(h, t):
        hs = slice(h * HEAD_W, (h + 1) * HEAD_W)
        q = q_ref[t * tq:(t + 1) * tq, hs]
        return [lax.dot_general(q, kv[2 * g][:, hs], _NT, preferred_element_type=F32) for g in range(n_seg)]

    def row_max(s):
        m = s[0].max(axis=-1, keepdims=True)
        for g in range(1, n_seg):
            m = jnp.maximum(m, s[g].max(axis=-1, keepdims=True))
        return m

    def probs(s, m):
        return [jnp.exp2((s[g] - m) * _EXP2_SCALE).astype(BF16) for g in range(n_seg)]

    def values(g, h):
        v = kv[2 * g + 1][:, h * MLA_V:(h + 1) * MLA_V]
        one_col = jnp.where(lax.broadcasted_iota(jnp.int32, v.shape, 1) == 0, 1.0, 0.0).astype(BF16)
        return jnp.concatenate([v, one_col], axis=1)

    v_aug = {(g, h): values(g, h) for g in range(n_seg) for h in range(heads)}

    def weighted(p, h):
        acc = 0.0
        for g in range(n_seg):
            acc = acc + jnp.dot(p[g], v_aug[g, h], preferred_element_type=F32)
        return acc

    def store(acc, h, t):
        out = acc[:, :MLA_V] / acc[:, MLA_V:MLA_V + 1]
        o_ref[t * tq:(t + 1) * tq, h * MLA_V:(h + 1) * MLA_V] = out.astype(BF16)

    if lookahead >= len(work):
        s = [scores(*w) for w in work]
        m = [row_max(x) for x in s]
        p = [probs(x, y) for x, y in zip(s, m)]
        acc = [weighted(x, w[0]) for x, w in zip(p, work)]
        for x, w in zip(acc, work):
            store(x, *w)
    else:
        s_next = scores(*work[0])
        for idx, (h, t) in enumerate(work):
            s = s_next
            if idx + 1 < len(work):
                s_next = scores(*work[idx + 1])
            store(weighted(probs(s, row_max(s)), h), h, t)


def _attention(q, q_row0, segs, *, n_batch, seq, tq, heads, side=()):
    qb0 = q_row0 // seq
    n_hblk = MLA_HEADS // heads
    n_steps = n_batch * n_hblk
    in_specs = [pl.BlockSpec((seq, heads * HEAD_W), lambda b, h: (qb0 + b, h))]
    args = [q]
    for keys, vals, n_keys in segs:
        in_specs += [pl.BlockSpec((n_keys, heads * HEAD_W), lambda b, h: (b, h)),
                     pl.BlockSpec((n_keys, heads * MLA_V), lambda b, h: (b, h))]
        args += [keys, vals]
    out_specs = [pl.BlockSpec((seq, heads * MLA_V), lambda b, h: (b, h))]
    out_shape = [jax.ShapeDtypeStruct((n_batch * seq, MLA_HEADS * MLA_V), BF16)]
    for w in side:
        rows, cols = w.shape[0] // n_steps, w.shape[1]
        assert rows * n_steps == w.shape[0] and rows % 16 == 0, "side cast needs whole bf16 row tiles per step"
        spec = lambda: pl.BlockSpec((rows, cols), lambda b, h: (b * n_hblk + h, 0))
        in_specs.append(spec())
        out_specs.append(spec())
        out_shape.append(jax.ShapeDtypeStruct(w.shape, BF16))
        args.append(w)
    outs = pl.pallas_call(
        functools.partial(_attn_kernel, n_seg=len(segs), n_side=len(side), heads=heads, tq=tq,
                          lookahead=1 if seq > tq else heads),
        grid=(n_batch, n_hblk),
        in_specs=in_specs, out_specs=out_specs, out_shape=out_shape,
        compiler_params=_cparams(2),
        name=f"mla_attn_{seq}",
    )(*args)
    return outs if side else outs[0]


def _merge_kernel(og_ref, om_ref, ga_ref, gb_ref, x_ref, mod_ref, g_ref, wg_ref, wm_ref, wo_ref, o_ref):
    yg = jnp.dot(og_ref[...], wg_ref[...], preferred_element_type=F32)
    ym = jnp.dot(om_ref[...], wm_ref[...], preferred_element_type=F32)
    m = (jax.nn.sigmoid(ga_ref[...]) * yg + jax.nn.sigmoid(gb_ref[...]) * ym).astype(BF16)
    o_ref[...] = jnp.dot(m, wo_ref[...], preferred_element_type=F32)
    _residual(x_ref, o_ref, g_ref, mod_ref, 1, 1.0)


def _merge(o_gla, o_mla, proj, x, stream, mod, gains, w_gla_out, w_mla_out, w_out):
    tm = MERGE_TM
    tile = lambda c0: pl.BlockSpec((tm, D_MODEL), lambda i: (i, c0 // D_MODEL))
    return pl.pallas_call(
        _merge_kernel,
        grid=(stream.n_tok // tm,),
        in_specs=[tile(0), tile(0), tile(COL_GA), tile(COL_GB), tile(0),
                  pl.BlockSpec((None, N_MOD, D_MODEL), lambda i: (stream.mod_row(i, tm), 0, 0)),
                  _resident((6, D_MODEL)),
                  _resident((D_MODEL, D_MODEL)), _resident((D_MODEL, D_MODEL)), _resident((D_MODEL, D_MODEL))],
        out_specs=tile(0),
        out_shape=jax.ShapeDtypeStruct((stream.n_tok, D_MODEL), F32),
        compiler_params=_cparams(1),
        name=f"mixer_merge_{stream.name}",
    )(o_gla, o_mla, proj, proj, x, mod, gains, w_gla_out, w_mla_out, w_out)


_SWAP = np.arange(MLA_ROPE)
_SWAP = np.where(_SWAP % 32 < 16, _SWAP + 16, _SWAP - 16)


_FF_BLOCKS = D_FF // LANES
_U_PER_STEP = FF_TILE // LANES


def _prep_ffn_in_kernel(g_ref, *refs):
    u_refs, o_ref = refs[:-1], refs[-1]
    j = pl.program_id(0)

    @pl.when(j < N_FF - 1)
    def _():
        o_ref[:, :FF_TILE] = g_ref[...].astype(BF16)
        for q, u_ref in enumerate(u_refs):
            o_ref[:, FF_TILE + q * LANES:FF_TILE + (q + 1) * LANES] = u_ref[...].astype(BF16)

    @pl.when(j == N_FF - 1)
    def _():
        o_ref[:, :FF_LAST] = g_ref[:, :FF_LAST].astype(BF16)
        for q in range(FF_LAST // LANES):
            o_ref[:, FF_LAST + q * LANES:FF_LAST + (q + 1) * LANES] = u_refs[q][...].astype(BF16)
        o_ref[:, 2 * FF_LAST:] = jnp.zeros((D_MODEL, 2 * (FF_TILE - FF_LAST)), BF16)


def _prep_ffn_in(w, name):
    last = 2 * _FF_BLOCKS - 1
    u_spec = lambda q: pl.BlockSpec((D_MODEL, LANES),
                                    lambda j: (0, jnp.minimum(_FF_BLOCKS + _U_PER_STEP * j + q, last)))
    return pl.pallas_call(
        _prep_ffn_in_kernel,
        grid=(N_FF,),
        in_specs=[pl.BlockSpec((D_MODEL, FF_TILE), lambda j: (0, j))] + [u_spec(q) for q in range(_U_PER_STEP)],
        out_specs=pl.BlockSpec((D_MODEL, 2 * FF_TILE), lambda j: (0, j)),
        out_shape=jax.ShapeDtypeStruct((D_MODEL, N_FF * 2 * FF_TILE), BF16),
        compiler_params=_cparams(1),
        name=name,
    )(w, *([w] * _U_PER_STEP))


_FF_FULL = D_FF // FF_TILE
_FF_TAIL = (D_FF - _FF_FULL * FF_TILE) // LANES


def _prep_ffn_out_kernel(w_ref, *refs):
    tail_refs, o_ref = refs[:-1], refs[-1]
    j = pl.program_id(0)

    @pl.when(j < _FF_FULL)
    def _():
        o_ref[...] = w_ref[...].astype(BF16)

    @pl.when(j == _FF_FULL)
    def _():
        for t, t_ref in enumerate(tail_refs):
            o_ref[t * LANES:(t + 1) * LANES, :] = t_ref[...].astype(BF16)
        o_ref[_FF_TAIL * LANES:, :] = jnp.zeros((FF_TILE - _FF_TAIL * LANES, D_MODEL), BF16)


def _prep_ffn_out(w, name):
    tail = lambda t: pl.BlockSpec((LANES, D_MODEL), lambda j: (_FF_FULL * (FF_TILE // LANES) + t, 0))
    return pl.pallas_call(
        _prep_ffn_out_kernel,
        grid=(N_FF,),
        in_specs=[pl.BlockSpec((FF_TILE, D_MODEL), lambda j: (jnp.minimum(j, _FF_FULL - 1), 0))]
        + [tail(t) for t in range(_FF_TAIL)],
        out_specs=pl.BlockSpec((FF_TILE, D_MODEL), lambda j: (j, 0)),
        out_shape=jax.ShapeDtypeStruct((FF_PAD, D_MODEL), BF16),
        compiler_params=_cparams(1),
        name=name,
    )(w, *([w] * _FF_TAIL))


_W_IN_GROUPS = ((0, 6144), (7264, 2048), (9312, 2048), (6176, 512), (6688, 512))
_W_IN_GROUPS_T = ((0, 6144), (7264, 2048), (9312, 2048), (6176, 1024))
_W_IN_ROW_UNIT = 32


def _prep_w_in_small(wt):
    af, ab, kr = wt[6144:6160], wt[6160:6176], wt[7200:7264]
    return jnp.concatenate([kr, kr[_SWAP], af, ab,
                            jnp.zeros((PROJ_SMALL - 2 * MLA_ROPE - 2 * GLA_RANK, D_MODEL), F32)], axis=0).astype(BF16)


def _prep_w_uq(w):
    w = w.astype(BF16).reshape(Q_LORA, MLA_HEADS, MLA_NOPE + MLA_ROPE)
    rope = w[..., MLA_NOPE:]
    return jnp.concatenate([w[..., :MLA_NOPE], rope, rope[..., _SWAP]], axis=-1).reshape(Q_LORA, MLA_HEADS * HEAD_W)


def _prep_w_ukv(w):
    w = w.astype(BF16).reshape(KV_LORA, MLA_HEADS, MLA_NOPE + MLA_V)
    return jnp.concatenate([w[..., :MLA_NOPE].reshape(KV_LORA, -1), w[..., MLA_NOPE:].reshape(KV_LORA, -1)], axis=1)


def _prep_w_alpha(w):
    out = jnp.zeros((2, LANES, GLA_HEADS * GLA_DK), BF16)
    out = out.at[0, 0:GLA_RANK].set(w[0].astype(BF16))
    return out.at[1, GLA_RANK:2 * GLA_RANK].set(w[1].astype(BF16))


def _rope_table(tm):
    rows = DEC_SEQ // GRID_W
    row = np.repeat(np.arange(rows), GRID_W).astype(np.float64)
    col = np.tile(np.arange(GRID_W), rows).astype(np.float64)
    half = MLA_ROPE // 2
    inv_freq = ROPE_THETA ** (-np.arange(0, half, 2, dtype=np.float64) / half)
    ang_r, ang_c = row[:, None] * inv_freq, col[:, None] * inv_freq
    cos = np.concatenate([np.cos(ang_r), np.cos(ang_r), np.cos(ang_c), np.cos(ang_c)], axis=1)
    sin = np.concatenate([-np.sin(ang_r), np.sin(ang_r), -np.sin(ang_c), np.sin(ang_c)], axis=1)
    z = np.zeros((DEC_SEQ, LANES - MLA_ROPE))
    lat = np.concatenate([cos, z, sin, z], axis=1)
    ident = np.concatenate([np.ones((tm, MLA_ROPE)), np.zeros((tm, 2 * LANES - MLA_ROPE))], axis=1)
    return jnp.asarray(np.concatenate([ident, lat], axis=0), F32)


def kernel(x_prompt, x_sample, cache_ckv, cache_krope, state_gla_fwd, state_gla_bwd, c, c_ctx, w_ada, b_ada, norm_gains, w_ffn1_in, w_ffn1_out, w_ffn2_in, w_ffn2_out, w_in, w_gla_alpha, b_gla_alpha, gla_norm, w_gla_out, q_norm, kv_norm, w_uq, w_ukv, w_mla_out, w_out):
    c_all = jnp.concatenate([c_ctx[None, :], c, jnp.zeros((8 - 1 - DEC_BATCH, D_MODEL), F32)], axis=0)
    mod = _modulation(c_all, w_ada[0], b_ada[0]).reshape(8, N_MOD, D_MODEL)
    gains = norm_gains[0]

    streams = (_CTX, _LAT)
    x = {_CTX: x_prompt.reshape(N_CTX, D_MODEL), _LAT: x_sample.reshape(N_LAT, D_MODEL)}

    w_gu, w_o = _prep_ffn_in(w_ffn1_in[0], "prep_ffn0_in"), _prep_ffn_out(w_ffn1_out[0], "prep_ffn0_out")
    wt = jnp.swapaxes(w_in[0], 0, 1)
    x_ctx, w_main = _ffn(x[_CTX], _CTX, mod, gains, w_gu, w_o, sub=0, tm=FFN_TM // 2, w_in_t=wt)
    x = {_CTX: x_ctx, _LAT: _ffn(x[_LAT], _LAT, mod, gains, w_gu, w_o, sub=0)}
    w_small = _prep_w_in_small(wt)
    proj, small = {}, {}
    proj[_CTX], small[_CTX], w_o2 = _mixer_proj(x[_CTX], _CTX, mod, gains, w_main, w_small, w_ffn2_out[0], "ffn_out")
    proj[_LAT], small[_LAT], w_gu2 = _mixer_proj(x[_LAT], _LAT, mod, gains, w_main, w_small, w_ffn2_in[0], "ffn_in")

    wa = _prep_w_alpha(w_gla_alpha[0])
    ba = b_gla_alpha[0]
    gn = gla_norm[0].reshape(1, GLA_DV)
    o_gla = {}
    o_gla[_CTX], s_f, s_b = _gla(proj[_CTX], small[_CTX], wa, ba, gn, seq=SEQ, n_batch=BATCH, row0=0,
                                 emit_state=True)
    (o_gla[_LAT],) = _gla(proj[_LAT], small[_LAT], wa, ba, gn, seq=DEC_SEQ, n_batch=DEC_BATCH, row0=0,
                          s0=(state_gla_fwd[:, 0], state_gla_bwd[:, 0]))

    tm = MLA_TM
    rope_tab = _rope_table(tm)
    w_q = _prep_w_uq(w_uq[0])
    q = {s: _q_proj(proj[s], s, q_norm[0].reshape(1, Q_LORA), w_q, rope_tab) for s in streams}
    w_kv = _prep_w_ukv(w_ukv[0])
    kvn = kv_norm[0].reshape(1, KV_LORA)
    ckv_col = COL_CKV // KV_LORA
    k_ctx, v_ctx, ckv_new = _kv_proj(proj[_CTX], ckv_col, small[_CTX], kvn, w_kv, rope_tab, n_rows=N_CTX, row0=0,
                                     rope_row=lambda i: _CTX.rope_row(i, tm), norm=True, emit_ckv=True)
    k_lat, v_lat = _kv_proj(proj[_LAT], ckv_col, small[_LAT], kvn, w_kv, rope_tab, n_rows=N_LAT, row0=0,
                            rope_row=lambda i: _LAT.rope_row(i, tm), norm=True, emit_ckv=False)
    n_past = DEC_BATCH * PAST_LEN
    kr_past = jnp.pad(cache_krope[:, 0].reshape(n_past, MLA_ROPE), ((0, 0), (0, LANES - MLA_ROPE)))
    k_past, v_past = _kv_proj(cache_ckv[:, 0].reshape(n_past, KV_LORA), 0, kr_past, kvn, w_kv, rope_tab,
                              n_rows=n_past, row0=0, rope_row=lambda i: 0 * i, norm=False, emit_ckv=False)
    o_mla = {_CTX: _attention(q[_CTX], 0, [(k_ctx, v_ctx, SEQ)], n_batch=BATCH, seq=SEQ, tq=SEQ, heads=MLA_HEADS)}
    o_mla[_LAT], *w_merge = _attention(q[_LAT], 0, [(k_past, v_past, PAST_LEN), (k_lat, v_lat, DEC_SEQ)],
                                       n_batch=DEC_BATCH, seq=DEC_SEQ, tq=ATT_TQ, heads=ATT_HEADS,
                                       side=(w_gla_out[0], w_mla_out[0], w_out[0]))
    x = {s: _merge(o_gla[s], o_mla[s], proj[s], x[s], s, mod, gains, *w_merge) for s in streams}
    y = {s: _ffn(x[s], s, mod, gains, w_gu2, w_o2, sub=2) for s in streams}

    y_prompt = y[_CTX].reshape(BATCH, SEQ, D_MODEL)
    y_sample = y[_LAT].reshape(DEC_BATCH, DEC_SEQ, D_MODEL)
    new_ckv = ckv_new.reshape(BATCH, 1, SEQ, KV_LORA)
    new_krope = small[_CTX][:, :MLA_ROPE].reshape(BATCH, 1, SEQ, MLA_ROPE)
    return (y_prompt, y_sample, new_ckv, new_krope,
            s_f.reshape(BATCH, 1, GLA_HEADS, GLA_DK, GLA_DV), s_b.reshape(BATCH, 1, GLA_HEADS, GLA_DK, GLA_DV))
```
